```python
import jax, jax.numpy as jnp
from jax import lax
import numpy as np

D_MODEL = 1024
BATCH = 32
SEQ = 256
DEPTH = 4
DEC_BATCH = 8
DEC_SEQ = 2048
PAST_LEN = 256

GRID_W = 64
N_EVEN = (DEPTH + 1) // 2
N_ODD = DEPTH // 2
ROPE_BASE = 10000.0
EPS = 1e-6
Q_BLOCK = 128
NEG_INF = -1e30

MLA_HEADS = 8
MLA_NOPE = 64
MLA_ROPE = 32
MLA_V = 64
MLA_Q_RANK = 384
MLA_KV_RANK = 256
MLA_WIDTH = MLA_HEADS * MLA_V
MLA_SCALE = (MLA_NOPE + MLA_ROPE) ** -0.5
POOL_WINDOWS = (2, 4, 8, 16)
POOL_GROUPS = 4
POOL_GROUP_W = D_MODEL // 8
POOL_WIDTH = POOL_GROUPS * POOL_GROUP_W
A_IN = MLA_Q_RANK + MLA_KV_RANK + MLA_ROPE + MLA_WIDTH + 2 * POOL_WIDTH
A_MIX = MLA_WIDTH + POOL_WIDTH
SWA_HEADS = 16
SWA_KV_HEADS = 4
SWA_HEAD_DIM = 64
SWA_WINDOW = 128
SWA_WIDTH = SWA_HEADS * SWA_HEAD_DIM
SWA_KV_W = SWA_KV_HEADS * SWA_HEAD_DIM
SWA_SCALE = SWA_HEAD_DIM ** -0.5
C_IN = 2 * SWA_WIDTH + 2 * SWA_KV_W

kernel_name = "hybrid_diffusion_mla_pool_swa_step"


def rmsnorm(x, g):
    xf = x.astype(jnp.float32)
    y = xf * lax.rsqrt(jnp.mean(xf * xf, axis=-1, keepdims=True) + EPS)
    return (y * g.astype(jnp.float32)).astype(x.dtype)


def ada_mod(cvec, w, b):
    m = jax.nn.silu(cvec) @ w + b
    shift, scale, gate = jnp.split(m, 3, axis=-1)
    return shift[:, None], scale[:, None], gate[:, None]


def modulate(x, mod, g):
    shift, scale, _ = mod
    return rmsnorm(x, g) * (1 + scale) + shift


def grid_positions(T):
    rows = T // GRID_W
    row = jnp.repeat(jnp.arange(rows, dtype=jnp.float32), GRID_W)
    col = jnp.tile(jnp.arange(GRID_W, dtype=jnp.float32), rows)
    return row, col


def _rope_1d(x, pos):
    n = x.shape[-1]
    inv = ROPE_BASE ** (-jnp.arange(0, n, 2, dtype=jnp.float32) / n)
    ang = pos[:, None] * inv[None]
    cos = jnp.cos(ang)[None, :, None, :]
    sin = jnp.sin(ang)[None, :, None, :]
    x1, x2 = x[..., : n // 2], x[..., n // 2:]
    return jnp.concatenate([x1 * cos - x2 * sin, x1 * sin + x2 * cos], axis=-1)


def rope_2d(x):
    T, R = x.shape[1], x.shape[-1]
    row, col = grid_positions(T)
    xf = x.astype(jnp.float32)
    out = jnp.concatenate([_rope_1d(xf[..., : R // 2], row), _rope_1d(xf[..., R // 2:], col)], axis=-1)
    return out.astype(x.dtype)


def dense_attention(q, k, v, scale, sink=None):
    B, T, H, d = q.shape
    KH = k.shape[2]
    G = H // KH
    dv = v.shape[-1]
    nb = T // Q_BLOCK
    qb = q.reshape(B, nb, Q_BLOCK, KH, G, d).transpose(1, 0, 2, 3, 4, 5)

    def block(qi):
        s = jnp.einsum('bqkgd,bskd->bkgqs', qi, k).astype(jnp.float32) * scale
        if sink is not None:
            sk = jnp.broadcast_to(sink.astype(jnp.float32).reshape(1, KH, G, 1, 1), s.shape[:-1] + (1,))
            p = jax.nn.softmax(jnp.concatenate([s, sk], axis=-1), axis=-1)[..., :-1]
        else:
            p = jax.nn.softmax(s, axis=-1)
        return jnp.einsum('bkgqs,bskd->bqkgd', p.astype(v.dtype), v)

    o = lax.map(block, qb)
    return o.transpose(1, 0, 2, 3, 4, 5).reshape(B, T, H, dv)


def banded_attention(q, k, v, ck, cv, sink, scale):
    B, T, H, d = q.shape
    KH = k.shape[2]
    G = H // KH
    W = SWA_WINDOW
    nb = T // W
    P = ck.shape[1]
    qb = q.reshape(B, nb, W, KH, G, d)

    def neighbours(t):
        tp = jnp.pad(t, ((0, 0), (W, W), (0, 0), (0, 0))).reshape(B, nb + 2, W, KH, t.shape[-1])
        return jnp.concatenate([tp[:, :-2], tp[:, 1:-1], tp[:, 2:]], axis=2)

    kw, vw = neighbours(k), neighbours(v)
    s_loc = jnp.einsum('bnqkgd,bnskd->bnkgqs', qb, kw).astype(jnp.float32) * scale
    qpos = jnp.arange(nb)[:, None] * W + jnp.arange(W)[None]
    kpos = (jnp.arange(nb)[:, None] - 1) * W + jnp.arange(3 * W)[None]
    rel = kpos[:, None, :] - qpos[:, :, None]
    valid = (jnp.abs(rel) <= SWA_WINDOW) & (kpos[:, None, :] >= 0) & (kpos[:, None, :] < T)
    s_loc = jnp.where(valid[None, :, None, None], s_loc, NEG_INF)
    s_ctx = jnp.einsum('bnqkgd,bpkd->bnkgqp', qb, ck).astype(jnp.float32) * scale
    sk = jnp.broadcast_to(sink.astype(jnp.float32).reshape(1, 1, KH, G, 1, 1), s_loc.shape[:-1] + (1,))
    p = jax.nn.softmax(jnp.concatenate([s_loc, s_ctx, sk], axis=-1), axis=-1)
    L = 3 * W
    p_loc = p[..., :L].astype(v.dtype)
    p_ctx = p[..., L:L + P].astype(v.dtype)
    o = jnp.einsum('bnkgqs,bnskd->bnqkgd', p_loc, vw) + jnp.einsum('bnkgqp,bpkd->bnqkgd', p_ctx, cv)
    return o.reshape(B, T, H, v.shape[-1])


def multiscale_pool(v, w_pool, s_pool):
    B, T, _ = v.shape
    vf = v.astype(jnp.float32).reshape(B, T, POOL_GROUPS, POOL_GROUP_W)
    S = jnp.concatenate([jnp.zeros((B, 1, POOL_GROUPS, POOL_GROUP_W), jnp.float32), lax.cumsum(vf, axis=1)], axis=1)
    t = jnp.arange(T)
    outs = []
    for g, w in enumerate(POOL_WINDOWS):
        lo = jnp.clip(t - w // 2, 0, T)
        hi = jnp.clip(t + w // 2, 0, T)
        cnt = (hi - lo).astype(jnp.float32)[None, :, None]
        outs.append((S[:, hi, g] - S[:, lo, g]) / cnt)
    pooled = jnp.stack(outs, axis=2) - vf
    y = jnp.einsum('btgc,gce->btge', pooled, w_pool.astype(jnp.float32)).reshape(B, T, POOL_WIDTH)
    return (y * s_pool.astype(jnp.float32)).astype(v.dtype)


def mla_pool_mixer(h, w_in, g_qn, g_kvn, w_uq, w_ukv, w_pool, s_pool, w_out, ctx_ckv=None, ctx_krope=None):
    B, T, _ = h.shape
    i1 = MLA_Q_RANK
    i2 = i1 + MLA_KV_RANK
    i3 = i2 + MLA_ROPE
    i4 = i3 + MLA_WIDTH
    i5 = i4 + POOL_WIDTH
    cq, ckv, krope, gate_a, v_pool, gate_b = jnp.split(h @ w_in, [i1, i2, i3, i4, i5], axis=-1)
    ckv = rmsnorm(ckv, g_kvn)
    q = (rmsnorm(cq, g_qn) @ w_uq).reshape(B, T, MLA_HEADS, MLA_NOPE + MLA_ROPE)
    if ctx_ckv is not None:
        q = jnp.concatenate([q[..., :MLA_NOPE], rope_2d(q[..., MLA_NOPE:])], axis=-1)
        kr_lat = rope_2d(krope[:, :, None, :])
        keys_ckv = jnp.concatenate([ckv, ctx_ckv], axis=1)
        keys_kr = jnp.concatenate([kr_lat, ctx_krope[:, :, None, :]], axis=1)
    else:
        keys_ckv, keys_kr = ckv, krope[:, :, None, :]
    S = keys_ckv.shape[1]
    kv = (keys_ckv @ w_ukv).reshape(B, S, MLA_HEADS, MLA_NOPE + MLA_V)
    k = jnp.concatenate([kv[..., :MLA_NOPE], jnp.broadcast_to(keys_kr, (B, S, MLA_HEADS, MLA_ROPE))], axis=-1)
    v = kv[..., MLA_NOPE:]
    o = dense_attention(q, k, v, MLA_SCALE)
    a = o.reshape(B, T, MLA_WIDTH) * jax.nn.silu(gate_a)
    b = multiscale_pool(v_pool, w_pool, s_pool) * jax.nn.silu(gate_b)
    out = jnp.concatenate([a, b], axis=-1) @ w_out
    return out, ckv, krope


def swa_mixer(h, w_in, sink, w_out, ctx_k=None, ctx_v=None):
    B, T, _ = h.shape
    q, k, v, g = jnp.split(h @ w_in, [SWA_WIDTH, SWA_WIDTH + SWA_KV_W, SWA_WIDTH + 2 * SWA_KV_W], axis=-1)
    q = q.reshape(B, T, SWA_HEADS, SWA_HEAD_DIM)
    k = k.reshape(B, T, SWA_KV_HEADS, SWA_HEAD_DIM)
    v = v.reshape(B, T, SWA_KV_HEADS, SWA_HEAD_DIM)
    if ctx_k is None:
        o = dense_attention(q, k, v, SWA_SCALE, sink)
    else:
        o = banded_attention(rope_2d(q), rope_2d(k), v, ctx_k, ctx_v, sink, SWA_SCALE)
    out = (o.reshape(B, T, SWA_WIDTH) * jax.nn.silu(g)) @ w_out
    return out, k, v


def setup_inputs(seed: int = 0) -> dict:
    key = jax.random.key(seed)
    ks = jax.random.split(key, 23)
    f = jnp.float32
    D = D_MODEL

    def nrm(k, shape, scale=1.0):
        return jax.random.normal(k, shape, f) * scale

    return {
        "x_prompt": nrm(ks[0], (BATCH, SEQ, D)),
        "x_sample": nrm(ks[1], (DEC_BATCH, DEC_SEQ, D)),
        "cache_ckv": nrm(ks[2], (DEC_BATCH, N_EVEN, PAST_LEN, MLA_KV_RANK)),
        "cache_krope": nrm(ks[3], (DEC_BATCH, N_EVEN, PAST_LEN, MLA_ROPE)),
        "cache_k": nrm(ks[4], (DEC_BATCH, N_ODD, PAST_LEN, SWA_KV_HEADS, SWA_HEAD_DIM)),
        "cache_v": nrm(ks[5], (DEC_BATCH, N_ODD, PAST_LEN, SWA_KV_HEADS, SWA_HEAD_DIM)),
        "c": nrm(ks[6], (DEC_BATCH, D)),
        "c_ctx": nrm(ks[7], (D,)),
        "ada_w": nrm(ks[8], (DEPTH, D, 3 * D), D ** -0.5),
        "ada_b": nrm(ks[9], (DEPTH, 3 * D), 0.02),
        "norm_pre": 1.0 + nrm(ks[10], (DEPTH, D), 0.1),
        "norm_post": 1.0 + nrm(ks[11], (DEPTH, D), 0.1),
        "mla_w_in": nrm(ks[12], (N_EVEN, D, A_IN), D ** -0.5),
        "mla_g_qn": 1.0 + nrm(ks[13], (N_EVEN, MLA_Q_RANK), 0.1),
        "mla_g_kvn": 1.0 + nrm(ks[14], (N_EVEN, MLA_KV_RANK), 0.1),
        "mla_w_uq": nrm(ks[15], (N_EVEN, MLA_Q_RANK, MLA_HEADS * (MLA_NOPE + MLA_ROPE)), MLA_Q_RANK ** -0.5),
        "mla_w_ukv": nrm(ks[16], (N_EVEN, MLA_KV_RANK, MLA_HEADS * (MLA_NOPE + MLA_V)), MLA_KV_RANK ** -0.5),
        "pool_w": nrm(ks[17], (N_EVEN, POOL_GROUPS, POOL_GROUP_W, POOL_GROUP_W), POOL_GROUP_W ** -0.5),
        "pool_scale": 1.0 + nrm(ks[18], (N_EVEN, POOL_WIDTH), 0.1),
        "mixa_w_out": nrm(ks[19], (N_EVEN, A_MIX, D), A_MIX ** -0.5),
        "swa_w_in": nrm(ks[20], (N_ODD, D, C_IN), D ** -0.5),
        "swa_sink": nrm(ks[21], (N_ODD, SWA_HEADS)),
        "swa_w_out": nrm(ks[22], (N_ODD, SWA_WIDTH, D), SWA_WIDTH ** -0.5),
    }


def reference(x_prompt, x_sample, cache_ckv, cache_krope, cache_k, cache_v, c, c_ctx,
              ada_w, ada_b, norm_pre, norm_post,
              mla_w_in, mla_g_qn, mla_g_kvn, mla_w_uq, mla_w_ukv, pool_w, pool_scale, mixa_w_out,
              swa_w_in, swa_sink, swa_w_out):
    xp, xs = x_prompt, x_sample
    st_ckv, st_krope, st_k, st_v = [], [], [], []
    for l in range(DEPTH):
        mp = ada_mod(c_ctx[None], ada_w[l], ada_b[l])
        ms = ada_mod(c, ada_w[l], ada_b[l])
        hp = modulate(xp, mp, norm_pre[l])
        hs = modulate(xs, ms, norm_pre[l])
        if l % 2 == 0:
            i = l // 2
            wts = (mla_w_in[i], mla_g_qn[i], mla_g_kvn[i], mla_w_uq[i], mla_w_ukv[i], pool_w[i], pool_scale[i], mixa_w_out[i])
            op, ckv, krope = mla_pool_mixer(hp, *wts)
            os_, _, _ = mla_pool_mixer(hs, *wts, ctx_ckv=cache_ckv[:, i], ctx_krope=cache_krope[:, i])
            st_ckv.append(ckv)
            st_krope.append(krope)
        else:
            i = l // 2
            op, kc, vc = swa_mixer(hp, swa_w_in[i], swa_sink[i], swa_w_out[i])
            os_, _, _ = swa_mixer(hs, swa_w_in[i], swa_sink[i], swa_w_out[i], ctx_k=cache_k[:, i], ctx_v=cache_v[:, i])
            st_k.append(kc)
            st_v.append(vc)
        xp = xp + mp[2] * rmsnorm(op, norm_post[l])
        xs = xs + ms[2] * rmsnorm(os_, norm_post[l])
    state_ckv = jnp.stack(st_ckv, axis=1)
    state_krope = jnp.stack(st_krope, axis=1)
    state_k = jnp.stack(st_k, axis=1)
    state_v = jnp.stack(st_v, axis=1)
    return (xp, xs, state_ckv, state_krope, state_k, state_v)
```

```python
import functools

import numpy as np
import jax
import jax.numpy as jnp
from jax import lax
from jax.experimental import pallas as pl
from jax.experimental.pallas import tpu as pltpu

F32 = jnp.float32
BF16 = jnp.bfloat16

D_MODEL = 1024
BATCH = 32
SEQ = 256
DEPTH = 4
DEC_BATCH = 8
DEC_SEQ = 2048
PAST_LEN = 256
GRID_W = 64
ROPE_BASE = 10000.0
EPS = 1e-6
NEG_INF = -1e30

MLA_HEADS = 8
MLA_NOPE = 64
MLA_ROPE = 32
MLA_V = 64
MLA_Q_RANK = 384
MLA_KV_RANK = 256
MLA_WIDTH = MLA_HEADS * MLA_V
MLA_SCALE = (MLA_NOPE + MLA_ROPE) ** -0.5
POOL_WINDOWS = (2, 4, 8, 16)
POOL_GROUPS = 4
POOL_GROUP_W = D_MODEL // 8
POOL_WIDTH = POOL_GROUPS * POOL_GROUP_W
SWA_HEADS = 16
SWA_KV_HEADS = 4
SWA_HEAD_DIM = 64
SWA_WINDOW = 128
SWA_WIDTH = SWA_HEADS * SWA_HEAD_DIM
SWA_KV_W = SWA_KV_HEADS * SWA_HEAD_DIM
SWA_SCALE = SWA_HEAD_DIM ** -0.5

LANES = 128
HALF = LANES // 2
BF16_SUBLANES = 16
MOD_ROWS = 16
TOK_TILE = 256
POOL_HALO = 64
VMEM_LIMIT = 48 * 1024 * 1024

_CQ = (0, MLA_Q_RANK)
_CKV = (_CQ[1], _CQ[1] + MLA_KV_RANK)
_KR = (_CKV[1], _CKV[1] + LANES)
_GA = (_KR[1], _KR[1] + MLA_WIDTH)
_VP = (_GA[1], _GA[1] + POOL_WIDTH)
_GB = (_VP[1], _VP[1] + POOL_WIDTH)
MLA_IN_COLS = _GB[1]
MLA_HEAD_PAD = LANES
MLA_QK_W = MLA_HEADS * MLA_HEAD_PAD


def _params():
    return pltpu.CompilerParams(dimension_semantics=("arbitrary",), vmem_limit_bytes=VMEM_LIMIT)


def _params2():
    return pltpu.CompilerParams(dimension_semantics=("arbitrary", "arbitrary"), vmem_limit_bytes=VMEM_LIMIT)


def _dot(a, b):
    return jnp.dot(a, b, preferred_element_type=F32)


def _dot_nt(a, b):
    return lax.dot_general(a, b, (((1,), (1,)), ((), ())), preferred_element_type=F32)


def _silu(x):
    return x / (1.0 + jnp.exp(-x))


def _rms(x):
    return x * lax.rsqrt(jnp.mean(x * x, axis=-1, keepdims=True) + EPS)


def _lane_iota(shape):
    return lax.broadcasted_iota(jnp.int32, shape, len(shape) - 1)


def _full(shape):
    nd = len(shape)
    return pl.BlockSpec(shape, lambda *_: (0,) * nd)


def _ada_body(c_ref, w_ref, b_ref, o_ref):
    s = _silu(c_ref[...]).astype(BF16)
    o_ref[0] = _dot(s, w_ref[0].astype(BF16)) + b_ref[0]


def _ada_call(cvec, ada_w, ada_b):
    tn = 768
    return pl.pallas_call(
        _ada_body,
        grid=(DEPTH, 3 * D_MODEL // tn),
        in_specs=[
            pl.BlockSpec((MOD_ROWS, D_MODEL), lambda l, j: (0, 0)),
            pl.BlockSpec((1, D_MODEL, tn), lambda l, j: (l, 0, j)),
            pl.BlockSpec((1, 1, tn), lambda l, j: (l, 0, j)),
        ],
        out_specs=pl.BlockSpec((1, MOD_ROWS, tn), lambda l, j: (l, 0, j)),
        out_shape=jax.ShapeDtypeStruct((DEPTH, MOD_ROWS, 3 * D_MODEL), F32),
        compiler_params=_params2(),
        name="ada_mod",
    )(cvec, ada_w, ada_b.reshape(DEPTH, 1, 3 * D_MODEL))


def _modulated(x_ref, mod_ref, g_ref):
    x = x_ref[...]
    m = mod_ref[0]
    xn = _rms(x) * g_ref[...]
    return (xn * (1.0 + m[1:2, :]) + m[0:1, :]).astype(BF16)


def _rope_block(b, cos, sin, partner):
    first = (_lane_iota(b.shape) & partner) == 0
    sw = jnp.where(first, pltpu.roll(b, LANES - partner, 1), pltpu.roll(b, partner, 1))
    return b * cos + sw * sin


def _mod_row_fn(latent, tile):
    if latent:
        per_seq = DEC_SEQ // tile
        return lambda i: (1 + i // per_seq, 0, 0)
    return lambda i: (0, 0, 0)


def _mla_in_body(*refs, rope, state):
    x_ref, mod_ref, g_ref, win_ref, gq_ref, gkv_ref, wuq_ref, wuk_ref, wuv_ref = refs[:9]
    refs = refs[9:]
    if rope:
        cos_ref, sin_ref = refs[:2]
        refs = refs[2:]
    q_ref, kc_ref, v_ref, ga_ref, vp_ref, gb_ref = refs[:6]
    h = _modulated(x_ref, mod_ref, g_ref)

    def proj(c):
        return _dot(h, win_ref[:, c[0]:c[1]])

    cqn = (_rms(proj(_CQ)) * gq_ref[...]).astype(BF16)
    q = _dot(cqn, wuq_ref[...]) * MLA_SCALE
    ckvn = _rms(proj(_CKV)) * gkv_ref[...]
    ckvb = ckvn.astype(BF16)
    kr = proj(_KR)
    if state:
        refs[6][...] = ckvn
        refs[7][...] = kr
    if rope:
        cos = cos_ref[...]
        sin = sin_ref[...]
        kr = _rope_block(kr, cos, sin, MLA_ROPE // 4)
    kc = _dot(ckvb, wuk_ref[...])
    for hh in range(MLA_HEADS):
        blk = slice(hh * MLA_HEAD_PAD, (hh + 1) * MLA_HEAD_PAD)
        qb = q[:, blk]
        if rope:
            qb = _rope_block(qb, cos, sin, MLA_ROPE // 4)
        q_ref[:, blk] = qb.astype(BF16)
        kc_ref[:, blk] = (kc[:, blk] + kr).astype(BF16)
    v_ref[...] = _dot(ckvb, wuv_ref[...]).astype(BF16)
    ga_ref[...] = _silu(proj(_GA)).astype(BF16)
    vp_ref[...] = proj(_VP).astype(BF16)
    gb_ref[...] = _silu(proj(_GB)).astype(BF16)


def _mla_in_call(x, mod, gpre, w, latent, rope_tabs):
    n = x.shape[0]
    tm = TOK_TILE
    row = lambda i: (i, 0)
    in_specs = [
        pl.BlockSpec((tm, D_MODEL), row),
        pl.BlockSpec((1, 3, D_MODEL), _mod_row_fn(latent, tm)),
        _full((1, D_MODEL)),
        _full((D_MODEL, MLA_IN_COLS)),
        _full((1, MLA_Q_RANK)),
        _full((1, MLA_KV_RANK)),
        _full((MLA_Q_RANK, MLA_QK_W)),
        _full((MLA_KV_RANK, MLA_QK_W)),
        _full((MLA_KV_RANK, MLA_WIDTH)),
    ]
    args = [x, mod, gpre, w["w_in"], w["g_qn"], w["g_kvn"], w["w_uq"], w["w_uk"], w["w_uv"]]
    if latent:
        per_seq = DEC_SEQ // tm
        in_specs += [pl.BlockSpec((tm, LANES), lambda i: (i % per_seq, 0))] * 2
        args += list(rope_tabs)
    widths = [MLA_QK_W, MLA_QK_W, MLA_WIDTH, MLA_WIDTH, POOL_WIDTH, POOL_WIDTH]
    out_shape = [jax.ShapeDtypeStruct((n, c), BF16) for c in widths]
    out_specs = [pl.BlockSpec((tm, c), row) for c in widths]
    if not latent:
        out_shape += [jax.ShapeDtypeStruct((n, MLA_KV_RANK), F32), jax.ShapeDtypeStruct((n, LANES), F32)]
        out_specs += [pl.BlockSpec((tm, MLA_KV_RANK), row), pl.BlockSpec((tm, LANES), row)]
    return pl.pallas_call(
        functools.partial(_mla_in_body, rope=latent, state=not latent),
        grid=(n // tm,),
        in_specs=in_specs,
        out_specs=out_specs,
        out_shape=out_shape,
        compiler_params=_params(),
        name="mla_in_latent" if latent else "mla_in_context",
    )(*args)


def _mla_ctx_body(ckv_ref, kr_ref, wuk_ref, wuv_ref, kc_ref, v_ref):
    ckvb = ckv_ref[...].astype(BF16)
    kr = kr_ref[...]
    kc = _dot(ckvb, wuk_ref[...])
    for hh in range(MLA_HEADS):
        blk = slice(hh * MLA_HEAD_PAD, (hh + 1) * MLA_HEAD_PAD)
        kc_ref[:, blk] = (kc[:, blk] + kr).astype(BF16)
    v_ref[...] = _dot(ckvb, wuv_ref[...]).astype(BF16)


def _mla_ctx_call(ckv, kr_blk, w):
    n = ckv.shape[0]
    tm = PAST_LEN
    row = lambda i: (i, 0)
    return pl.pallas_call(
        _mla_ctx_body,
        grid=(n // tm,),
        in_specs=[
            pl.BlockSpec((tm, MLA_KV_RANK), row),
            pl.BlockSpec((tm, LANES), row),
            _full((MLA_KV_RANK, MLA_QK_W)),
            _full((MLA_KV_RANK, MLA_WIDTH)),
        ],
        out_specs=[pl.BlockSpec((tm, MLA_QK_W), row), pl.BlockSpec((tm, MLA_WIDTH), row)],
        out_shape=[jax.ShapeDtypeStruct((n, MLA_QK_W), BF16), jax.ShapeDtypeStruct((n, MLA_WIDTH), BF16)],
        compiler_params=_params(),
        name="mla_ctx_kv",
    )(ckv, kr_blk, w["w_uk"], w["w_uv"])


def _mla_attn_body(*refs, has_ctx):
    if has_ctx:
        q_ref, kc_ref, v_ref, kcc_ref, vc_ref, ga_ref, o_ref = refs
    else:
        q_ref, kc_ref, v_ref, ga_ref, o_ref = refs
    low = _lane_iota((q_ref.shape[0], LANES)) < HALF
    for j in range(MLA_HEADS // 2):
        pair = slice(j * LANES, (j + 1) * LANES)
        outs = []
        for hh in (2 * j, 2 * j + 1):
            blk = slice(hh * MLA_HEAD_PAD, (hh + 1) * MLA_HEAD_PAD)
            qh = q_ref[:, blk]
            s = _dot_nt(qh, kc_ref[:, blk])
            m = jnp.max(s, axis=-1, keepdims=True)
            if has_ctx:
                sc = _dot_nt(qh, kcc_ref[:, blk])
                m = jnp.maximum(m, jnp.max(sc, axis=-1, keepdims=True))
            p = jnp.exp(s - m)
            l = jnp.sum(p, axis=-1, keepdims=True)
            o = _dot(p.astype(BF16), v_ref[:, pair])
            if has_ctx:
                pc = jnp.exp(sc - m)
                l = l + jnp.sum(pc, axis=-1, keepdims=True)
                o = o + _dot(pc.astype(BF16), vc_ref[:, pair])
            outs.append(o / l)
        o_pair = jnp.where(low, outs[0], outs[1])
        o_ref[:, pair] = (o_pair * ga_ref[:, pair].astype(F32)).astype(BF16)


def _mla_attn_call(q, kc, v, ga, nbatch, seq, ctx=None):
    tq = min(seq, 256)
    nq = seq // tq
    qrow = lambda b, i: (b * nq + i, 0)
    brow = lambda b, i: (b, 0)
    in_specs = [
        pl.BlockSpec((tq, MLA_QK_W), qrow),
        pl.BlockSpec((seq, MLA_QK_W), brow),
        pl.BlockSpec((seq, MLA_WIDTH), brow),
    ]
    args = [q, kc, v]
    if ctx is not None:
        in_specs += [pl.BlockSpec((PAST_LEN, MLA_QK_W), brow), pl.BlockSpec((PAST_LEN, MLA_WIDTH), brow)]
        args += list(ctx)
    in_specs.append(pl.BlockSpec((tq, MLA_WIDTH), qrow))
    args.append(ga)
    return pl.pallas_call(
        functools.partial(_mla_attn_body, has_ctx=ctx is not None),
        grid=(nbatch, nq),
        in_specs=in_specs,
        out_specs=pl.BlockSpec((tq, MLA_WIDTH), qrow),
        out_shape=jax.ShapeDtypeStruct((nbatch * seq, MLA_WIDTH), BF16),
        compiler_params=_params2(),
        name="mla_attn_latent" if ctx is not None else "mla_attn_context",
    )(*args)


def _post_residual(out, x_ref, mod_ref, gpost_ref, o_ref):
    m = mod_ref[0]
    o_ref[...] = x_ref[...] + m[2:3, :] * (_rms(out) * gpost_ref[...])


def _mla_out_body(a_ref, vp_ref, vprev_ref, vnext_ref, gb_ref, pm_ref, wp_ref, sp_ref, wo_ref,
                  x_ref, mod_ref, gpost_ref, o_ref, *, seq):
    tm = a_ref.shape[0]
    per_seq = seq // tm
    it = pl.program_id(0) % per_seq
    cur = vp_ref[...]
    zero = jnp.zeros_like(vprev_ref[...])
    prev = jnp.where(it == 0, zero, vprev_ref[...])
    nxt = jnp.where(it == per_seq - 1, zero, vnext_ref[...])
    slab = jnp.concatenate([prev, cur, nxt], axis=0)
    t = it * tm + lax.broadcasted_iota(jnp.int32, (tm, LANES), 0)
    bs = []
    for g, w in enumerate(POOL_WINDOWS):
        cols = slice(g * POOL_GROUP_W, (g + 1) * POOL_GROUP_W)
        tot = _dot(pm_ref[g], slab[:, cols])
        cnt = jnp.minimum(t + w // 2, seq) - jnp.maximum(t - w // 2, 0)
        pooled = tot / cnt.astype(F32) - cur[:, cols].astype(F32)
        y = _dot(pooled.astype(BF16), wp_ref[g]) * sp_ref[:, cols]
        bs.append((y * gb_ref[:, cols].astype(F32)).astype(BF16))
    b = jnp.concatenate(bs, axis=1)
    out = _dot(a_ref[...], wo_ref[0:MLA_WIDTH, :]) + _dot(b, wo_ref[MLA_WIDTH:, :])
    _post_residual(out, x_ref, mod_ref, gpost_ref, o_ref)


def _pool_masks(tm):
    r = np.arange(tm)[:, None]
    c = np.arange(tm + 2 * POOL_HALO)[None, :]
    rel = c - POOL_HALO - r
    ms = [((rel >= -(w // 2)) & (rel <= w // 2 - 1)) for w in POOL_WINDOWS]
    return jnp.asarray(np.stack(ms).astype(np.float32), dtype=BF16)


def _mla_out_call(a, vp, gb, x, mod, gpost, w, latent):
    n = x.shape[0]
    tm = TOK_TILE
    seq = DEC_SEQ if latent else SEQ
    row = lambda i: (i, 0)
    hb = tm // POOL_HALO
    nhb = n // POOL_HALO
    return pl.pallas_call(
        functools.partial(_mla_out_body, seq=seq),
        grid=(n // tm,),
        in_specs=[
            pl.BlockSpec((tm, MLA_WIDTH), row),
            pl.BlockSpec((tm, POOL_WIDTH), row),
            pl.BlockSpec((POOL_HALO, POOL_WIDTH), lambda i: (jnp.maximum(i * hb - 1, 0), 0)),
            pl.BlockSpec((POOL_HALO, POOL_WIDTH), lambda i: (jnp.minimum((i + 1) * hb, nhb - 1), 0)),
            pl.BlockSpec((tm, POOL_WIDTH), row),
            _full((POOL_GROUPS, tm, tm + 2 * POOL_HALO)),
            _full((POOL_GROUPS, POOL_GROUP_W, POOL_GROUP_W)),
            _full((1, POOL_WIDTH)),
            _full((MLA_WIDTH + POOL_WIDTH, D_MODEL)),
            pl.BlockSpec((tm, D_MODEL), row),
            pl.BlockSpec((1, 3, D_MODEL), _mod_row_fn(latent, tm)),
            _full((1, D_MODEL)),
        ],
        out_specs=pl.BlockSpec((tm, D_MODEL), row),
        out_shape=jax.ShapeDtypeStruct((n, D_MODEL), F32),
        compiler_params=_params(),
        name="mla_out_latent" if latent else "mla_out_context",
    )(a, vp, vp, vp, gb, _pool_masks(tm), w["w_pool"], w["s_pool"], w["w_out"], x, mod, gpost)


def _swa_out_body(a_ref, wo_ref, x_ref, mod_ref, gpost_ref, o_ref):
    _post_residual(_dot(a_ref[...], wo_ref[...]), x_ref, mod_ref, gpost_ref, o_ref)


def _swa_out_call(a, x, mod, gpost, w_out, latent):
    n = x.shape[0]
    tm = TOK_TILE
    row = lambda i: (i, 0)
    return pl.pallas_call(
        _swa_out_body,
        grid=(n // tm,),
        in_specs=[
            pl.BlockSpec((tm, SWA_WIDTH), row),
            _full((SWA_WIDTH, D_MODEL)),
            pl.BlockSpec((tm, D_MODEL), row),
            pl.BlockSpec((1, 3, D_MODEL), _mod_row_fn(latent, tm)),
            _full((1, D_MODEL)),
        ],
        out_specs=pl.BlockSpec((tm, D_MODEL), row),
        out_shape=jax.ShapeDtypeStruct((n, D_MODEL), F32),
        compiler_params=_params(),
        name="swa_out_latent" if latent else "swa_out_context",
    )(a, w_out, x, mod, gpost)


_SQ = (0, SWA_WIDTH)
_SK = (_SQ[1], _SQ[1] + SWA_KV_W)
_SV = (_SK[1], _SK[1] + SWA_KV_W)
_SG = (_SV[1], _SV[1] + SWA_WIDTH)
SWA_IN_COLS = _SG[1]
SWA_KV_DUP_W = 2 * SWA_KV_W


def _dup_heads(x, o_ref):
    for b in range(SWA_KV_W // LANES):
        blk = x[:, b * LANES:(b + 1) * LANES]
        rot = pltpu.roll(blk, HALF, 1)
        low = _lane_iota(blk.shape) < HALF
        o_ref[:, (2 * b) * LANES:(2 * b + 1) * LANES] = jnp.where(low, blk, rot).astype(BF16)
        o_ref[:, (2 * b + 1) * LANES:(2 * b + 2) * LANES] = jnp.where(low, rot, blk).astype(BF16)


def _swa_in_body(*refs, rope, state):
    x_ref, mod_ref, g_ref, win_ref = refs[:4]
    refs = refs[4:]
    if rope:
        cos_ref, sin_ref = refs[:2]
        refs = refs[2:]
    q_ref, kd_ref, vd_ref, gs_ref = refs[:4]
    h = _modulated(x_ref, mod_ref, g_ref)

    def proj(c):
        return _dot(h, win_ref[:, c[0]:c[1]])

    q = proj(_SQ) * SWA_SCALE
    k = proj(_SK)
    v = proj(_SV)
    if state:
        refs[4][...] = k
        refs[5][...] = v
    if rope:
        cos = cos_ref[...]
        sin = sin_ref[...]
        for b in range(SWA_WIDTH // LANES):
            blk = slice(b * LANES, (b + 1) * LANES)
            q_ref[:, blk] = _rope_block(q[:, blk], cos, sin, SWA_HEAD_DIM // 4).astype(BF16)
        k = jnp.concatenate(
            [_rope_block(k[:, b * LANES:(b + 1) * LANES], cos, sin, SWA_HEAD_DIM // 4)
             for b in range(SWA_KV_W // LANES)], axis=1)
    else:
        q_ref[...] = q.astype(BF16)
    _dup_heads(k, kd_ref)
    _dup_heads(v, vd_ref)
    gs_ref[...] = _silu(proj(_SG)).astype(BF16)


def _swa_in_call(x, mod, gpre, w_in, latent, rope_tabs):
    n = x.shape[0]
    tm = TOK_TILE
    row = lambda i: (i, 0)
    in_specs = [
        pl.BlockSpec((tm, D_MODEL), row),
        pl.BlockSpec((1, 3, D_MODEL), _mod_row_fn(latent, tm)),
        _full((1, D_MODEL)),
        _full((D_MODEL, SWA_IN_COLS)),
    ]
    args = [x, mod, gpre, w_in]
    if latent:
        per_seq = DEC_SEQ // tm
        in_specs += [pl.BlockSpec((tm, LANES), lambda i: (i % per_seq, 0))] * 2
        args += list(rope_tabs)
    widths = [SWA_WIDTH, SWA_KV_DUP_W, SWA_KV_DUP_W, SWA_WIDTH]
    out_shape = [jax.ShapeDtypeStruct((n, c), BF16) for c in widths]
    out_specs = [pl.BlockSpec((tm, c), row) for c in widths]
    if not latent:
        out_shape += [jax.ShapeDtypeStruct((n, SWA_KV_W), F32)] * 2
        out_specs += [pl.BlockSpec((tm, SWA_KV_W), row)] * 2
    return pl.pallas_call(
        functools.partial(_swa_in_body, rope=latent, state=not latent),
        grid=(n // tm,),
        in_specs=in_specs,
        out_specs=out_specs,
        out_shape=out_shape,
        compiler_params=_params(),
        name="swa_in_latent" if latent else "swa_in_context",
    )(*args)


def _swa_pair_heads(q_ref, j):
    qp = q_ref[:, j * LANES:(j + 1) * LANES]
    low = _lane_iota(qp.shape) < HALF
    zero = jnp.zeros_like(qp)
    kv = slice((j // 2) * LANES, (j // 2 + 1) * LANES)
    return (jnp.where(low, qp, zero), jnp.where(low, zero, qp)), kv, low


def _swa_softmax_out(scores, values, sink):
    m = sink
    for s in scores:
        m = jnp.maximum(m, jnp.max(s, axis=-1, keepdims=True))
    l = jnp.exp(sink - m)
    o = None
    for s, v in zip(scores, values):
        p = jnp.exp(s - m)
        l = l + jnp.sum(p, axis=-1, keepdims=True)
        pv = _dot(p.astype(BF16), v)
        o = pv if o is None else o + pv
    return o / l


def _swa_dense_body(sink_ref, q_ref, kd_ref, vd_ref, gs_ref, o_ref):
    for j in range(SWA_HEADS // 2):
        qs, kv, low = _swa_pair_heads(q_ref, j)
        outs = []
        for e in range(2):
            s = _dot_nt(qs[e], kd_ref[:, kv])
            outs.append(_swa_softmax_out([s], [vd_ref[:, kv]], sink_ref[2 * j + e]))
        pair = slice(j * LANES, (j + 1) * LANES)
        o_ref[:, pair] = (jnp.where(low, outs[0], outs[1]) * gs_ref[:, pair].astype(F32)).astype(BF16)


def _swa_dense_call(sink, q, kd, vd, gs):
    n = q.shape[0]
    row = lambda b: (b, 0)
    return pl.pallas_call(
        _swa_dense_body,
        grid=(n // SEQ,),
        in_specs=[
            pl.BlockSpec(memory_space=pltpu.SMEM),
            pl.BlockSpec((SEQ, SWA_WIDTH), row),
            pl.BlockSpec((SEQ, SWA_KV_DUP_W), row),
            pl.BlockSpec((SEQ, SWA_KV_DUP_W), row),
            pl.BlockSpec((SEQ, SWA_WIDTH), row),
        ],
        out_specs=pl.BlockSpec((SEQ, SWA_WIDTH), row),
        out_shape=jax.ShapeDtypeStruct((n, SWA_WIDTH), BF16),
        compiler_params=_params(),
        name="swa_attn_context",
    )(sink, q, kd, vd, gs)


def _swa_band_body(sink_ref, q_ref, kp_ref, kc_ref, kn_ref, kx_ref, vp_ref, vc_ref, vn_ref, vx_ref,
                   gs_ref, o_ref):
    w = SWA_WINDOW
    n = pl.program_id(1)
    nb = pl.num_programs(1)
    r = lax.broadcasted_iota(jnp.int32, (w, w), 0)
    c = lax.broadcasted_iota(jnp.int32, (w, w), 1)
    prev_ok = c >= r + jnp.where(n > 0, 0, w)
    next_ok = c <= r - jnp.where(n < nb - 1, 0, w)
    for j in range(SWA_HEADS // 2):
        qs, kv, low = _swa_pair_heads(q_ref, j)
        outs = []
        for e in range(2):
            sp = jnp.where(prev_ok, _dot_nt(qs[e], kp_ref[:, kv]), NEG_INF)
            sc = _dot_nt(qs[e], kc_ref[:, kv])
            sn = jnp.where(next_ok, _dot_nt(qs[e], kn_ref[:, kv]), NEG_INF)
            sx = _dot_nt(qs[e], kx_ref[:, kv])
            outs.append(_swa_softmax_out(
                [sp, sc, sn, sx], [vp_ref[:, kv], vc_ref[:, kv], vn_ref[:, kv], vx_ref[:, kv]],
                sink_ref[2 * j + e]))
        pair = slice(j * LANES, (j + 1) * LANES)
        o_ref[:, pair] = (jnp.where(low, outs[0], outs[1]) * gs_ref[:, pair].astype(F32)).astype(BF16)


def _swa_band_call(sink, q, kd, vd, kx, vx, gs):
    w = SWA_WINDOW
    nb = DEC_SEQ // w
    total = DEC_BATCH * nb
    cur = lambda b, i: (b * nb + i, 0)
    prev = lambda b, i: (jnp.maximum(b * nb + i - 1, 0), 0)
    nxt = lambda b, i: (jnp.minimum(b * nb + i + 1, total - 1), 0)
    ctx = lambda b, i: (b, 0)
    kvs = lambda f: pl.BlockSpec((w, SWA_KV_DUP_W), f)
    cxs = pl.BlockSpec((PAST_LEN, SWA_KV_DUP_W), ctx)
    return pl.pallas_call(
        _swa_band_body,
        grid=(DEC_BATCH, nb),
        in_specs=[
            pl.BlockSpec(memory_space=pltpu.SMEM),
            pl.BlockSpec((w, SWA_WIDTH), cur),
            kvs(prev), kvs(cur), kvs(nxt), cxs,
            kvs(prev), kvs(cur), kvs(nxt), cxs,
            pl.BlockSpec((w, SWA_WIDTH), cur),
        ],
        out_specs=pl.BlockSpec((w, SWA_WIDTH), cur),
        out_shape=jax.ShapeDtypeStruct((DEC_BATCH * DEC_SEQ, SWA_WIDTH), BF16),
        compiler_params=_params2(),
        name="swa_attn_latent",
    )(sink, q, kd, kd, kd, kx, vd, vd, vd, vx, gs)


def _rope_tables(rope_dims, lane_period, lane_off):
    n = rope_dims // 2
    hf = n // 2
    t = jnp.arange(DEC_SEQ)
    pos = jnp.stack([(t // GRID_W).astype(F32), (t % GRID_W).astype(F32)], axis=0)
    inv = ROPE_BASE ** (-jnp.arange(0, n, 2, dtype=F32) / n)
    ang = pos[:, :, None] * inv[None, None, :]
    cos = jnp.cos(ang)
    sin = jnp.sin(ang)
    cos_d = jnp.concatenate([cos, cos], axis=-1)
    sin_d = jnp.concatenate([-sin, sin], axis=-1)
    cos_r = jnp.concatenate([cos_d[0], cos_d[1]], axis=-1)
    sin_r = jnp.concatenate([sin_d[0], sin_d[1]], axis=-1)
    reps = LANES // lane_period
    pad = ((0, 0), (lane_off, lane_period - lane_off - rope_dims))
    cos_p = jnp.tile(jnp.pad(cos_r, pad, constant_values=1.0), (1, reps))
    sin_p = jnp.tile(jnp.pad(sin_r, pad), (1, reps))
    return cos_p, sin_p


def _prep_mla(i, mla_w_in, mla_g_qn, mla_g_kvn, mla_w_uq, mla_w_ukv, pool_w, pool_scale, mixa_w_out):
    w_in = mla_w_in[i]
    i1 = MLA_Q_RANK
    i2 = i1 + MLA_KV_RANK
    i3 = i2 + MLA_ROPE
    kr = jnp.pad(w_in[:, i2:i3], ((0, 0), (MLA_NOPE, LANES - MLA_NOPE - MLA_ROPE)))
    w_in_r = jnp.concatenate([w_in[:, :i2], kr, w_in[:, i3:]], axis=1).astype(BF16)
    dq = MLA_NOPE + MLA_ROPE
    w_uq = jnp.pad(mla_w_uq[i].reshape(MLA_Q_RANK, MLA_HEADS, dq),
                   ((0, 0), (0, 0), (0, MLA_HEAD_PAD - dq))).reshape(MLA_Q_RANK, MLA_QK_W).astype(BF16)
    ukv = mla_w_ukv[i].reshape(MLA_KV_RANK, MLA_HEADS, MLA_NOPE + MLA_V)
    w_uk = jnp.pad(ukv[:, :, :MLA_NOPE], ((0, 0), (0, 0), (0, MLA_HEAD_PAD - MLA_NOPE)))
    w_uk = w_uk.reshape(MLA_KV_RANK, MLA_QK_W).astype(BF16)
    w_uv = ukv[:, :, MLA_NOPE:].reshape(MLA_KV_RANK, MLA_WIDTH).astype(BF16)
    return {
        "w_in": w_in_r, "w_uq": w_uq, "w_uk": w_uk, "w_uv": w_uv,
        "g_qn": mla_g_qn[i].reshape(1, -1), "g_kvn": mla_g_kvn[i].reshape(1, -1),
        "w_pool": pool_w[i].astype(BF16), "s_pool": pool_scale[i].reshape(1, -1),
        "w_out": mixa_w_out[i].astype(BF16),
    }


def kernel(x_prompt, x_sample, cache_ckv, cache_krope, cache_k, cache_v, c, c_ctx, ada_w, ada_b, norm_pre, norm_post, mla_w_in, mla_g_qn, mla_g_kvn, mla_w_uq, mla_w_ukv, pool_w, pool_scale, mixa_w_out, swa_w_in, swa_sink, swa_w_out):
    n_p = BATCH * SEQ
    n_s = DEC_BATCH * DEC_SEQ
    xp = x_prompt.reshape(n_p, D_MODEL)
    xs = x_sample.reshape(n_s, D_MODEL)
    cvec = jnp.concatenate(
        [c_ctx[None], c, jnp.zeros((MOD_ROWS - 1 - DEC_BATCH, D_MODEL), F32)], axis=0)
    mods = _ada_call(cvec, ada_w, ada_b).reshape(DEPTH, MOD_ROWS, 3, D_MODEL)
    mla_tabs = _rope_tables(MLA_ROPE, LANES, MLA_NOPE)
    swa_tabs = _rope_tables(SWA_HEAD_DIM, SWA_HEAD_DIM, 0)
    st_ckv, st_krope, st_k, st_v = [], [], [], []
    for l in range(DEPTH):
        i = l // 2
        mod = mods[l]
        gpre = norm_pre[l].reshape(1, D_MODEL)
        gpost = norm_post[l].reshape(1, D_MODEL)
        if l % 2 == 0:
            w = _prep_mla(i, mla_w_in, mla_g_qn, mla_g_kvn, mla_w_uq, mla_w_ukv, pool_w, pool_scale,
                          mixa_w_out)
            qp, kcp, vp, gap, vpp, gbp, ckv, krb = _mla_in_call(xp, mod, gpre, w, False, None)
            qs, kcs, vs, gas, vps, gbs = _mla_in_call(xs, mod, gpre, w, True, mla_tabs)
            ctx_kr = jnp.pad(cache_krope[:, i].reshape(DEC_BATCH * PAST_LEN, MLA_ROPE),
                             ((0, 0), (MLA_NOPE, LANES - MLA_NOPE - MLA_ROPE)))
            ctx = _mla_ctx_call(cache_ckv[:, i].reshape(DEC_BATCH * PAST_LEN, MLA_KV_RANK), ctx_kr, w)
            ap = _mla_attn_call(qp, kcp, vp, gap, BATCH, SEQ)
            as_ = _mla_attn_call(qs, kcs, vs, gas, DEC_BATCH, DEC_SEQ, ctx)
            xp = _mla_out_call(ap, vpp, gbp, xp, mod, gpost, w, False)
            xs = _mla_out_call(as_, vps, gbs, xs, mod, gpost, w, True)
            st_ckv.append(ckv.reshape(BATCH, SEQ, MLA_KV_RANK))
            st_krope.append(krb[:, MLA_NOPE:MLA_NOPE + MLA_ROPE].reshape(BATCH, SEQ, MLA_ROPE))
        else:
            w_in = swa_w_in[i].astype(BF16)
            w_out = swa_w_out[i].astype(BF16)
            sink = swa_sink[i]
            qp, kdp, vdp, gsp, kst, vst = _swa_in_call(xp, mod, gpre, w_in, False, None)
            qs, kds, vds, gss = _swa_in_call(xs, mod, gpre, w_in, True, swa_tabs)

            def dup(t):
                t = t[:, i].reshape(DEC_BATCH * PAST_LEN, SWA_KV_HEADS, SWA_HEAD_DIM)
                return jnp.concatenate([t, t], axis=-1).reshape(DEC_BATCH * PAST_LEN, SWA_KV_DUP_W).astype(BF16)

            ap = _swa_dense_call(sink, qp, kdp, vdp, gsp)
            as_ = _swa_band_call(sink, qs, kds, vds, dup(cache_k), dup(cache_v), gss)
            xp = _swa_out_call(ap, xp, mod, gpost, w_out, False)
            xs = _swa_out_call(as_, xs, mod, gpost, w_out, True)
            st_k.append(kst.reshape(BATCH, SEQ, SWA_KV_HEADS, SWA_HEAD_DIM))
            st_v.append(vst.reshape(BATCH, SEQ, SWA_KV_HEADS, SWA_HEAD_DIM))
    return (
        xp.reshape(BATCH, SEQ, D_MODEL),
        xs.reshape(DEC_BATCH, DEC_SEQ, D_MODEL),
        jnp.stack(st_ckv, axis=1),
        jnp.stack(st_krope, axis=1),
        jnp.stack(st_k, axis=1),
        jnp.stack(st_v, axis=1),
    )
```

```python
import functools

import numpy as np
import jax
import jax.numpy as jnp
from jax import lax
from jax.experimental import pallas as pl
from jax.experimental.pallas import tpu as pltpu

F32 = jnp.float32
BF16 = jnp.bfloat16

D_MODEL = 1024
BATCH = 32
SEQ = 256
DEPTH = 4
DEC_BATCH = 8
DEC_SEQ = 2048
PAST_LEN = 256
GRID_W = 64
ROPE_BASE = 10000.0
EPS = 1e-6
NEG_INF = -1e30

MLA_HEADS = 8
MLA_NOPE = 64
MLA_ROPE = 32
MLA_V = 64
MLA_Q_RANK = 384
MLA_KV_RANK = 256
MLA_WIDTH = MLA_HEADS * MLA_V
MLA_SCALE = (MLA_NOPE + MLA_ROPE) ** -0.5
POOL_WINDOWS = (2, 4, 8, 16)
POOL_GROUPS = 4
POOL_GROUP_W = D_MODEL // 8
POOL_WIDTH = POOL_GROUPS * POOL_GROUP_W
SWA_HEADS = 16
SWA_KV_HEADS = 4
SWA_HEAD_DIM = 64
SWA_WINDOW = 128
SWA_WIDTH = SWA_HEADS * SWA_HEAD_DIM
SWA_KV_W = SWA_KV_HEADS * SWA_HEAD_DIM
SWA_SCALE = SWA_HEAD_DIM ** -0.5

LANES = 128
HALF = LANES // 2
BF16_SUBLANES = 16
MOD_ROWS = 16
TOK_TILE = 256
POOL_HALO = 64
VMEM_LIMIT = 48 * 1024 * 1024

_CQ = (0, MLA_Q_RANK)
_CKV = (_CQ[1], _CQ[1] + MLA_KV_RANK)
_KR = (_CKV[1], _CKV[1] + LANES)
_GA = (_KR[1], _KR[1] + MLA_WIDTH)
_VP = (_GA[1], _GA[1] + POOL_WIDTH)
_GB = (_VP[1], _VP[1] + POOL_WIDTH)
MLA_IN_COLS = _GB[1]
MLA_HEAD_PAD = LANES
MLA_QK_W = MLA_HEADS * MLA_HEAD_PAD


def _params():
    return pltpu.CompilerParams(dimension_semantics=("arbitrary",), vmem_limit_bytes=VMEM_LIMIT)


def _params2():
    return pltpu.CompilerParams(dimension_semantics=("arbitrary", "arbitrary"), vmem_limit_bytes=VMEM_LIMIT)


def _dot(a, b):
    return jnp.dot(a, b, preferred_element_type=F32)


def _dot_nt(a, b):
    return lax.dot_general(a, b, (((1,), (1,)), ((), ())), preferred_element_type=F32)


def _silu(x):
    return x / (1.0 + jnp.exp(-x))


def _rms(x):
    return x * lax.rsqrt(jnp.mean(x * x, axis=-1, keepdims=True) + EPS)


def _lane_iota(shape):
    return lax.broadcasted_iota(jnp.int32, shape, len(shape) - 1)


def _full(shape):
    nd = len(shape)
    return pl.BlockSpec(shape, lambda *_: (0,) * nd)


def _ada_body(c_ref, w_ref, b_ref, o_ref):
    s = _silu(c_ref[...]).astype(BF16)
    o_ref[0] = _dot(s, w_ref[0].astype(BF16)) + b_ref[0]


def _ada_call(cvec, ada_w, ada_b):
    tn = 768
    return pl.pallas_call(
        _ada_body,
        grid=(DEPTH, 3 * D_MODEL // tn),
        in_specs=[
            pl.BlockSpec((MOD_ROWS, D_MODEL), lambda l, j: (0, 0)),
            pl.BlockSpec((1, D_MODEL, tn), lambda l, j: (l, 0, j)),
            pl.BlockSpec((1, 1, tn), lambda l, j: (l, 0, j)),
        ],
        out_specs=pl.BlockSpec((1, MOD_ROWS, tn), lambda l, j: (l, 0, j)),
        out_shape=jax.ShapeDtypeStruct((DEPTH, MOD_ROWS, 3 * D_MODEL), F32),
        compiler_params=_params2(),
        name="ada_mod",
    )(cvec, ada_w, ada_b.reshape(DEPTH, 1, 3 * D_MODEL))


def _modulated(x_ref, mod_ref, g_ref):
    x = x_ref[...]
    m = mod_ref[0]
    xn = _rms(x) * g_ref[...]
    return (xn * (1.0 + m[1:2, :]) + m[0:1, :]).astype(BF16)


def _rope_block(b, cos, sin, partner):
    first = (_lane_iota(b.shape) & partner) == 0
    sw = jnp.where(first, pltpu.roll(b, LANES - partner, 1), pltpu.roll(b, partner, 1))
    return b * cos + sw * sin


def _mod_row_fn(latent, tile):
    if latent:
        per_seq = DEC_SEQ // tile
        return lambda i: (1 + i // per_seq, 0, 0)
    return lambda i: (0, 0, 0)


def _mla_in_body(*refs, rope, state):
    x_ref, mod_ref, g_ref, win_ref, gq_ref, gkv_ref, wuq_ref, wuk_ref, wuv_ref = refs[:9]
    refs = refs[9:]
    if rope:
        cos_ref, sin_ref = refs[:2]
        refs = refs[2:]
    q_ref, kc_ref, v_ref, ga_ref, vp_ref, gb_ref = refs[:6]
    h = _modulated(x_ref, mod_ref, g_ref)

    def proj(c):
        return _dot(h, win_ref[:, c[0]:c[1]])

    cqn = (_rms(proj(_CQ)) * gq_ref[...]).astype(BF16)
    q = _dot(cqn, wuq_ref[...]) * MLA_SCALE
    ckvn = _rms(proj(_CKV)) * gkv_ref[...]
    ckvb = ckvn.astype(BF16)
    kr = proj(_KR)
    if state:
        refs[6][...] = ckvn
        refs[7][...] = kr
    if rope:
        cos = cos_ref[...]
        sin = sin_ref[...]
        kr = _rope_block(kr, cos, sin, MLA_ROPE // 4)
    kc = _dot(ckvb, wuk_ref[...])
    for hh in range(MLA_HEADS):
        blk = slice(hh * MLA_HEAD_PAD, (hh + 1) * MLA_HEAD_PAD)
        qb = q[:, blk]
        if rope:
            qb = _rope_block(qb, cos, sin, MLA_ROPE // 4)
        q_ref[:, blk] = qb.astype(BF16)
        kc_ref[:, blk] = (kc[:, blk] + kr).astype(BF16)
    v_ref[...] = _dot(ckvb, wuv_ref[...]).astype(BF16)
    ga_ref[...] = _silu(proj(_GA)).astype(BF16)
    vp_ref[...] = proj(_VP).astype(BF16)
    gb_ref[...] = _silu(proj(_GB)).astype(BF16)


def _mla_in_call(x, mod, gpre, w, latent, rope_tabs):
    n = x.shape[0]
    tm = TOK_TILE
    row = lambda i: (i, 0)
    in_specs = [
        pl.BlockSpec((tm, D_MODEL), row),
        pl.BlockSpec((1, 3, D_MODEL), _mod_row_fn(latent, tm)),
        _full((1, D_MODEL)),
        _full((D_MODEL, MLA_IN_COLS)),
        _full((1, MLA_Q_RANK)),
        _full((1, MLA_KV_RANK)),
        _full((MLA_Q_RANK, MLA_QK_W)),
        _full((MLA_KV_RANK, MLA_QK_W)),
        _full((MLA_KV_RANK, MLA_WIDTH)),
    ]
    args = [x, mod, gpre, w["w_in"], w["g_qn"], w["g_kvn"], w["w_uq"], w["w_uk"], w["w_uv"]]
    if latent:
        per_seq = DEC_SEQ // tm
        in_specs += [pl.BlockSpec((tm, LANES), lambda i: (i % per_seq, 0))] * 2
        args += list(rope_tabs)
    widths = [MLA_QK_W, MLA_QK_W, MLA_WIDTH, MLA_WIDTH, POOL_WIDTH, POOL_WIDTH]
    out_shape = [jax.ShapeDtypeStruct((n, c), BF16) for c in widths]
    out_specs = [pl.BlockSpec((tm, c), row) for c in widths]
    if not latent:
        out_shape += [jax.ShapeDtypeStruct((n, MLA_KV_RANK), F32), jax.ShapeDtypeStruct((n, LANES), F32)]
        out_specs += [pl.BlockSpec((tm, MLA_KV_RANK), row), pl.BlockSpec((tm, LANES), row)]
    return pl.pallas_call(
        functools.partial(_mla_in_body, rope=latent, state=not latent),
        grid=(n // tm,),
        in_specs=in_specs,
        out_specs=out_specs,
        out_shape=out_shape,
        compiler_params=_params(),
        name="mla_in_latent" if latent else "mla_in_context",
    )(*args)


def _mla_ctx_body(ckv_ref, kr_ref, wuk_ref, wuv_ref, kc_ref, v_ref):
    ckvb = ckv_ref[...].astype(BF16)
    kr = kr_ref[...]
    kc = _dot(ckvb, wuk_ref[...])
    for hh in range(MLA_HEADS):
        blk = slice(hh * MLA_HEAD_PAD, (hh + 1) * MLA_HEAD_PAD)
        kc_ref[:, blk] = (kc[:, blk] + kr).astype(BF16)
    v_ref[...] = _dot(ckvb, wuv_ref[...]).astype(BF16)


def _mla_ctx_call(ckv, kr_blk, w):
    n = ckv.shape[0]
    tm = PAST_LEN
    row = lambda i: (i, 0)
    return pl.pallas_call(
        _mla_ctx_body,
        grid=(n // tm,),
        in_specs=[
            pl.BlockSpec((tm, MLA_KV_RANK), row),
            pl.BlockSpec((tm, LANES), row),
            _full((MLA_KV_RANK, MLA_QK_W)),
            _full((MLA_KV_RANK, MLA_WIDTH)),
        ],
        out_specs=[pl.BlockSpec((tm, MLA_QK_W), row), pl.BlockSpec((tm, MLA_WIDTH), row)],
        out_shape=[jax.ShapeDtypeStruct((n, MLA_QK_W), BF16), jax.ShapeDtypeStruct((n, MLA_WIDTH), BF16)],
        compiler_params=_params(),
        name="mla_ctx_kv",
    )(ckv, kr_blk, w["w_uk"], w["w_uv"])


def _mla_attn_body(*refs, has_ctx):
    if has_ctx:
        q_ref, kc_ref, v_ref, kcc_ref, vc_ref, ga_ref, o_ref = refs
    else:
        q_ref, kc_ref, v_ref, ga_ref, o_ref = refs
    low = _lane_iota((q_ref.shape[0], LANES)) < HALF
    for j in range(MLA_HEADS // 2):
        pair = slice(j * LANES, (j + 1) * LANES)
        outs = []
        for hh in (2 * j, 2 * j + 1):
            blk = slice(hh * MLA_HEAD_PAD, (hh + 1) * MLA_HEAD_PAD)
            qh = q_ref[:, blk]
            s = _dot_nt(qh, kc_ref[:, blk])
            m = jnp.max(s, axis=-1, keepdims=True)
            if has_ctx:
                sc = _dot_nt(qh, kcc_ref[:, blk])
                m = jnp.maximum(m, jnp.max(sc, axis=-1, keepdims=True))
            p = jnp.exp(s - m)
            l = jnp.sum(p, axis=-1, keepdims=True)
            o = _dot(p.astype(BF16), v_ref[:, pair])
            if has_ctx:
                pc = jnp.exp(sc - m)
                l = l + jnp.sum(pc, axis=-1, keepdims=True)
                o = o + _dot(pc.astype(BF16), vc_ref[:, pair])
            outs.append(o / l)
        o_pair = jnp.where(low, outs[0], outs[1])
        o_ref[:, pair] = (o_pair * ga_ref[:, pair].astype(F32)).astype(BF16)


def _mla_attn_call(q, kc, v, ga, nbatch, seq, ctx=None):
    tq = min(seq, 256)
    nq = seq // tq
    qrow = lambda b, i: (b * nq + i, 0)
    brow = lambda b, i: (b, 0)
    in_specs = [
        pl.BlockSpec((tq, MLA_QK_W), qrow),
        pl.BlockSpec((seq, MLA_QK_W), brow),
        pl.BlockSpec((seq, MLA_WIDTH), brow),
    ]
    args = [q, kc, v]
    if ctx is not None:
        in_specs += [pl.BlockSpec((PAST_LEN, MLA_QK_W), brow), pl.BlockSpec((PAST_LEN, MLA_WIDTH), brow)]
        args += list(ctx)
    in_specs.append(pl.BlockSpec((tq, MLA_WIDTH), qrow))
    args.append(ga)
    return pl.pallas_call(
        functools.partial(_mla_attn_body, has_ctx=ctx is not None),
        grid=(nbatch, nq),
        in_specs=in_specs,
        out_specs=pl.BlockSpec((tq, MLA_WIDTH), qrow),
        out_shape=jax.ShapeDtypeStruct((nbatch * seq, MLA_WIDTH), BF16),
        compiler_params=_params2(),
        name="mla_attn_latent" if ctx is not None else "mla_attn_context",
    )(*args)


def _post_residual(out, x_ref, mod_ref, gpost_ref, o_ref):
    m = mod_ref[0]
    o_ref[...] = x_ref[...] + m[2:3, :] * (_rms(out) * gpost_ref[...])


def _mla_out_body(a_ref, vp_ref, vprev_ref, vnext_ref, gb_ref, pm_ref, wp_ref, sp_ref, wo_ref,
                  x_ref, mod_ref, gpost_ref, o_ref, *, seq):
    tm = a_ref.shape[0]
    per_seq = seq // tm
    it = pl.program_id(0) % per_seq
    cur = vp_ref[...]
    zero = jnp.zeros_like(vprev_ref[...])
    prev = jnp.where(it == 0, zero, vprev_ref[...])
    nxt = jnp.where(it == per_seq - 1, zero, vnext_ref[...])
    slab = jnp.concatenate([prev, cur, nxt], axis=0)
    t = it * tm + lax.broadcasted_iota(jnp.int32, (tm, LANES), 0)
    bs = []
    for g, w in enumerate(POOL_WINDOWS):
        cols = slice(g * POOL_GROUP_W, (g + 1) * POOL_GROUP_W)
        tot = _dot(pm_ref[g], slab[:, cols])
        cnt = jnp.minimum(t + w // 2, seq) - jnp.maximum(t - w // 2, 0)
        pooled = tot / cnt.astype(F32) - cur[:, cols].astype(F32)
        y = _dot(pooled.astype(BF16), wp_ref[g]) * sp_ref[:, cols]
        bs.append((y * gb_ref[:, cols].astype(F32)).astype(BF16))
    b = jnp.concatenate(bs, axis=1)
    out = _dot(a_ref[...], wo_ref[0:MLA_WIDTH, :]) + _dot(b, wo_ref[MLA_WIDTH:, :])
    _post_residual(out, x_ref, mod_ref, gpost_ref, o_ref)


def _pool_masks(tm):
    r = np.arange(tm)[:, None]
    c = np.arange(tm + 2 * POOL_HALO)[None, :]
    rel = c - POOL_HALO - r
    ms = [((rel >= -(w // 2)) & (rel <= w // 2 - 1)) for w in POOL_WINDOWS]
    return jnp.asarray(np.stack(ms).astype(np.float32), dtype=BF16)


def _mla_out_call(a, vp, gb, x, mod, gpost, w, latent):
    n = x.shape[0]
    tm = TOK_TILE
    seq = DEC_SEQ if latent else SEQ
    row = lambda i: (i, 0)
    hb = tm // POOL_HALO
    nhb = n // POOL_HALO
    return pl.pallas_call(
        functools.partial(_mla_out_body, seq=seq),
        grid=(n // tm,),
        in_specs=[
            pl.BlockSpec((tm, MLA_WIDTH), row),
            pl.BlockSpec((tm, POOL_WIDTH), row),
            pl.BlockSpec((POOL_HALO, POOL_WIDTH), lambda i: (jnp.maximum(i * hb - 1, 0), 0)),
            pl.BlockSpec((POOL_HALO, POOL_WIDTH), lambda i: (jnp.minimum((i + 1) * hb, nhb - 1), 0)),
            pl.BlockSpec((tm, POOL_WIDTH), row),
            _full((POOL_GROUPS, tm, tm + 2 * POOL_HALO)),
            _full((POOL_GROUPS, POOL_GROUP_W, POOL_GROUP_W)),
            _full((1, POOL_WIDTH)),
            _full((MLA_WIDTH + POOL_WIDTH, D_MODEL)),
            pl.BlockSpec((tm, D_MODEL), row),
            pl.BlockSpec((1, 3, D_MODEL), _mod_row_fn(latent, tm)),
            _full((1, D_MODEL)),
        ],
        out_specs=pl.BlockSpec((tm, D_MODEL), row),
        out_shape=jax.ShapeDtypeStruct((n, D_MODEL), F32),
        compiler_params=_params(),
        name="mla_out_latent" if latent else "mla_out_context",
    )(a, vp, vp, vp, gb, _pool_masks(tm), w["w_pool"], w["s_pool"], w["w_out"], x, mod, gpost)


def _swa_out_body(a_ref, wo_ref, x_ref, mod_ref, gpost_ref, o_ref):
    _post_residual(_dot(a_ref[...], wo_ref[...]), x_ref, mod_ref, gpost_ref, o_ref)


def _swa_out_call(a, x, mod, gpost, w_out, latent):
    n = x.shape[0]
    tm = TOK_TILE
    row = lambda i: (i, 0)
    return pl.pallas_call(
        _swa_out_body,
        grid=(n // tm,),
        in_specs=[
            pl.BlockSpec((tm, SWA_WIDTH), row),
            _full((SWA_WIDTH, D_MODEL)),
            pl.BlockSpec((tm, D_MODEL), row),
            pl.BlockSpec((1, 3, D_MODEL), _mod_row_fn(latent, tm)),
            _full((1, D_MODEL)),
        ],
        out_specs=pl.BlockSpec((tm, D_MODEL), row),
        out_shape=jax.ShapeDtypeStruct((n, D_MODEL), F32),
        compiler_params=_params(),
        name="swa_out_latent" if latent else "swa_out_context",
    )(a, w_out, x, mod, gpost)


_SQ = (0, SWA_WIDTH)
_SK = (_SQ[1], _SQ[1] + SWA_KV_W)
_SV = (_SK[1], _SK[1] + SWA_KV_W)
_SG = (_SV[1], _SV[1] + SWA_WIDTH)
SWA_IN_COLS = _SG[1]
SWA_KV_DUP_W = 2 * SWA_KV_W


def _dup_heads(x, o_ref):
    for b in range(SWA_KV_W // LANES):
        blk = x[:, b * LANES:(b + 1) * LANES]
        rot = pltpu.roll(blk, HALF, 1)
        low = _lane_iota(blk.shape) < HALF
        o_ref[:, (2 * b) * LANES:(2 * b + 1) * LANES] = jnp.where(low, blk, rot).astype(BF16)
        o_ref[:, (2 * b + 1) * LANES:(2 * b + 2) * LANES] = jnp.where(low, rot, blk).astype(BF16)


def _swa_in_body(*refs, rope, state):
    x_ref, mod_ref, g_ref, win_ref = refs[:4]
    refs = refs[4:]
    if rope:
        cos_ref, sin_ref = refs[:2]
        refs = refs[2:]
    q_ref, kd_ref, vd_ref, gs_ref = refs[:4]
    h = _modulated(x_ref, mod_ref, g_ref)

    def proj(c):
        return _dot(h, win_ref[:, c[0]:c[1]])

    q = proj(_SQ) * SWA_SCALE
    k = proj(_SK)
    v = proj(_SV)
    if state:
        refs[4][...] = k
        refs[5][...] = v
    if rope:
        cos = cos_ref[...]
        sin = sin_ref[...]
        for b in range(SWA_WIDTH // LANES):
            blk = slice(b * LANES, (b + 1) * LANES)
            q_ref[:, blk] = _rope_block(q[:, blk], cos, sin, SWA_HEAD_DIM // 4).astype(BF16)
        k = jnp.concatenate(
            [_rope_block(k[:, b * LANES:(b + 1) * LANES], cos, sin, SWA_HEAD_DIM // 4)
             for b in range(SWA_KV_W // LANES)], axis=1)
    else:
        q_ref[...] = q.astype(BF16)
    _dup_heads(k, kd_ref)
    _dup_heads(v, vd_ref)
    gs_ref[...] = _silu(proj(_SG)).astype(BF16)


def _swa_in_call(x, mod, gpre, w_in, latent, rope_tabs):
    n = x.shape[0]
    tm = TOK_TILE
    row = lambda i: (i, 0)
    in_specs = [
        pl.BlockSpec((tm, D_MODEL), row),
        pl.BlockSpec((1, 3, D_MODEL), _mod_row_fn(latent, tm)),
        _full((1, D_MODEL)),
        _full((D_MODEL, SWA_IN_COLS)),
    ]
    args = [x, mod, gpre, w_in]
    if latent:
        per_seq = DEC_SEQ // tm
        in_specs += [pl.BlockSpec((tm, LANES), lambda i: (i % per_seq, 0))] * 2
        args += list(rope_tabs)
    widths = [SWA_WIDTH, SWA_KV_DUP_W, SWA_KV_DUP_W, SWA_WIDTH]
    out_shape = [jax.ShapeDtypeStruct((n, c), BF16) for c in widths]
    out_specs = [pl.BlockSpec((tm, c), row) for c in widths]
    if not latent:
        out_shape += [jax.ShapeDtypeStruct((n, SWA_KV_W), F32)] * 2
        out_specs += [pl.BlockSpec((tm, SWA_KV_W), row)] * 2
    return pl.pallas_call(
        functools.partial(_swa_in_body, rope=latent, state=not latent),
        grid=(n // tm,),
        in_specs=in_specs,
        out_specs=out_specs,
        out_shape=out_shape,
        compiler_params=_params(),
        name="swa_in_latent" if latent else "swa_in_context",
    )(*args)


SWA_GROUP = SWA_HEADS // SWA_KV_HEADS


def _swa_group_attend(qa, qb, parts, sinks):
    rows = qa.shape[0]
    low = _lane_iota(qa.shape) < HALF
    zero = jnp.zeros_like(qa)
    qst = jnp.concatenate(
        [jnp.where(low, qa, zero), jnp.where(low, zero, qa), jnp.where(low, qb, zero), jnp.where(low, zero, qb)],
        axis=0)
    sink_t = jnp.concatenate([jnp.full((rows, LANES), s, F32) for s in sinks], axis=0)
    scores = []
    t = sink_t
    for k, _, bias in parts:
        s = _dot_nt(qst, k)
        if bias is not None:
            s = s + bias
        scores.append(s)
        for cb in range(s.shape[1] // LANES):
            t = jnp.maximum(t, s[:, cb * LANES:(cb + 1) * LANES])
    m = jnp.max(t, axis=-1, keepdims=True)
    acc = None
    for s, (_, v, _) in zip(scores, parts):
        p = jnp.exp(s - m).astype(BF16)
        pv = _dot(p, jnp.concatenate([v, jnp.ones_like(v)], axis=1))
        acc = pv if acc is None else acc + pv
    o = acc[:, :LANES] / (acc[:, LANES:] + jnp.exp(sink_t - m))
    return (jnp.where(low, o[0:rows], o[rows:2 * rows]),
            jnp.where(low, o[2 * rows:3 * rows], o[3 * rows:4 * rows]))


def _swa_dense_body(sink_ref, q_ref, kd_ref, vd_ref, gs_ref, o_ref):
    for kh in range(SWA_KV_HEADS):
        kv = slice(kh * LANES, (kh + 1) * LANES)
        ta = slice(2 * kh * LANES, (2 * kh + 1) * LANES)
        tb = slice((2 * kh + 1) * LANES, (2 * kh + 2) * LANES)
        sinks = [sink_ref[SWA_GROUP * kh + e] for e in range(SWA_GROUP)]
        oa, ob = _swa_group_attend(q_ref[:, ta], q_ref[:, tb], [(kd_ref[:, kv], vd_ref[:, kv], None)], sinks)
        o_ref[:, ta] = (oa * gs_ref[:, ta].astype(F32)).astype(BF16)
        o_ref[:, tb] = (ob * gs_ref[:, tb].astype(F32)).astype(BF16)


def _swa_dense_call(sink, q, kd, vd, gs):
    n = q.shape[0]
    row = lambda b: (b, 0)
    return pl.pallas_call(
        _swa_dense_body,
        grid=(n // SEQ,),
        in_specs=[
            pl.BlockSpec(memory_space=pltpu.SMEM),
            pl.BlockSpec((SEQ, SWA_WIDTH), row),
            pl.BlockSpec((SEQ, SWA_KV_DUP_W), row),
            pl.BlockSpec((SEQ, SWA_KV_DUP_W), row),
            pl.BlockSpec((SEQ, SWA_WIDTH), row),
        ],
        out_specs=pl.BlockSpec((SEQ, SWA_WIDTH), row),
        out_shape=jax.ShapeDtypeStruct((n, SWA_WIDTH), BF16),
        compiler_params=_params(),
        name="swa_attn_context",
    )(sink, q, kd, vd, gs)


SWA_Q_TILE = 512
SWA_SPAN = 3 * SWA_WINDOW


def _swa_band_body(sink_ref, q_ref, kd_ref, vd_ref, kx_ref, vx_ref, gs_ref, o_ref):
    w = SWA_WINDOW
    nsub = q_ref.shape[0] // w
    step = pl.program_id(1)
    r = lax.broadcasted_iota(jnp.int32, (w, SWA_SPAN), 0)
    c = lax.broadcasted_iota(jnp.int32, (w, SWA_SPAN), 1)

    def block(sb, carry):
        qbase = (step * nsub + sb) * w
        start = pl.multiple_of(jnp.clip(qbase - w, 0, DEC_SEQ - SWA_SPAN), w)
        rel = c - r + (start - qbase)
        bias = jnp.where(jnp.abs(rel) <= w, 0.0, NEG_INF).astype(F32)
        bias = jnp.concatenate([bias] * SWA_GROUP, axis=0)
        rows = pl.ds(pl.multiple_of(sb * w, w), w)
        span = pl.ds(start, SWA_SPAN)
        for kh in range(SWA_KV_HEADS):
            kv = slice(kh * LANES, (kh + 1) * LANES)
            ta = slice(2 * kh * LANES, (2 * kh + 1) * LANES)
            tb = slice((2 * kh + 1) * LANES, (2 * kh + 2) * LANES)
            sinks = [sink_ref[SWA_GROUP * kh + e] for e in range(SWA_GROUP)]
            parts = [(kd_ref[span, kv], vd_ref[span, kv], bias), (kx_ref[:, kv], vx_ref[:, kv], None)]
            oa, ob = _swa_group_attend(q_ref[rows, ta], q_ref[rows, tb], parts, sinks)
            o_ref[rows, ta] = (oa * gs_ref[rows, ta].astype(F32)).astype(BF16)
            o_ref[rows, tb] = (ob * gs_ref[rows, tb].astype(F32)).astype(BF16)
        return carry

    lax.fori_loop(0, nsub, block, 0)


def _swa_band_call(sink, q, kd, vd, kx, vx, gs):
    tq = SWA_Q_TILE
    nq = DEC_SEQ // tq
    qrow = lambda b, i: (b * nq + i, 0)
    brow = lambda b, i: (b, 0)
    return pl.pallas_call(
        _swa_band_body,
        grid=(DEC_BATCH, nq),
        in_specs=[
            pl.BlockSpec(memory_space=pltpu.SMEM),
            pl.BlockSpec((tq, SWA_WIDTH), qrow),
            pl.BlockSpec((DEC_SEQ, SWA_KV_DUP_W), brow),
            pl.BlockSpec((DEC_SEQ, SWA_KV_DUP_W), brow),
            pl.BlockSpec((PAST_LEN, SWA_KV_DUP_W), brow),
            pl.BlockSpec((PAST_LEN, SWA_KV_DUP_W), brow),
            pl.BlockSpec((tq, SWA_WIDTH), qrow),
        ],
        out_specs=pl.BlockSpec((tq, SWA_WIDTH), qrow),
        out_shape=jax.ShapeDtypeStruct((DEC_BATCH * DEC_SEQ, SWA_WIDTH), BF16),
        compiler_params=_params2(),
        name="swa_attn_latent",
    )(sink, q, kd, vd, kx, vx, gs)


def _rope_tables(rope_dims, lane_period, lane_off):
    n = rope_dims // 2
    hf = n // 2
    t = jnp.arange(DEC_SEQ)
    pos = jnp.stack([(t // GRID_W).astype(F32), (t % GRID_W).astype(F32)], axis=0)
    inv = ROPE_BASE ** (-jnp.arange(0, n, 2, dtype=F32) / n)
    ang = pos[:, :, None] * inv[None, None, :]
    cos = jnp.cos(ang)
    sin = jnp.sin(ang)
    cos_d = jnp.concatenate([cos, cos], axis=-1)
    sin_d = jnp.concatenate([-sin, sin], axis=-1)
    cos_r = jnp.concatenate([cos_d[0], cos_d[1]], axis=-1)
    sin_r = jnp.concatenate([sin_d[0], sin_d[1]], axis=-1)
    reps = LANES // lane_period
    pad = ((0, 0), (lane_off, lane_period - lane_off - rope_dims))
    cos_p = jnp.tile(jnp.pad(cos_r, pad, constant_values=1.0), (1, reps))
    sin_p = jnp.tile(jnp.pad(sin_r, pad), (1, reps))
    return cos_p, sin_p


def _prep_mla(i, mla_w_in, mla_g_qn, mla_g_kvn, mla_w_uq, mla_w_ukv, pool_w, pool_scale, mixa_w_out):
    w_in = mla_w_in[i]
    i1 = MLA_Q_RANK
    i2 = i1 + MLA_KV_RANK
    i3 = i2 + MLA_ROPE
    kr = jnp.pad(w_in[:, i2:i3], ((0, 0), (MLA_NOPE, LANES - MLA_NOPE - MLA_ROPE)))
    w_in_r = jnp.concatenate([w_in[:, :i2], kr, w_in[:, i3:]], axis=1).astype(BF16)
    dq = MLA_NOPE + MLA_ROPE
    w_uq = jnp.pad(mla_w_uq[i].reshape(MLA_Q_RANK, MLA_HEADS, dq),
                   ((0, 0), (0, 0), (0, MLA_HEAD_PAD - dq))).reshape(MLA_Q_RANK, MLA_QK_W).astype(BF16)
    ukv = mla_w_ukv[i].reshape(MLA_KV_RANK, MLA_HEADS, MLA_NOPE + MLA_V)
    w_uk = jnp.pad(ukv[:, :, :MLA_NOPE], ((0, 0), (0, 0), (0, MLA_HEAD_PAD - MLA_NOPE)))
    w_uk = w_uk.reshape(MLA_KV_RANK, MLA_QK_W).astype(BF16)
    w_uv = ukv[:, :, MLA_NOPE:].reshape(MLA_KV_RANK, MLA_WIDTH).astype(BF16)
    return {
        "w_in": w_in_r, "w_uq": w_uq, "w_uk": w_uk, "w_uv": w_uv,
        "g_qn": mla_g_qn[i].reshape(1, -1), "g_kvn": mla_g_kvn[i].reshape(1, -1),
        "w_pool": pool_w[i].astype(BF16), "s_pool": pool_scale[i].reshape(1, -1),
        "w_out": mixa_w_out[i].astype(BF16),
    }


def kernel(x_prompt, x_sample, cache_ckv, cache_krope, cache_k, cache_v, c, c_ctx, ada_w, ada_b, norm_pre, norm_post, mla_w_in, mla_g_qn, mla_g_kvn, mla_w_uq, mla_w_ukv, pool_w, pool_scale, mixa_w_out, swa_w_in, swa_sink, swa_w_out):
    n_p = BATCH * SEQ
    n_s = DEC_BATCH * DEC_SEQ
    xp = x_prompt.reshape(n_p, D_MODEL)
    xs = x_sample.reshape(n_s, D_MODEL)
    cvec = jnp.concatenate(
        [c_ctx[None], c, jnp.zeros((MOD_ROWS - 1 - DEC_BATCH, D_MODEL), F32)], axis=0)
    mods = _ada_call(cvec, ada_w, ada_b).reshape(DEPTH, MOD_ROWS, 3, D_MODEL)
    mla_tabs = _rope_tables(MLA_ROPE, LANES, MLA_NOPE)
    swa_tabs = _rope_tables(SWA_HEAD_DIM, SWA_HEAD_DIM, 0)
    st_ckv, st_krope, st_k, st_v = [], [], [], []
    for l in range(DEPTH):
        i = l // 2
        mod = mods[l]
        gpre = norm_pre[l].reshape(1, D_MODEL)
        gpost = norm_post[l].reshape(1, D_MODEL)
        if l % 2 == 0:
            w = _prep_mla(i, mla_w_in, mla_g_qn, mla_g_kvn, mla_w_uq, mla_w_ukv, pool_w, pool_scale,
                          mixa_w_out)
            qp, kcp, vp, gap, vpp, gbp, ckv, krb = _mla_in_call(xp, mod, gpre, w, False, None)
            qs, kcs, vs, gas, vps, gbs = _mla_in_call(xs, mod, gpre, w, True, mla_tabs)
            ctx_kr = jnp.pad(cache_krope[:, i].reshape(DEC_BATCH * PAST_LEN, MLA_ROPE),
                             ((0, 0), (MLA_NOPE, LANES - MLA_NOPE - MLA_ROPE)))
            ctx = _mla_ctx_call(cache_ckv[:, i].reshape(DEC_BATCH * PAST_LEN, MLA_KV_RANK), ctx_kr, w)
            ap = _mla_attn_call(qp, kcp, vp, gap, BATCH, SEQ)
            as_ = _mla_attn_call(qs, kcs, vs, gas, DEC_BATCH, DEC_SEQ, ctx)
            xp = _mla_out_call(ap, vpp, gbp, xp, mod, gpost, w, False)
            xs = _mla_out_call(as_, vps, gbs, xs, mod, gpost, w, True)
            st_ckv.append(ckv.reshape(BATCH, SEQ, MLA_KV_RANK))
            st_krope.append(krb[:, MLA_NOPE:MLA_NOPE + MLA_ROPE].reshape(BATCH, SEQ, MLA_ROPE))
        else:
            w_in = swa_w_in[i].astype(BF16)
            w_out = swa_w_out[i].astype(BF16)
            sink = swa_sink[i]
            qp, kdp, vdp, gsp, kst, vst = _swa_in_call(xp, mod, gpre, w_in, False, None)
            qs, kds, vds, gss = _swa_in_call(xs, mod, gpre, w_in, True, swa_tabs)

            def dup(t):
                t = t[:, i].reshape(DEC_BATCH * PAST_LEN, SWA_KV_HEADS, SWA_HEAD_DIM)
                return jnp.concatenate([t, t], axis=-1).reshape(DEC_BATCH * PAST_LEN, SWA_KV_DUP_W).astype(BF16)

            ap = _swa_dense_call(sink, qp, kdp, vdp, gsp)
            as_ = _swa_band_call(sink, qs, kds, vds, dup(cache_k), dup(cache_v), gss)
            xp = _swa_out_call(ap, xp, mod, gpost, w_out, False)
            xs = _swa_out_call(as_, xs, mod, gpost, w_out, True)
            st_k.append(kst.reshape(BATCH, SEQ, SWA_KV_HEADS, SWA_HEAD_DIM))
            st_v.append(vst.reshape(BATCH, SEQ, SWA_KV_HEADS, SWA_HEAD_DIM))
    return (
        xp.reshape(BATCH, SEQ, D_MODEL),
        xs.reshape(DEC_BATCH, DEC_SEQ, D_MODEL),
        jnp.stack(st_ckv, axis=1),
        jnp.stack(st_krope, axis=1),
        jnp.stack(st_k, axis=1),
        jnp.stack(st_v, axis=1),
    )
```

```python
import functools

import numpy as np
import jax
import jax.numpy as jnp
from jax import lax
from jax.experimental import pallas as pl
from jax.experimental.pallas import tpu as pltpu

F32 = jnp.float32
BF16 = jnp.bfloat16

D_MODEL = 1024
BATCH = 32
SEQ = 256
DEPTH = 4
DEC_BATCH = 8
DEC_SEQ = 2048
PAST_LEN = 256
GRID_W = 64
ROPE_BASE = 10000.0
EPS = 1e-6
NEG_INF = -1e30

MLA_HEADS = 8
MLA_NOPE = 64
MLA_ROPE = 32
MLA_V = 64
MLA_Q_RANK = 384
MLA_KV_RANK = 256
MLA_WIDTH = MLA_HEADS * MLA_V
MLA_SCALE = (MLA_NOPE + MLA_ROPE) ** -0.5
POOL_WINDOWS = (2, 4, 8, 16)
POOL_GROUPS = 4
POOL_GROUP_W = D_MODEL // 8
POOL_WIDTH = POOL_GROUPS * POOL_GROUP_W
SWA_HEADS = 16
SWA_KV_HEADS = 4
SWA_HEAD_DIM = 64
SWA_WINDOW = 128
SWA_WIDTH = SWA_HEADS * SWA_HEAD_DIM
SWA_KV_W = SWA_KV_HEADS * SWA_HEAD_DIM
SWA_SCALE = SWA_HEAD_DIM ** -0.5

LANES = 128
HALF = LANES // 2
BF16_SUBLANES = 16
MOD_ROWS = 16
TOK_TILE = 512
LOG2E = 1.4426950408889634
POOL_BLOCK = 128
POOL_HALO = 64
VMEM_LIMIT = 48 * 1024 * 1024

_CQ = (0, MLA_Q_RANK)
_CKV = (_CQ[1], _CQ[1] + MLA_KV_RANK)
_KR = (_CKV[1], _CKV[1] + LANES)
_GA = (_KR[1], _KR[1] + MLA_WIDTH)
_VP = (_GA[1], _GA[1] + POOL_WIDTH)
_GB = (_VP[1], _VP[1] + POOL_WIDTH)
MLA_IN_COLS = _GB[1]
MLA_HEAD_PAD = LANES
MLA_QK_W = MLA_HEADS * MLA_HEAD_PAD
MLA_Q_TILE = 512


def _params():
    return pltpu.CompilerParams(dimension_semantics=("arbitrary",), vmem_limit_bytes=VMEM_LIMIT)


def _params2():
    return pltpu.CompilerParams(dimension_semantics=("arbitrary", "arbitrary"), vmem_limit_bytes=VMEM_LIMIT)


def _dot(a, b):
    return jnp.dot(a, b, preferred_element_type=F32)


def _dot_nt(a, b):
    return lax.dot_general(a, b, (((1,), (1,)), ((), ())), preferred_element_type=F32)


def _silu(x):
    return x / (1.0 + jnp.exp(-x))


def _rms(x):
    return x * lax.rsqrt(jnp.mean(x * x, axis=-1, keepdims=True) + EPS)


def _lane_iota(shape):
    return lax.broadcasted_iota(jnp.int32, shape, len(shape) - 1)


def _full(shape):
    nd = len(shape)
    return pl.BlockSpec(shape, lambda *_: (0,) * nd)


def _ada_body(c_ref, w_ref, b_ref, o_ref):
    s = _silu(c_ref[...]).astype(BF16)
    o_ref[0] = _dot(s, w_ref[0].astype(BF16)) + b_ref[0]


def _ada_call(cvec, ada_w, ada_b):
    tn = 768
    return pl.pallas_call(
        _ada_body,
        grid=(DEPTH, 3 * D_MODEL // tn),
        in_specs=[
            pl.BlockSpec((MOD_ROWS, D_MODEL), lambda l, j: (0, 0)),
            pl.BlockSpec((1, D_MODEL, tn), lambda l, j: (l, 0, j)),
            pl.BlockSpec((1, 1, tn), lambda l, j: (l, 0, j)),
        ],
        out_specs=pl.BlockSpec((1, MOD_ROWS, tn), lambda l, j: (l, 0, j)),
        out_shape=jax.ShapeDtypeStruct((DEPTH, MOD_ROWS, 3 * D_MODEL), F32),
        compiler_params=_params2(),
        name="ada_mod",
    )(cvec, ada_w, ada_b.reshape(DEPTH, 1, 3 * D_MODEL))


def _modulated(x_ref, mod_ref, g_ref):
    x = x_ref[...]
    m = mod_ref[0]
    xn = _rms(x) * g_ref[...]
    return (xn * (1.0 + m[1:2, :]) + m[0:1, :]).astype(BF16)


def _rope_block(b, cos, sin, partner):
    first = (_lane_iota(b.shape) & partner) == 0
    sw = jnp.where(first, pltpu.roll(b, LANES - partner, 1), pltpu.roll(b, partner, 1))
    return b * cos + sw * sin


def _mod_row_fn(latent, tile):
    if latent:
        per_seq = DEC_SEQ // tile
        return lambda i: (1 + i // per_seq, 0, 0)
    return lambda i: (0, 0, 0)


def _mla_in_body(*refs, rope, state):
    x_ref, mod_ref, g_ref, win_ref, gq_ref, gkv_ref, wuq_ref, wuk_ref, wuv_ref = refs[:9]
    refs = refs[9:]
    if rope:
        cos_ref, sin_ref = refs[:2]
        refs = refs[2:]
    q_ref, kc_ref, v_ref, ga_ref, vp_ref, gb_ref = refs[:6]
    h = _modulated(x_ref, mod_ref, g_ref)

    def proj(c):
        return _dot(h, win_ref[:, c[0]:c[1]])

    cqn = (_rms(proj(_CQ)) * gq_ref[...]).astype(BF16)
    q = _dot(cqn, wuq_ref[...]) * (MLA_SCALE * LOG2E)
    ckvn = _rms(proj(_CKV)) * gkv_ref[...]
    ckvb = ckvn.astype(BF16)
    kr = proj(_KR)
    if state:
        refs[6][...] = ckvn
        refs[7][...] = kr
    if rope:
        cos = cos_ref[...]
        sin = sin_ref[...]
        kr = _rope_block(kr, cos, sin, MLA_ROPE // 4)
    kc = _dot(ckvb, wuk_ref[...])
    for hh in range(MLA_HEADS):
        blk = slice(hh * MLA_HEAD_PAD, (hh + 1) * MLA_HEAD_PAD)
        qb = q[:, blk]
        if rope:
            qb = _rope_block(qb, cos, sin, MLA_ROPE // 4)
        q_ref[:, blk] = qb.astype(BF16)
        kc_ref[:, blk] = (kc[:, blk] + kr).astype(BF16)
    v_ref[...] = _dot(ckvb, wuv_ref[...]).astype(BF16)
    ga_ref[...] = _silu(proj(_GA)).astype(BF16)
    vp_ref[...] = proj(_VP).astype(BF16)
    gb_ref[...] = _silu(proj(_GB)).astype(BF16)


def _mla_in_call(x, mod, gpre, w, latent, rope_tabs):
    n = x.shape[0]
    tm = TOK_TILE
    row = lambda i: (i, 0)
    in_specs = [
        pl.BlockSpec((tm, D_MODEL), row),
        pl.BlockSpec((1, 3, D_MODEL), _mod_row_fn(latent, tm)),
        _full((1, D_MODEL)),
        _full((D_MODEL, MLA_IN_COLS)),
        _full((1, MLA_Q_RANK)),
        _full((1, MLA_KV_RANK)),
        _full((MLA_Q_RANK, MLA_QK_W)),
        _full((MLA_KV_RANK, MLA_QK_W)),
        _full((MLA_KV_RANK, MLA_WIDTH)),
    ]
    args = [x, mod, gpre, w["w_in"], w["g_qn"], w["g_kvn"], w["w_uq"], w["w_uk"], w["w_uv"]]
    if latent:
        per_seq = DEC_SEQ // tm
        in_specs += [pl.BlockSpec((tm, LANES), lambda i: (i % per_seq, 0))] * 2
        args += list(rope_tabs)
    widths = [MLA_QK_W, MLA_QK_W, MLA_WIDTH, MLA_WIDTH, POOL_WIDTH, POOL_WIDTH]
    out_shape = [jax.ShapeDtypeStruct((n, c), BF16) for c in widths]
    out_specs = [pl.BlockSpec((tm, c), row) for c in widths]
    if not latent:
        out_shape += [jax.ShapeDtypeStruct((n, MLA_KV_RANK), F32), jax.ShapeDtypeStruct((n, LANES), F32)]
        out_specs += [pl.BlockSpec((tm, MLA_KV_RANK), row), pl.BlockSpec((tm, LANES), row)]
    return pl.pallas_call(
        functools.partial(_mla_in_body, rope=latent, state=not latent),
        grid=(n // tm,),
        in_specs=in_specs,
        out_specs=out_specs,
        out_shape=out_shape,
        compiler_params=_params(),
        name="mla_in_latent" if latent else "mla_in_context",
    )(*args)


def _mla_ctx_body(ckv_ref, kr_ref, wuk_ref, wuv_ref, kc_ref, v_ref):
    ckvb = ckv_ref[...].astype(BF16)
    kr = kr_ref[...]
    kc = _dot(ckvb, wuk_ref[...])
    for hh in range(MLA_HEADS):
        blk = slice(hh * MLA_HEAD_PAD, (hh + 1) * MLA_HEAD_PAD)
        kc_ref[:, blk] = (kc[:, blk] + kr).astype(BF16)
    v_ref[...] = _dot(ckvb, wuv_ref[...]).astype(BF16)


def _mla_ctx_call(ckv, kr_blk, w):
    n = ckv.shape[0]
    tm = PAST_LEN
    row = lambda i: (i, 0)
    return pl.pallas_call(
        _mla_ctx_body,
        grid=(n // tm,),
        in_specs=[
            pl.BlockSpec((tm, MLA_KV_RANK), row),
            pl.BlockSpec((tm, LANES), row),
            _full((MLA_KV_RANK, MLA_QK_W)),
            _full((MLA_KV_RANK, MLA_WIDTH)),
        ],
        out_specs=[pl.BlockSpec((tm, MLA_QK_W), row), pl.BlockSpec((tm, MLA_WIDTH), row)],
        out_shape=[jax.ShapeDtypeStruct((n, MLA_QK_W), BF16), jax.ShapeDtypeStruct((n, MLA_WIDTH), BF16)],
        compiler_params=_params(),
        name="mla_ctx_kv",
    )(ckv, kr_blk, w["w_uk"], w["w_uv"])


def _mla_attn_body(*refs, has_ctx):
    if has_ctx:
        q_ref, kc_ref, v_ref, kcc_ref, vc_ref, ga_ref, o_ref, va_ref, vca_ref = refs
        aug = ((v_ref, va_ref), (vc_ref, vca_ref))
    else:
        q_ref, kc_ref, v_ref, ga_ref, o_ref, va_ref = refs
        aug = ((v_ref, va_ref),)

    @pl.when(pl.program_id(1) == 0)
    def _():
        for src, dst in aug:
            for j in range(MLA_HEADS // 2):
                dst[:, 2 * j * LANES:(2 * j + 1) * LANES] = src[:, j * LANES:(j + 1) * LANES]
                dst[:, (2 * j + 1) * LANES:(2 * j + 2) * LANES] = jnp.ones((src.shape[0], LANES), BF16)

    def lane_tile_max(t, s):
        for cb in range(s.shape[1] // LANES):
            blk = s[:, cb * LANES:(cb + 1) * LANES]
            t = blk if t is None else jnp.maximum(t, blk)
        return t

    low = _lane_iota((q_ref.shape[0], LANES)) < HALF
    for j in range(MLA_HEADS // 2):
        pair = slice(j * LANES, (j + 1) * LANES)
        vaug = slice(2 * j * LANES, (2 * j + 2) * LANES)
        outs = []
        for hh in (2 * j, 2 * j + 1):
            blk = slice(hh * MLA_HEAD_PAD, (hh + 1) * MLA_HEAD_PAD)
            qh = q_ref[:, blk]
            s = _dot_nt(qh, kc_ref[:, blk])
            t = lane_tile_max(None, s)
            if has_ctx:
                sc = _dot_nt(qh, kcc_ref[:, blk])
                t = lane_tile_max(t, sc)
            m = jnp.max(t, axis=-1, keepdims=True)
            acc = _dot(jnp.exp2(s - m).astype(BF16), va_ref[:, vaug])
            if has_ctx:
                acc = acc + _dot(jnp.exp2(sc - m).astype(BF16), vca_ref[:, vaug])
            outs.append(acc[:, :LANES] / acc[:, LANES:])
        o_pair = jnp.where(low, outs[0], outs[1])
        o_ref[:, pair] = (o_pair * ga_ref[:, pair].astype(F32)).astype(BF16)


def _mla_attn_call(q, kc, v, ga, nbatch, seq, ctx=None):
    tq = min(seq, MLA_Q_TILE)
    scratch = [pltpu.VMEM((seq, 2 * MLA_WIDTH), BF16)]
    if ctx is not None:
        scratch.append(pltpu.VMEM((PAST_LEN, 2 * MLA_WIDTH), BF16))
    nq = seq // tq
    qrow = lambda b, i: (b * nq + i, 0)
    brow = lambda b, i: (b, 0)
    in_specs = [
        pl.BlockSpec((tq, MLA_QK_W), qrow),
        pl.BlockSpec((seq, MLA_QK_W), brow),
        pl.BlockSpec((seq, MLA_WIDTH), brow),
    ]
    args = [q, kc, v]
    if ctx is not None:
        in_specs += [pl.BlockSpec((PAST_LEN, MLA_QK_W), brow), pl.BlockSpec((PAST_LEN, MLA_WIDTH), brow)]
        args += list(ctx)
    in_specs.append(pl.BlockSpec((tq, MLA_WIDTH), qrow))
    args.append(ga)
    return pl.pallas_call(
        functools.partial(_mla_attn_body, has_ctx=ctx is not None),
        grid=(nbatch, nq),
        in_specs=in_specs,
        out_specs=pl.BlockSpec((tq, MLA_WIDTH), qrow),
        out_shape=jax.ShapeDtypeStruct((nbatch * seq, MLA_WIDTH), BF16),
        scratch_shapes=scratch,
        compiler_params=_params2(),
        name="mla_attn_latent" if ctx is not None else "mla_attn_context",
    )(*args)


def _post_residual(out, x_ref, mod_ref, gpost_ref, o_ref):
    m = mod_ref[0]
    o_ref[...] = x_ref[...] + m[2:3, :] * (_rms(out) * gpost_ref[...])


def _mla_out_body(a_ref, vp_ref, vprev_ref, vnext_ref, gb_ref, pm_ref, wp_ref, sp_ref, wo_ref,
                  x_ref, mod_ref, gpost_ref, o_ref, *, seq):
    tm = a_ref.shape[0]
    cur = vp_ref[...]
    slab = jnp.concatenate([vprev_ref[...], cur, vnext_ref[...]], axis=0)
    srow = lax.broadcasted_iota(jnp.int32, (POOL_BLOCK + 2 * POOL_HALO, POOL_WIDTH), 0)
    trow = lax.broadcasted_iota(jnp.int32, (POOL_BLOCK, LANES), 0)
    zero = jnp.zeros((POOL_BLOCK + 2 * POOL_HALO, POOL_WIDTH), BF16)
    pooled = [[] for _ in POOL_WINDOWS]
    for sb in range(tm // POOL_BLOCK):
        pos0 = (pl.program_id(0) * tm + sb * POOL_BLOCK) % seq
        lo_cut = jnp.where(pos0 == 0, POOL_HALO, 0)
        hi_cut = jnp.where(pos0 == seq - POOL_BLOCK, POOL_BLOCK + POOL_HALO, POOL_BLOCK + 2 * POOL_HALO)
        chunk = slab[sb * POOL_BLOCK:(sb + 1) * POOL_BLOCK + 2 * POOL_HALO]
        chunk = jnp.where(srow >= lo_cut, chunk, zero)
        chunk = jnp.where(srow < hi_cut, chunk, zero)
        t = pos0 + trow
        for g, w in enumerate(POOL_WINDOWS):
            cols = slice(g * POOL_GROUP_W, (g + 1) * POOL_GROUP_W)
            tot = _dot(pm_ref[g], chunk[:, cols])
            cnt = jnp.minimum(t + w // 2, seq) - jnp.maximum(t - w // 2, 0)
            own = cur[sb * POOL_BLOCK:(sb + 1) * POOL_BLOCK, cols].astype(F32)
            pooled[g].append((tot / cnt.astype(F32) - own).astype(BF16))
    bs = []
    for g in range(POOL_GROUPS):
        cols = slice(g * POOL_GROUP_W, (g + 1) * POOL_GROUP_W)
        y = _dot(jnp.concatenate(pooled[g], axis=0), wp_ref[g]) * sp_ref[:, cols]
        bs.append((y * gb_ref[:, cols].astype(F32)).astype(BF16))
    b = jnp.concatenate(bs, axis=1)
    out = _dot(a_ref[...], wo_ref[0:MLA_WIDTH, :]) + _dot(b, wo_ref[MLA_WIDTH:, :])
    _post_residual(out, x_ref, mod_ref, gpost_ref, o_ref)


def _pool_masks():
    r = np.arange(POOL_BLOCK)[:, None]
    c = np.arange(POOL_BLOCK + 2 * POOL_HALO)[None, :]
    rel = c - POOL_HALO - r
    ms = [((rel >= -(w // 2)) & (rel <= w // 2 - 1)) for w in POOL_WINDOWS]
    return jnp.asarray(np.stack(ms).astype(np.float32), dtype=BF16)


def _mla_out_call(a, vp, gb, x, mod, gpost, w, latent):
    n = x.shape[0]
    tm = TOK_TILE
    seq = DEC_SEQ if latent else SEQ
    row = lambda i: (i, 0)
    hb = tm // POOL_HALO
    nhb = n // POOL_HALO
    return pl.pallas_call(
        functools.partial(_mla_out_body, seq=seq),
        grid=(n // tm,),
        in_specs=[
            pl.BlockSpec((tm, MLA_WIDTH), row),
            pl.BlockSpec((tm, POOL_WIDTH), row),
            pl.BlockSpec((POOL_HALO, POOL_WIDTH), lambda i: (jnp.maximum(i * hb - 1, 0), 0)),
            pl.BlockSpec((POOL_HALO, POOL_WIDTH), lambda i: (jnp.minimum((i + 1) * hb, nhb - 1), 0)),
            pl.BlockSpec((tm, POOL_WIDTH), row),
            _full((POOL_GROUPS, POOL_BLOCK, POOL_BLOCK + 2 * POOL_HALO)),
            _full((POOL_GROUPS, POOL_GROUP_W, POOL_GROUP_W)),
            _full((1, POOL_WIDTH)),
            _full((MLA_WIDTH + POOL_WIDTH, D_MODEL)),
            pl.BlockSpec((tm, D_MODEL), row),
            pl.BlockSpec((1, 3, D_MODEL), _mod_row_fn(latent, tm)),
            _full((1, D_MODEL)),
        ],
        out_specs=pl.BlockSpec((tm, D_MODEL), row),
        out_shape=jax.ShapeDtypeStruct((n, D_MODEL), F32),
        compiler_params=_params(),
        name="mla_out_latent" if latent else "mla_out_context",
    )(a, vp, vp, vp, gb, _pool_masks(), w["w_pool"], w["s_pool"], w["w_out"], x, mod, gpost)


def _swa_out_body(a_ref, wo_ref, x_ref, mod_ref, gpost_ref, o_ref):
    _post_residual(_dot(a_ref[...], wo_ref[...]), x_ref, mod_ref, gpost_ref, o_ref)


def _swa_out_call(a, x, mod, gpost, w_out, latent):
    n = x.shape[0]
    tm = TOK_TILE
    row = lambda i: (i, 0)
    return pl.pallas_call(
        _swa_out_body,
        grid=(n // tm,),
        in_specs=[
            pl.BlockSpec((tm, SWA_WIDTH), row),
            _full((SWA_WIDTH, D_MODEL)),
            pl.BlockSpec((tm, D_MODEL), row),
            pl.BlockSpec((1, 3, D_MODEL), _mod_row_fn(latent, tm)),
            _full((1, D_MODEL)),
        ],
        out_specs=pl.BlockSpec((tm, D_MODEL), row),
        out_shape=jax.ShapeDtypeStruct((n, D_MODEL), F32),
        compiler_params=_params(),
        name="swa_out_latent" if latent else "swa_out_context",
    )(a, w_out, x, mod, gpost)


_SQ = (0, SWA_WIDTH)
_SK = (_SQ[1], _SQ[1] + SWA_KV_W)
_SV = (_SK[1], _SK[1] + SWA_KV_W)
_SG = (_SV[1], _SV[1] + SWA_WIDTH)
SWA_IN_COLS = _SG[1]
SWA_KV_DUP_W = 2 * SWA_KV_W


def _dup_heads(x, o_ref):
    for b in range(SWA_KV_W // LANES):
        blk = x[:, b * LANES:(b + 1) * LANES]
        rot = pltpu.roll(blk, HALF, 1)
        low = _lane_iota(blk.shape) < HALF
        o_ref[:, (2 * b) * LANES:(2 * b + 1) * LANES] = jnp.where(low, blk, rot).astype(BF16)
        o_ref[:, (2 * b + 1) * LANES:(2 * b + 2) * LANES] = jnp.where(low, rot, blk).astype(BF16)


def _swa_in_body(*refs, rope, state):
    x_ref, mod_ref, g_ref, win_ref = refs[:4]
    refs = refs[4:]
    if rope:
        cos_ref, sin_ref = refs[:2]
        refs = refs[2:]
    q_ref, kd_ref, vd_ref, gs_ref = refs[:4]
    h = _modulated(x_ref, mod_ref, g_ref)

    def proj(c):
        return _dot(h, win_ref[:, c[0]:c[1]])

    q = proj(_SQ) * (SWA_SCALE * LOG2E)
    k = proj(_SK)
    v = proj(_SV)
    if state:
        refs[4][...] = k
        refs[5][...] = v
    if rope:
        cos = cos_ref[...]
        sin = sin_ref[...]
        for b in range(SWA_WIDTH // LANES):
            blk = slice(b * LANES, (b + 1) * LANES)
            q_ref[:, blk] = _rope_block(q[:, blk], cos, sin, SWA_HEAD_DIM // 4).astype(BF16)
        k = jnp.concatenate(
            [_rope_block(k[:, b * LANES:(b + 1) * LANES], cos, sin, SWA_HEAD_DIM // 4)
             for b in range(SWA_KV_W // LANES)], axis=1)
    else:
        q_ref[...] = q.astype(BF16)
    _dup_heads(k, kd_ref)
    _dup_heads(v, vd_ref)
    gs_ref[...] = _silu(proj(_SG)).astype(BF16)


def _swa_in_call(x, mod, gpre, w_in, latent, rope_tabs):
    n = x.shape[0]
    tm = TOK_TILE
    row = lambda i: (i, 0)
    in_specs = [
        pl.BlockSpec((tm, D_MODEL), row),
        pl.BlockSpec((1, 3, D_MODEL), _mod_row_fn(latent, tm)),
        _full((1, D_MODEL)),
        _full((D_MODEL, SWA_IN_COLS)),
    ]
    args = [x, mod, gpre, w_in]
    if latent:
        per_seq = DEC_SEQ // tm
        in_specs += [pl.BlockSpec((tm, LANES), lambda i: (i % per_seq, 0))] * 2
        args += list(rope_tabs)
    widths = [SWA_WIDTH, SWA_KV_DUP_W, SWA_KV_DUP_W, SWA_WIDTH]
    out_shape = [jax.ShapeDtypeStruct((n, c), BF16) for c in widths]
    out_specs = [pl.BlockSpec((tm, c), row) for c in widths]
    if not latent:
        out_shape += [jax.ShapeDtypeStruct((n, SWA_KV_W), F32)] * 2
        out_specs += [pl.BlockSpec((tm, SWA_KV_W), row)] * 2
    return pl.pallas_call(
        functools.partial(_swa_in_body, rope=latent, state=not latent),
        grid=(n // tm,),
        in_specs=in_specs,
        out_specs=out_specs,
        out_shape=out_shape,
        compiler_params=_params(),
        name="swa_in_latent" if latent else "swa_in_context",
    )(*args)


SWA_GROUP = SWA_HEADS // SWA_KV_HEADS


def _swa_group_attend(qa, qb, parts, sinks):
    rows = qa.shape[0]
    low = _lane_iota(qa.shape) < HALF
    zero = jnp.zeros_like(qa)
    qst = jnp.concatenate(
        [jnp.where(low, qa, zero), jnp.where(low, zero, qa), jnp.where(low, qb, zero), jnp.where(low, zero, qb)],
        axis=0)
    sink_t = jnp.concatenate([jnp.full((rows, LANES), s * LOG2E, F32) for s in sinks], axis=0)
    scores = []
    t = sink_t
    for k, _, bias in parts:
        s = _dot_nt(qst, k)
        if bias is not None:
            s = s + bias
        scores.append(s)
        for cb in range(s.shape[1] // LANES):
            t = jnp.maximum(t, s[:, cb * LANES:(cb + 1) * LANES])
    m = jnp.max(t, axis=-1, keepdims=True)
    acc = None
    for s, (_, v, _) in zip(scores, parts):
        p = jnp.exp2(s - m).astype(BF16)
        pv = _dot(p, jnp.concatenate([v, jnp.ones_like(v)], axis=1))
        acc = pv if acc is None else acc + pv
    o = acc[:, :LANES] / (acc[:, LANES:] + jnp.exp2(sink_t - m))
    return (jnp.where(low, o[0:rows], o[rows:2 * rows]),
            jnp.where(low, o[2 * rows:3 * rows], o[3 * rows:4 * rows]))


def _swa_dense_body(sink_ref, q_ref, kd_ref, vd_ref, gs_ref, o_ref):
    for kh in range(SWA_KV_HEADS):
        kv = slice(kh * LANES, (kh + 1) * LANES)
        ta = slice(2 * kh * LANES, (2 * kh + 1) * LANES)
        tb = slice((2 * kh + 1) * LANES, (2 * kh + 2) * LANES)
        sinks = [sink_ref[SWA_GROUP * kh + e] for e in range(SWA_GROUP)]
        oa, ob = _swa_group_attend(q_ref[:, ta], q_ref[:, tb], [(kd_ref[:, kv], vd_ref[:, kv], None)], sinks)
        o_ref[:, ta] = (oa * gs_ref[:, ta].astype(F32)).astype(BF16)
        o_ref[:, tb] = (ob * gs_ref[:, tb].astype(F32)).astype(BF16)


def _swa_dense_call(sink, q, kd, vd, gs):
    n = q.shape[0]
    row = lambda b: (b, 0)
    return pl.pallas_call(
        _swa_dense_body,
        grid=(n // SEQ,),
        in_specs=[
            pl.BlockSpec(memory_space=pltpu.SMEM),
            pl.BlockSpec((SEQ, SWA_WIDTH), row),
            pl.BlockSpec((SEQ, SWA_KV_DUP_W), row),
            pl.BlockSpec((SEQ, SWA_KV_DUP_W), row),
            pl.BlockSpec((SEQ, SWA_WIDTH), row),
        ],
        out_specs=pl.BlockSpec((SEQ, SWA_WIDTH), row),
        out_shape=jax.ShapeDtypeStruct((n, SWA_WIDTH), BF16),
        compiler_params=_params(),
        name="swa_attn_context",
    )(sink, q, kd, vd, gs)


SWA_Q_TILE = 512
SWA_SPAN = 3 * SWA_WINDOW


def _swa_band_body(sink_ref, q_ref, kd_ref, vd_ref, kx_ref, vx_ref, gs_ref, o_ref):
    w = SWA_WINDOW
    nsub = q_ref.shape[0] // w
    step = pl.program_id(1)
    r = lax.broadcasted_iota(jnp.int32, (w, SWA_SPAN), 0)
    c = lax.broadcasted_iota(jnp.int32, (w, SWA_SPAN), 1)

    def block(sb, carry):
        qbase = (step * nsub + sb) * w
        start = pl.multiple_of(jnp.clip(qbase - w, 0, DEC_SEQ - SWA_SPAN), w)
        rel = c - r + (start - qbase)
        bias = jnp.where(jnp.abs(rel) <= w, 0.0, NEG_INF).astype(F32)
        bias = jnp.concatenate([bias] * SWA_GROUP, axis=0)
        rows = pl.ds(pl.multiple_of(sb * w, w), w)
        span = pl.ds(start, SWA_SPAN)
        for kh in range(SWA_KV_HEADS):
            kv = slice(kh * LANES, (kh + 1) * LANES)
            ta = slice(2 * kh * LANES, (2 * kh + 1) * LANES)
            tb = slice((2 * kh + 1) * LANES, (2 * kh + 2) * LANES)
            sinks = [sink_ref[SWA_GROUP * kh + e] for e in range(SWA_GROUP)]
            parts = [(kd_ref[span, kv], vd_ref[span, kv], bias), (kx_ref[:, kv], vx_ref[:, kv], None)]
            oa, ob = _swa_group_attend(q_ref[rows, ta], q_ref[rows, tb], parts, sinks)
            o_ref[rows, ta] = (oa * gs_ref[rows, ta].astype(F32)).astype(BF16)
            o_ref[rows, tb] = (ob * gs_ref[rows, tb].astype(F32)).astype(BF16)
        return carry

    lax.fori_loop(0, nsub, block, 0)


def _swa_band_call(sink, q, kd, vd, kx, vx, gs):
    tq = SWA_Q_TILE
    nq = DEC_SEQ // tq
    qrow = lambda b, i: (b * nq + i, 0)
    brow = lambda b, i: (b, 0)
    return pl.pallas_call(
        _swa_band_body,
        grid=(DEC_BATCH, nq),
        in_specs=[
            pl.BlockSpec(memory_space=pltpu.SMEM),
            pl.BlockSpec((tq, SWA_WIDTH), qrow),
            pl.BlockSpec((DEC_SEQ, SWA_KV_DUP_W), brow),
            pl.BlockSpec((DEC_SEQ, SWA_KV_DUP_W), brow),
            pl.BlockSpec((PAST_LEN, SWA_KV_DUP_W), brow),
            pl.BlockSpec((PAST_LEN, SWA_KV_DUP_W), brow),
            pl.BlockSpec((tq, SWA_WIDTH), qrow),
        ],
        out_specs=pl.BlockSpec((tq, SWA_WIDTH), qrow),
        out_shape=jax.ShapeDtypeStruct((DEC_BATCH * DEC_SEQ, SWA_WIDTH), BF16),
        compiler_params=_params2(),
        name="swa_attn_latent",
    )(sink, q, kd, vd, kx, vx, gs)


def _rope_tables(rope_dims, lane_period, lane_off):
    n = rope_dims // 2
    hf = n // 2
    t = jnp.arange(DEC_SEQ)
    pos = jnp.stack([(t // GRID_W).astype(F32), (t % GRID_W).astype(F32)], axis=0)
    inv = ROPE_BASE ** (-jnp.arange(0, n, 2, dtype=F32) / n)
    ang = pos[:, :, None] * inv[None, None, :]
    cos = jnp.cos(ang)
    sin = jnp.sin(ang)
    cos_d = jnp.concatenate([cos, cos], axis=-1)
    sin_d = jnp.concatenate([-sin, sin], axis=-1)
    cos_r = jnp.concatenate([cos_d[0], cos_d[1]], axis=-1)
    sin_r = jnp.concatenate([sin_d[0], sin_d[1]], axis=-1)
    reps = LANES // lane_period
    pad = ((0, 0), (lane_off, lane_period - lane_off - rope_dims))
    cos_p = jnp.tile(jnp.pad(cos_r, pad, constant_values=1.0), (1, reps))
    sin_p = jnp.tile(jnp.pad(sin_r, pad), (1, reps))
    return cos_p, sin_p


def _prep_mla(i, mla_w_in, mla_g_qn, mla_g_kvn, mla_w_uq, mla_w_ukv, pool_w, pool_scale, mixa_w_out):
    w_in = mla_w_in[i]
    i1 = MLA_Q_RANK
    i2 = i1 + MLA_KV_RANK
    i3 = i2 + MLA_ROPE
    kr = jnp.pad(w_in[:, i2:i3], ((0, 0), (MLA_NOPE, LANES - MLA_NOPE - MLA_ROPE)))
    w_in_r = jnp.concatenate([w_in[:, :i2], kr, w_in[:, i3:]], axis=1).astype(BF16)
    dq = MLA_NOPE + MLA_ROPE
    w_uq = jnp.pad(mla_w_uq[i].reshape(MLA_Q_RANK, MLA_HEADS, dq),
                   ((0, 0), (0, 0), (0, MLA_HEAD_PAD - dq))).reshape(MLA_Q_RANK, MLA_QK_W).astype(BF16)
    ukv = mla_w_ukv[i].reshape(MLA_KV_RANK, MLA_HEADS, MLA_NOPE + MLA_V)
    w_uk = jnp.pad(ukv[:, :, :MLA_NOPE], ((0, 0), (0, 0), (0, MLA_HEAD_PAD - MLA_NOPE)))
    w_uk = w_uk.reshape(MLA_KV_RANK, MLA_QK_W).astype(BF16)
    w_uv = ukv[:, :, MLA_NOPE:].reshape(MLA_KV_RANK, MLA_WIDTH).astype(BF16)
    return {
        "w_in": w_in_r, "w_uq": w_uq, "w_uk": w_uk, "w_uv": w_uv,
        "g_qn": mla_g_qn[i].reshape(1, -1), "g_kvn": mla_g_kvn[i].reshape(1, -1),
        "w_pool": pool_w[i].astype(BF16), "s_pool": pool_scale[i].reshape(1, -1),
        "w_out": mixa_w_out[i].astype(BF16),
    }


def kernel(x_prompt, x_sample, cache_ckv, cache_krope, cache_k, cache_v, c, c_ctx, ada_w, ada_b, norm_pre, norm_post, mla_w_in, mla_g_qn, mla_g_kvn, mla_w_uq, mla_w_ukv, pool_w, pool_scale, mixa_w_out, swa_w_in, swa_sink, swa_w_out):
    n_p = BATCH * SEQ
    n_s = DEC_BATCH * DEC_SEQ
    xp = x_prompt.reshape(n_p, D_MODEL)
    xs = x_sample.reshape(n_s, D_MODEL)
    cvec = jnp.concatenate(
        [c_ctx[None], c, jnp.zeros((MOD_ROWS - 1 - DEC_BATCH, D_MODEL), F32)], axis=0)
    mods = _ada_call(cvec, ada_w, ada_b).reshape(DEPTH, MOD_ROWS, 3, D_MODEL)
    mla_tabs = _rope_tables(MLA_ROPE, LANES, MLA_NOPE)
    swa_tabs = _rope_tables(SWA_HEAD_DIM, SWA_HEAD_DIM, 0)
    st_ckv, st_krope, st_k, st_v = [], [], [], []
    for l in range(DEPTH):
        i = l // 2
        mod = mods[l]
        gpre = norm_pre[l].reshape(1, D_MODEL)
        gpost = norm_post[l].reshape(1, D_MODEL)
        if l % 2 == 0:
            w = _prep_mla(i, mla_w_in, mla_g_qn, mla_g_kvn, mla_w_uq, mla_w_ukv, pool_w, pool_scale,
                          mixa_w_out)
            qp, kcp, vp, gap, vpp, gbp, ckv, krb = _mla_in_call(xp, mod, gpre, w, False, None)
            qs, kcs, vs, gas, vps, gbs = _mla_in_call(xs, mod, gpre, w, True, mla_tabs)
            ctx_kr = jnp.pad(cache_krope[:, i].reshape(DEC_BATCH * PAST_LEN, MLA_ROPE),
                             ((0, 0), (MLA_NOPE, LANES - MLA_NOPE - MLA_ROPE)))
            ctx = _mla_ctx_call(cache_ckv[:, i].reshape(DEC_BATCH * PAST_LEN, MLA_KV_RANK), ctx_kr, w)
            ap = _mla_attn_call(qp, kcp, vp, gap, BATCH, SEQ)
            as_ = _mla_attn_call(qs, kcs, vs, gas, DEC_BATCH, DEC_SEQ, ctx)
            xp = _mla_out_call(ap, vpp, gbp, xp, mod, gpost, w, False)
            xs = _mla_out_call(as_, vps, gbs, xs, mod, gpost, w, True)
            st_ckv.append(ckv.reshape(BATCH, SEQ, MLA_KV_RANK))
            st_krope.append(krb[:, MLA_NOPE:MLA_NOPE + MLA_ROPE].reshape(BATCH, SEQ, MLA_ROPE))
        else:
            w_in = swa_w_in[i].astype(BF16)
            w_out = swa_w_out[i].astype(BF16)
            sink = swa_sink[i]
            qp, kdp, vdp, gsp, kst, vst = _swa_in_call(xp, mod, gpre, w_in, False, None)
            qs, kds, vds, gss = _swa_in_call(xs, mod, gpre, w_in, True, swa_tabs)

            def dup(t):
                t = t[:, i].reshape(DEC_BATCH * PAST_LEN, SWA_KV_HEADS, SWA_HEAD_DIM)
                return jnp.concatenate([t, t], axis=-1).reshape(DEC_BATCH * PAST_LEN, SWA_KV_DUP_W).astype(BF16)

            ap = _swa_dense_call(sink, qp, kdp, vdp, gsp)
            as_ = _swa_band_call(sink, qs, kds, vds, dup(cache_k), dup(cache_v), gss)
            xp = _swa_out_call(ap, xp, mod, gpost, w_out, False)
            xs = _swa_out_call(as_, xs, mod, gpost, w_out, True)
            st_k.append(kst.reshape(BATCH, SEQ, SWA_KV_HEADS, SWA_HEAD_DIM))
            st_v.append(vst.reshape(BATCH, SEQ, SWA_KV_HEADS, SWA_HEAD_DIM))
    return (
        xp.reshape(BATCH, SEQ, D_MODEL),
        xs.reshape(DEC_BATCH, DEC_SEQ, D_MODEL),
        jnp.stack(st_ckv, axis=1),
        jnp.stack(st_krope, axis=1),
        jnp.stack(st_k, axis=1),
        jnp.stack(st_v, axis=1),
    )
```

```python
import functools

import numpy as np
import jax
import jax.numpy as jnp
from jax import lax
from jax.experimental import pallas as pl
from jax.experimental.pallas import tpu as pltpu

F32 = jnp.float32
BF16 = jnp.bfloat16

D_MODEL = 1024
BATCH = 32
SEQ = 256
DEPTH = 4
DEC_BATCH = 8
DEC_SEQ = 2048
PAST_LEN = 256
GRID_W = 64
ROPE_BASE = 10000.0
EPS = 1e-6
NEG_INF = -1e30

MLA_HEADS = 8
MLA_NOPE = 64
MLA_ROPE = 32
MLA_V = 64
MLA_Q_RANK = 384
MLA_KV_RANK = 256
MLA_WIDTH = MLA_HEADS * MLA_V
MLA_SCALE = (MLA_NOPE + MLA_ROPE) ** -0.5
POOL_WINDOWS = (2, 4, 8, 16)
POOL_GROUPS = 4
POOL_GROUP_W = D_MODEL // 8
POOL_WIDTH = POOL_GROUPS * POOL_GROUP_W
SWA_HEADS = 16
SWA_KV_HEADS = 4
SWA_HEAD_DIM = 64
SWA_WINDOW = 128
SWA_WIDTH = SWA_HEADS * SWA_HEAD_DIM
SWA_KV_W = SWA_KV_HEADS * SWA_HEAD_DIM
SWA_SCALE = SWA_HEAD_DIM ** -0.5

LANES = 128
HALF = LANES // 2
BF16_SUBLANES = 16
MOD_ROWS = 16
TOK_TILE = 512
LOG2E = 1.4426950408889634
POOL_BLOCK = 128
POOL_HALO = 64
VMEM_LIMIT = 48 * 1024 * 1024

_CQ = (0, MLA_Q_RANK)
_CKV = (_CQ[1], _CQ[1] + MLA_KV_RANK)
_KR = (_CKV[1], _CKV[1] + LANES)
_GA = (_KR[1], _KR[1] + MLA_WIDTH)
_VP = (_GA[1], _GA[1] + POOL_WIDTH)
_GB = (_VP[1], _VP[1] + POOL_WIDTH)
MLA_IN_COLS = _GB[1]
MLA_HEAD_PAD = LANES
MLA_QK_W = MLA_HEADS * MLA_HEAD_PAD
MLA_Q_TILE = 512


def _params():
    return pltpu.CompilerParams(dimension_semantics=("arbitrary",), vmem_limit_bytes=VMEM_LIMIT)


def _params2():
    return pltpu.CompilerParams(dimension_semantics=("arbitrary", "arbitrary"), vmem_limit_bytes=VMEM_LIMIT)


def _dot(a, b):
    return jnp.dot(a, b, preferred_element_type=F32)


def _dot_nt(a, b):
    return lax.dot_general(a, b, (((1,), (1,)), ((), ())), preferred_element_type=F32)


def _silu(x):
    return x / (1.0 + jnp.exp(-x))


def _rms(x):
    return x * lax.rsqrt(jnp.mean(x * x, axis=-1, keepdims=True) + EPS)


def _lane_iota(shape):
    return lax.broadcasted_iota(jnp.int32, shape, len(shape) - 1)


def _full(shape):
    nd = len(shape)
    return pl.BlockSpec(shape, lambda *_: (0,) * nd)


def _ada_body(c_ref, w_ref, b_ref, o_ref):
    s = _silu(c_ref[...]).astype(BF16)
    o_ref[0] = _dot(s, w_ref[0].astype(BF16)) + b_ref[0]


def _ada_call(cvec, ada_w, ada_b):
    tn = 768
    return pl.pallas_call(
        _ada_body,
        grid=(DEPTH, 3 * D_MODEL // tn),
        in_specs=[
            pl.BlockSpec((MOD_ROWS, D_MODEL), lambda l, j: (0, 0)),
            pl.BlockSpec((1, D_MODEL, tn), lambda l, j: (l, 0, j)),
            pl.BlockSpec((1, 1, tn), lambda l, j: (l, 0, j)),
        ],
        out_specs=pl.BlockSpec((1, MOD_ROWS, tn), lambda l, j: (l, 0, j)),
        out_shape=jax.ShapeDtypeStruct((DEPTH, MOD_ROWS, 3 * D_MODEL), F32),
        compiler_params=_params2(),
        name="ada_mod",
    )(cvec, ada_w, ada_b.reshape(DEPTH, 1, 3 * D_MODEL))


def _modulated(x, mod_ref, g_ref):
    m = mod_ref[0]
    xn = _rms(x) * g_ref[...]
    return (xn * (1.0 + m[1:2, :]) + m[0:1, :]).astype(BF16)


def _rope_block(b, cos, sin, partner):
    first = (_lane_iota(b.shape) & partner) == 0
    sw = jnp.where(first, pltpu.roll(b, LANES - partner, 1), pltpu.roll(b, partner, 1))
    return b * cos + sw * sin


def _mod_row_fn(latent, tile):
    if latent:
        per_seq = DEC_SEQ // tile
        return lambda i: (1 + i // per_seq, 0, 0)
    return lambda i: (0, 0, 0)


def _mla_in_compute(x, ins, refs, *, rope, state):
    mod_ref, g_ref, win_ref, gq_ref, gkv_ref, wuq_ref, wuk_ref, wuv_ref = ins[:8]
    if rope:
        cos_ref, sin_ref = ins[8:10]
    q_ref, kc_ref, v_ref, ga_ref, vp_ref, gb_ref = refs[:6]
    h = _modulated(x, mod_ref, g_ref)

    def proj(c):
        return _dot(h, win_ref[:, c[0]:c[1]])

    cqn = (_rms(proj(_CQ)) * gq_ref[...]).astype(BF16)
    q = _dot(cqn, wuq_ref[...]) * (MLA_SCALE * LOG2E)
    ckvn = _rms(proj(_CKV)) * gkv_ref[...]
    ckvb = ckvn.astype(BF16)
    kr = proj(_KR)
    if state:
        refs[6][...] = ckvn
        refs[7][...] = kr
    if rope:
        cos = cos_ref[...]
        sin = sin_ref[...]
        kr = _rope_block(kr, cos, sin, MLA_ROPE // 4)
    kc = _dot(ckvb, wuk_ref[...])
    for hh in range(MLA_HEADS):
        blk = slice(hh * MLA_HEAD_PAD, (hh + 1) * MLA_HEAD_PAD)
        qb = q[:, blk]
        if rope:
            qb = _rope_block(qb, cos, sin, MLA_ROPE // 4)
        q_ref[:, blk] = qb.astype(BF16)
        kc_ref[:, blk] = (kc[:, blk] + kr).astype(BF16)
    v_ref[...] = _dot(ckvb, wuv_ref[...]).astype(BF16)
    ga_ref[...] = _silu(proj(_GA)).astype(BF16)
    vp_ref[...] = proj(_VP).astype(BF16)
    gb_ref[...] = _silu(proj(_GB)).astype(BF16)


class _Part:
    def __init__(self, name, compute, in_specs, args, out_specs, out_shape):
        self.name = name
        self.compute = compute
        self.in_specs = in_specs
        self.args = args
        self.out_specs = out_specs
        self.out_shape = out_shape


def _mla_in_part(n, mod, gpre, w, latent, rope_tabs):
    tm = TOK_TILE
    row = lambda i: (i, 0)
    in_specs = [
        pl.BlockSpec((1, 3, D_MODEL), _mod_row_fn(latent, tm)),
        _full((1, D_MODEL)),
        _full((D_MODEL, MLA_IN_COLS)),
        _full((1, MLA_Q_RANK)),
        _full((1, MLA_KV_RANK)),
        _full((MLA_Q_RANK, MLA_QK_W)),
        _full((MLA_KV_RANK, MLA_QK_W)),
        _full((MLA_KV_RANK, MLA_WIDTH)),
    ]
    args = [mod, gpre, w["w_in"], w["g_qn"], w["g_kvn"], w["w_uq"], w["w_uk"], w["w_uv"]]
    if latent:
        per_seq = DEC_SEQ // tm
        in_specs += [pl.BlockSpec((tm, LANES), lambda i: (i % per_seq, 0))] * 2
        args += list(rope_tabs)
    widths = [MLA_QK_W, MLA_QK_W, MLA_WIDTH, MLA_WIDTH, POOL_WIDTH, POOL_WIDTH]
    out_shape = [jax.ShapeDtypeStruct((n, c), BF16) for c in widths]
    out_specs = [pl.BlockSpec((tm, c), row) for c in widths]
    if not latent:
        out_shape += [jax.ShapeDtypeStruct((n, MLA_KV_RANK), F32), jax.ShapeDtypeStruct((n, LANES), F32)]
        out_specs += [pl.BlockSpec((tm, MLA_KV_RANK), row), pl.BlockSpec((tm, LANES), row)]
    compute = functools.partial(_mla_in_compute, rope=latent, state=not latent)
    return _Part("mla_in", compute, in_specs, args, out_specs, out_shape)


def _stage_body(*refs, out_part, in_part):
    x_ref = refs[0]
    n_out_in = len(out_part.in_specs) if out_part else 0
    n_in_in = len(in_part.in_specs) if in_part else 0
    out_ins = refs[1:1 + n_out_in]
    in_ins = refs[1 + n_out_in:1 + n_out_in + n_in_in]
    outs = refs[1 + n_out_in + n_in_in:]
    x = x_ref[...]
    if out_part:
        x = out_part.compute(x, out_ins)
        outs[0][...] = x
        outs = outs[1:]
    if in_part:
        in_part.compute(x, in_ins, outs)


def _stage_call(x, latent, out_part, in_part):
    n = x.shape[0]
    tm = TOK_TILE
    row = lambda i: (i, 0)
    in_specs = [pl.BlockSpec((tm, D_MODEL), row)]
    args = [x]
    out_specs, out_shape = [], []
    names = []
    if out_part:
        in_specs += out_part.in_specs
        args += out_part.args
        out_specs.append(pl.BlockSpec((tm, D_MODEL), row))
        out_shape.append(jax.ShapeDtypeStruct((n, D_MODEL), F32))
        names.append(out_part.name)
    if in_part:
        in_specs += in_part.in_specs
        args += in_part.args
        out_specs += in_part.out_specs
        out_shape += in_part.out_shape
        names.append(in_part.name)
    names.append("latent" if latent else "context")
    return pl.pallas_call(
        functools.partial(_stage_body, out_part=out_part, in_part=in_part),
        grid=(n // tm,),
        in_specs=in_specs,
        out_specs=out_specs,
        out_shape=out_shape,
        compiler_params=_params(),
        name="_".join(names),
    )(*args)


def _mla_ctx_body(ckv_ref, kr_ref, wuk_ref, wuv_ref, kc_ref, v_ref):
    ckvb = ckv_ref[...].astype(BF16)
    kr = kr_ref[...]
    kc = _dot(ckvb, wuk_ref[...])
    for hh in range(MLA_HEADS):
        blk = slice(hh * MLA_HEAD_PAD, (hh + 1) * MLA_HEAD_PAD)
        kc_ref[:, blk] = (kc[:, blk] + kr).astype(BF16)
    v_ref[...] = _dot(ckvb, wuv_ref[...]).astype(BF16)


def _mla_ctx_call(ckv, kr_blk, w):
    n = ckv.shape[0]
    tm = PAST_LEN
    row = lambda i: (i, 0)
    return pl.pallas_call(
        _mla_ctx_body,
        grid=(n // tm,),
        in_specs=[
            pl.BlockSpec((tm, MLA_KV_RANK), row),
            pl.BlockSpec((tm, LANES), row),
            _full((MLA_KV_RANK, MLA_QK_W)),
            _full((MLA_KV_RANK, MLA_WIDTH)),
        ],
        out_specs=[pl.BlockSpec((tm, MLA_QK_W), row), pl.BlockSpec((tm, MLA_WIDTH), row)],
        out_shape=[jax.ShapeDtypeStruct((n, MLA_QK_W), BF16), jax.ShapeDtypeStruct((n, MLA_WIDTH), BF16)],
        compiler_params=_params(),
        name="mla_ctx_kv",
    )(ckv, kr_blk, w["w_uk"], w["w_uv"])


def _mla_attn_body(*refs, has_ctx):
    if has_ctx:
        q_ref, kc_ref, v_ref, kcc_ref, vc_ref, ga_ref, o_ref, va_ref, vca_ref = refs
        aug = ((v_ref, va_ref), (vc_ref, vca_ref))
    else:
        q_ref, kc_ref, v_ref, ga_ref, o_ref, va_ref = refs
        aug = ((v_ref, va_ref),)

    @pl.when(pl.program_id(1) == 0)
    def _():
        for src, dst in aug:
            for j in range(MLA_HEADS // 2):
                dst[:, 2 * j * LANES:(2 * j + 1) * LANES] = src[:, j * LANES:(j + 1) * LANES]
                dst[:, (2 * j + 1) * LANES:(2 * j + 2) * LANES] = jnp.ones((src.shape[0], LANES), BF16)

    def lane_tile_max(t, s):
        for cb in range(s.shape[1] // LANES):
            blk = s[:, cb * LANES:(cb + 1) * LANES]
            t = blk if t is None else jnp.maximum(t, blk)
        return t

    low = _lane_iota((q_ref.shape[0], LANES)) < HALF
    for j in range(MLA_HEADS // 2):
        pair = slice(j * LANES, (j + 1) * LANES)
        vaug = slice(2 * j * LANES, (2 * j + 2) * LANES)
        outs = []
        for hh in (2 * j, 2 * j + 1):
            blk = slice(hh * MLA_HEAD_PAD, (hh + 1) * MLA_HEAD_PAD)
            qh = q_ref[:, blk]
            s = _dot_nt(qh, kc_ref[:, blk])
            t = lane_tile_max(None, s)
            if has_ctx:
                sc = _dot_nt(qh, kcc_ref[:, blk])
                t = lane_tile_max(t, sc)
            m = jnp.max(t, axis=-1, keepdims=True)
            acc = _dot(jnp.exp2(s - m).astype(BF16), va_ref[:, vaug])
            if has_ctx:
                acc = acc + _dot(jnp.exp2(sc - m).astype(BF16), vca_ref[:, vaug])
            outs.append(acc[:, :LANES] / acc[:, LANES:])
        o_pair = jnp.where(low, outs[0], outs[1])
        o_ref[:, pair] = (o_pair * ga_ref[:, pair].astype(F32)).astype(BF16)


def _mla_attn_call(q, kc, v, ga, nbatch, seq, ctx=None):
    tq = min(seq, MLA_Q_TILE)
    scratch = [pltpu.VMEM((seq, 2 * MLA_WIDTH), BF16)]
    if ctx is not None:
        scratch.append(pltpu.VMEM((PAST_LEN, 2 * MLA_WIDTH), BF16))
    nq = seq // tq
    qrow = lambda b, i: (b * nq + i, 0)
    brow = lambda b, i: (b, 0)
    in_specs = [
        pl.BlockSpec((tq, MLA_QK_W), qrow),
        pl.BlockSpec((seq, MLA_QK_W), brow),
        pl.BlockSpec((seq, MLA_WIDTH), brow),
    ]
    args = [q, kc, v]
    if ctx is not None:
        in_specs += [pl.BlockSpec((PAST_LEN, MLA_QK_W), brow), pl.BlockSpec((PAST_LEN, MLA_WIDTH), brow)]
        args += list(ctx)
    in_specs.append(pl.BlockSpec((tq, MLA_WIDTH), qrow))
    args.append(ga)
    return pl.pallas_call(
        functools.partial(_mla_attn_body, has_ctx=ctx is not None),
        grid=(nbatch, nq),
        in_specs=in_specs,
        out_specs=pl.BlockSpec((tq, MLA_WIDTH), qrow),
        out_shape=jax.ShapeDtypeStruct((nbatch * seq, MLA_WIDTH), BF16),
        scratch_shapes=scratch,
        compiler_params=_params2(),
        name="mla_attn_latent" if ctx is not None else "mla_attn_context",
    )(*args)


def _post_residual(out, x, mod_ref, gpost_ref):
    m = mod_ref[0]
    return x + m[2:3, :] * (_rms(out) * gpost_ref[...])


def _mla_out_compute(x, ins, *, seq):
    a_ref, vp_ref, vprev_ref, vnext_ref, gb_ref, pm_ref, wp_ref, sp_ref, wo_ref, mod_ref, gpost_ref = ins
    tm = a_ref.shape[0]
    cur = vp_ref[...]
    slab = jnp.concatenate([vprev_ref[...], cur, vnext_ref[...]], axis=0)
    srow = lax.broadcasted_iota(jnp.int32, (POOL_BLOCK + 2 * POOL_HALO, POOL_WIDTH), 0)
    trow = lax.broadcasted_iota(jnp.int32, (POOL_BLOCK, LANES), 0)
    zero = jnp.zeros((POOL_BLOCK + 2 * POOL_HALO, POOL_WIDTH), BF16)
    pooled = [[] for _ in POOL_WINDOWS]
    for sb in range(tm // POOL_BLOCK):
        pos0 = (pl.program_id(0) * tm + sb * POOL_BLOCK) % seq
        lo_cut = jnp.where(pos0 == 0, POOL_HALO, 0)
        hi_cut = jnp.where(pos0 == seq - POOL_BLOCK, POOL_BLOCK + POOL_HALO, POOL_BLOCK + 2 * POOL_HALO)
        chunk = slab[sb * POOL_BLOCK:(sb + 1) * POOL_BLOCK + 2 * POOL_HALO]
        chunk = jnp.where(srow >= lo_cut, chunk, zero)
        chunk = jnp.where(srow < hi_cut, chunk, zero)
        t = pos0 + trow
        for g, w in enumerate(POOL_WINDOWS):
            cols = slice(g * POOL_GROUP_W, (g + 1) * POOL_GROUP_W)
            tot = _dot(pm_ref[g], chunk[:, cols])
            cnt = jnp.minimum(t + w // 2, seq) - jnp.maximum(t - w // 2, 0)
            own = cur[sb * POOL_BLOCK:(sb + 1) * POOL_BLOCK, cols].astype(F32)
            pooled[g].append((tot / cnt.astype(F32) - own).astype(BF16))
    bs = []
    for g in range(POOL_GROUPS):
        cols = slice(g * POOL_GROUP_W, (g + 1) * POOL_GROUP_W)
        y = _dot(jnp.concatenate(pooled[g], axis=0), wp_ref[g]) * sp_ref[:, cols]
        bs.append((y * gb_ref[:, cols].astype(F32)).astype(BF16))
    b = jnp.concatenate(bs, axis=1)
    out = _dot(a_ref[...], wo_ref[0:MLA_WIDTH, :]) + _dot(b, wo_ref[MLA_WIDTH:, :])
    return _post_residual(out, x, mod_ref, gpost_ref)


def _pool_masks():
    r = np.arange(POOL_BLOCK)[:, None]
    c = np.arange(POOL_BLOCK + 2 * POOL_HALO)[None, :]
    rel = c - POOL_HALO - r
    ms = [((rel >= -(w // 2)) & (rel <= w // 2 - 1)) for w in POOL_WINDOWS]
    return jnp.asarray(np.stack(ms).astype(np.float32), dtype=BF16)


def _mla_out_part(n, a, vp, gb, mod, gpost, w, latent):
    tm = TOK_TILE
    seq = DEC_SEQ if latent else SEQ
    row = lambda i: (i, 0)
    hb = tm // POOL_HALO
    nhb = n // POOL_HALO
    in_specs = [
        pl.BlockSpec((tm, MLA_WIDTH), row),
        pl.BlockSpec((tm, POOL_WIDTH), row),
        pl.BlockSpec((POOL_HALO, POOL_WIDTH), lambda i: (jnp.maximum(i * hb - 1, 0), 0)),
        pl.BlockSpec((POOL_HALO, POOL_WIDTH), lambda i: (jnp.minimum((i + 1) * hb, nhb - 1), 0)),
        pl.BlockSpec((tm, POOL_WIDTH), row),
        _full((POOL_GROUPS, POOL_BLOCK, POOL_BLOCK + 2 * POOL_HALO)),
        _full((POOL_GROUPS, POOL_GROUP_W, POOL_GROUP_W)),
        _full((1, POOL_WIDTH)),
        _full((MLA_WIDTH + POOL_WIDTH, D_MODEL)),
        pl.BlockSpec((1, 3, D_MODEL), _mod_row_fn(latent, tm)),
        _full((1, D_MODEL)),
    ]
    args = [a, vp, vp, vp, gb, _pool_masks(), w["w_pool"], w["s_pool"], w["w_out"], mod, gpost]
    return _Part("mla_out", functools.partial(_mla_out_compute, seq=seq), in_specs, args, None, None)


def _swa_out_compute(x, ins):
    a_ref, wo_ref, mod_ref, gpost_ref = ins
    return _post_residual(_dot(a_ref[...], wo_ref[...]), x, mod_ref, gpost_ref)


def _swa_out_part(a, mod, gpost, w_out, latent):
    tm = TOK_TILE
    in_specs = [
        pl.BlockSpec((tm, SWA_WIDTH), lambda i: (i, 0)),
        _full((SWA_WIDTH, D_MODEL)),
        pl.BlockSpec((1, 3, D_MODEL), _mod_row_fn(latent, tm)),
        _full((1, D_MODEL)),
    ]
    return _Part("swa_out", _swa_out_compute, in_specs, [a, w_out, mod, gpost], None, None)


_SQ = (0, SWA_WIDTH)
_SK = (_SQ[1], _SQ[1] + SWA_KV_W)
_SV = (_SK[1], _SK[1] + SWA_KV_W)
_SG = (_SV[1], _SV[1] + SWA_WIDTH)
SWA_IN_COLS = _SG[1]
SWA_KV_DUP_W = 2 * SWA_KV_W


def _dup_heads(x, o_ref):
    for b in range(SWA_KV_W // LANES):
        blk = x[:, b * LANES:(b + 1) * LANES]
        rot = pltpu.roll(blk, HALF, 1)
        low = _lane_iota(blk.shape) < HALF
        o_ref[:, (2 * b) * LANES:(2 * b + 1) * LANES] = jnp.where(low, blk, rot).astype(BF16)
        o_ref[:, (2 * b + 1) * LANES:(2 * b + 2) * LANES] = jnp.where(low, rot, blk).astype(BF16)


def _swa_in_compute(x, ins, refs, *, rope, state):
    mod_ref, g_ref, win_ref = ins[:3]
    if rope:
        cos_ref, sin_ref = ins[3:5]
    q_ref, kd_ref, vd_ref, gs_ref = refs[:4]
    h = _modulated(x, mod_ref, g_ref)

    def proj(c):
        return _dot(h, win_ref[:, c[0]:c[1]])

    q = proj(_SQ) * (SWA_SCALE * LOG2E)
    k = proj(_SK)
    v = proj(_SV)
    if state:
        refs[4][...] = k
        refs[5][...] = v
    if rope:
        cos = cos_ref[...]
        sin = sin_ref[...]
        for b in range(SWA_WIDTH // LANES):
            blk = slice(b * LANES, (b + 1) * LANES)
            q_ref[:, blk] = _rope_block(q[:, blk], cos, sin, SWA_HEAD_DIM // 4).astype(BF16)
        k = jnp.concatenate(
            [_rope_block(k[:, b * LANES:(b + 1) * LANES], cos, sin, SWA_HEAD_DIM // 4)
             for b in range(SWA_KV_W // LANES)], axis=1)
    else:
        q_ref[...] = q.astype(BF16)
    _dup_heads(k, kd_ref)
    _dup_heads(v, vd_ref)
    gs_ref[...] = _silu(proj(_SG)).astype(BF16)


def _swa_in_part(n, mod, gpre, w_in, latent, rope_tabs):
    tm = TOK_TILE
    row = lambda i: (i, 0)
    in_specs = [
        pl.BlockSpec((1, 3, D_MODEL), _mod_row_fn(latent, tm)),
        _full((1, D_MODEL)),
        _full((D_MODEL, SWA_IN_COLS)),
    ]
    args = [mod, gpre, w_in]
    if latent:
        per_seq = DEC_SEQ // tm
        in_specs += [pl.BlockSpec((tm, LANES), lambda i: (i % per_seq, 0))] * 2
        args += list(rope_tabs)
    widths = [SWA_WIDTH, SWA_KV_DUP_W, SWA_KV_DUP_W, SWA_WIDTH]
    out_shape = [jax.ShapeDtypeStruct((n, c), BF16) for c in widths]
    out_specs = [pl.BlockSpec((tm, c), row) for c in widths]
    if not latent:
        out_shape += [jax.ShapeDtypeStruct((n, SWA_KV_W), F32)] * 2
        out_specs += [pl.BlockSpec((tm, SWA_KV_W), row)] * 2
    compute = functools.partial(_swa_in_compute, rope=latent, state=not latent)
    return _Part("swa_in", compute, in_specs, args, out_specs, out_shape)


SWA_GROUP = SWA_HEADS // SWA_KV_HEADS


def _swa_group_attend(qa, qb, parts, sinks):
    rows = qa.shape[0]
    low = _lane_iota(qa.shape) < HALF
    zero = jnp.zeros_like(qa)
    qst = jnp.concatenate(
        [jnp.where(low, qa, zero), jnp.where(low, zero, qa), jnp.where(low, qb, zero), jnp.where(low, zero, qb)],
        axis=0)
    sink_t = jnp.concatenate([jnp.full((rows, LANES), s * LOG2E, F32) for s in sinks], axis=0)
    scores = []
    t = sink_t
    for k, _, bias in parts:
        s = _dot_nt(qst, k)
        if bias is not None:
            s = s + bias
        scores.append(s)
        for cb in range(s.shape[1] // LANES):
            t = jnp.maximum(t, s[:, cb * LANES:(cb + 1) * LANES])
    m = jnp.max(t, axis=-1, keepdims=True)
    acc = None
    for s, (_, v, _) in zip(scores, parts):
        p = jnp.exp2(s - m).astype(BF16)
        pv = _dot(p, jnp.concatenate([v, jnp.ones_like(v)], axis=1))
        acc = pv if acc is None else acc + pv
    o = acc[:, :LANES] / (acc[:, LANES:] + jnp.exp2(sink_t - m))
    return (jnp.where(low, o[0:rows], o[rows:2 * rows]),
            jnp.where(low, o[2 * rows:3 * rows], o[3 * rows:4 * rows]))


def _swa_dense_body(sink_ref, q_ref, kd_ref, vd_ref, gs_ref, o_ref):
    for kh in range(SWA_KV_HEADS):
        kv = slice(kh * LANES, (kh + 1) * LANES)
        ta = slice(2 * kh * LANES, (2 * kh + 1) * LANES)
        tb = slice((2 * kh + 1) * LANES, (2 * kh + 2) * LANES)
        sinks = [sink_ref[SWA_GROUP * kh + e] for e in range(SWA_GROUP)]
        oa, ob = _swa_group_attend(q_ref[:, ta], q_ref[:, tb], [(kd_ref[:, kv], vd_ref[:, kv], None)], sinks)
        o_ref[:, ta] = (oa * gs_ref[:, ta].astype(F32)).astype(BF16)
        o_ref[:, tb] = (ob * gs_ref[:, tb].astype(F32)).astype(BF16)


def _swa_dense_call(sink, q, kd, vd, gs):
    n = q.shape[0]
    row = lambda b: (b, 0)
    return pl.pallas_call(
        _swa_dense_body,
        grid=(n // SEQ,),
        in_specs=[
            pl.BlockSpec(memory_space=pltpu.SMEM),
            pl.BlockSpec((SEQ, SWA_WIDTH), row),
            pl.BlockSpec((SEQ, SWA_KV_DUP_W), row),
            pl.BlockSpec((SEQ, SWA_KV_DUP_W), row),
            pl.BlockSpec((SEQ, SWA_WIDTH), row),
        ],
        out_specs=pl.BlockSpec((SEQ, SWA_WIDTH), row),
        out_shape=jax.ShapeDtypeStruct((n, SWA_WIDTH), BF16),
        compiler_params=_params(),
        name="swa_attn_context",
    )(sink, q, kd, vd, gs)


SWA_Q_TILE = 512
SWA_SPAN = 3 * SWA_WINDOW


def _swa_band_body(sink_ref, q_ref, kd_ref, vd_ref, kx_ref, vx_ref, gs_ref, o_ref):
    w = SWA_WINDOW
    nsub = q_ref.shape[0] // w
    step = pl.program_id(1)
    r = lax.broadcasted_iota(jnp.int32, (w, SWA_SPAN), 0)
    c = lax.broadcasted_iota(jnp.int32, (w, SWA_SPAN), 1)

    def block(sb, carry):
        qbase = (step * nsub + sb) * w
        start = pl.multiple_of(jnp.clip(qbase - w, 0, DEC_SEQ - SWA_SPAN), w)
        rel = c - r + (start - qbase)
        bias = jnp.where(jnp.abs(rel) <= w, 0.0, NEG_INF).astype(F32)
        bias = jnp.concatenate([bias] * SWA_GROUP, axis=0)
        rows = pl.ds(pl.multiple_of(sb * w, w), w)
        span = pl.ds(start, SWA_SPAN)
        for kh in range(SWA_KV_HEADS):
            kv = slice(kh * LANES, (kh + 1) * LANES)
            ta = slice(2 * kh * LANES, (2 * kh + 1) * LANES)
            tb = slice((2 * kh + 1) * LANES, (2 * kh + 2) * LANES)
            sinks = [sink_ref[SWA_GROUP * kh + e] for e in range(SWA_GROUP)]
            parts = [(kd_ref[span, kv], vd_ref[span, kv], bias), (kx_ref[:, kv], vx_ref[:, kv], None)]
            oa, ob = _swa_group_attend(q_ref[rows, ta], q_ref[rows, tb], parts, sinks)
            o_ref[rows, ta] = (oa * gs_ref[rows, ta].astype(F32)).astype(BF16)
            o_ref[rows, tb] = (ob * gs_ref[rows, tb].astype(F32)).astype(BF16)
        return carry

    lax.fori_loop(0, nsub, block, 0)


def _swa_band_call(sink, q, kd, vd, kx, vx, gs):
    tq = SWA_Q_TILE
    nq = DEC_SEQ // tq
    qrow = lambda b, i: (b * nq + i, 0)
    brow = lambda b, i: (b, 0)
    return pl.pallas_call(
        _swa_band_body,
        grid=(DEC_BATCH, nq),
        in_specs=[
            pl.BlockSpec(memory_space=pltpu.SMEM),
            pl.BlockSpec((tq, SWA_WIDTH), qrow),
            pl.BlockSpec((DEC_SEQ, SWA_KV_DUP_W), brow),
            pl.BlockSpec((DEC_SEQ, SWA_KV_DUP_W), brow),
            pl.BlockSpec((PAST_LEN, SWA_KV_DUP_W), brow),
            pl.BlockSpec((PAST_LEN, SWA_KV_DUP_W), brow),
            pl.BlockSpec((tq, SWA_WIDTH), qrow),
        ],
        out_specs=pl.BlockSpec((tq, SWA_WIDTH), qrow),
        out_shape=jax.ShapeDtypeStruct((DEC_BATCH * DEC_SEQ, SWA_WIDTH), BF16),
        compiler_params=_params2(),
        name="swa_attn_latent",
    )(sink, q, kd, vd, kx, vx, gs)


def _rope_tables(rope_dims, lane_period, lane_off):
    n = rope_dims // 2
    hf = n // 2
    t = jnp.arange(DEC_SEQ)
    pos = jnp.stack([(t // GRID_W).astype(F32), (t % GRID_W).astype(F32)], axis=0)
    inv = ROPE_BASE ** (-jnp.arange(0, n, 2, dtype=F32) / n)
    ang = pos[:, :, None] * inv[None, None, :]
    cos = jnp.cos(ang)
    sin = jnp.sin(ang)
    cos_d = jnp.concatenate([cos, cos], axis=-1)
    sin_d = jnp.concatenate([-sin, sin], axis=-1)
    cos_r = jnp.concatenate([cos_d[0], cos_d[1]], axis=-1)
    sin_r = jnp.concatenate([sin_d[0], sin_d[1]], axis=-1)
    reps = LANES // lane_period
    pad = ((0, 0), (lane_off, lane_period - lane_off - rope_dims))
    cos_p = jnp.tile(jnp.pad(cos_r, pad, constant_values=1.0), (1, reps))
    sin_p = jnp.tile(jnp.pad(sin_r, pad), (1, reps))
    return cos_p, sin_p


def _prep_mla(i, mla_w_in, mla_g_qn, mla_g_kvn, mla_w_uq, mla_w_ukv, pool_w, pool_scale, mixa_w_out):
    w_in = mla_w_in[i]
    i1 = MLA_Q_RANK
    i2 = i1 + MLA_KV_RANK
    i3 = i2 + MLA_ROPE
    kr = jnp.pad(w_in[:, i2:i3], ((0, 0), (MLA_NOPE, LANES - MLA_NOPE - MLA_ROPE)))
    w_in_r = jnp.concatenate([w_in[:, :i2], kr, w_in[:, i3:]], axis=1).astype(BF16)
    dq = MLA_NOPE + MLA_ROPE
    w_uq = jnp.pad(mla_w_uq[i].reshape(MLA_Q_RANK, MLA_HEADS, dq),
                   ((0, 0), (0, 0), (0, MLA_HEAD_PAD - dq))).reshape(MLA_Q_RANK, MLA_QK_W).astype(BF16)
    ukv = mla_w_ukv[i].reshape(MLA_KV_RANK, MLA_HEADS, MLA_NOPE + MLA_V)
    w_uk = jnp.pad(ukv[:, :, :MLA_NOPE], ((0, 0), (0, 0), (0, MLA_HEAD_PAD - MLA_NOPE)))
    w_uk = w_uk.reshape(MLA_KV_RANK, MLA_QK_W).astype(BF16)
    w_uv = ukv[:, :, MLA_NOPE:].reshape(MLA_KV_RANK, MLA_WIDTH).astype(BF16)
    return {
        "w_in": w_in_r, "w_uq": w_uq, "w_uk": w_uk, "w_uv": w_uv,
        "g_qn": mla_g_qn[i].reshape(1, -1), "g_kvn": mla_g_kvn[i].reshape(1, -1),
        "w_pool": pool_w[i].astype(BF16), "s_pool": pool_scale[i].reshape(1, -1),
        "w_out": mixa_w_out[i].astype(BF16),
    }


def kernel(x_prompt, x_sample, cache_ckv, cache_krope, cache_k, cache_v, c, c_ctx, ada_w, ada_b, norm_pre, norm_post, mla_w_in, mla_g_qn, mla_g_kvn, mla_w_uq, mla_w_ukv, pool_w, pool_scale, mixa_w_out, swa_w_in, swa_sink, swa_w_out):
    n_p = BATCH * SEQ
    n_s = DEC_BATCH * DEC_SEQ
    xp = x_prompt.reshape(n_p, D_MODEL)
    xs = x_sample.reshape(n_s, D_MODEL)
    cvec = jnp.concatenate(
        [c_ctx[None], c, jnp.zeros((MOD_ROWS - 1 - DEC_BATCH, D_MODEL), F32)], axis=0)
    mods = _ada_call(cvec, ada_w, ada_b).reshape(DEPTH, MOD_ROWS, 3, D_MODEL)
    mla_tabs = _rope_tables(MLA_ROPE, LANES, MLA_NOPE)
    swa_tabs = _rope_tables(SWA_HEAD_DIM, SWA_HEAD_DIM, 0)
    st_ckv, st_krope, st_k, st_v = [], [], [], []
    pend_p = pend_s = None

    def stage(x, latent, out_part, in_part):
        res = list(_stage_call(x, latent, out_part, in_part))
        if out_part is not None:
            return res[0], res[1:]
        return x, res

    for l in range(DEPTH):
        i = l // 2
        mod = mods[l]
        gpre = norm_pre[l].reshape(1, D_MODEL)
        gpost = norm_post[l].reshape(1, D_MODEL)
        if l % 2 == 0:
            w = _prep_mla(i, mla_w_in, mla_g_qn, mla_g_kvn, mla_w_uq, mla_w_ukv, pool_w, pool_scale,
                          mixa_w_out)
            xp, (qp, kcp, vp, gap, vpp, gbp, ckv, krb) = stage(
                xp, False, pend_p, _mla_in_part(n_p, mod, gpre, w, False, None))
            xs, (qs, kcs, vs, gas, vps, gbs) = stage(
                xs, True, pend_s, _mla_in_part(n_s, mod, gpre, w, True, mla_tabs))
            ctx_kr = jnp.pad(cache_krope[:, i].reshape(DEC_BATCH * PAST_LEN, MLA_ROPE),
                             ((0, 0), (MLA_NOPE, LANES - MLA_NOPE - MLA_ROPE)))
            ctx = _mla_ctx_call(cache_ckv[:, i].reshape(DEC_BATCH * PAST_LEN, MLA_KV_RANK), ctx_kr, w)
            ap = _mla_attn_call(qp, kcp, vp, gap, BATCH, SEQ)
            as_ = _mla_attn_call(qs, kcs, vs, gas, DEC_BATCH, DEC_SEQ, ctx)
            pend_p = _mla_out_part(n_p, ap, vpp, gbp, mod, gpost, w, False)
            pend_s = _mla_out_part(n_s, as_, vps, gbs, mod, gpost, w, True)
            st_ckv.append(ckv.reshape(BATCH, SEQ, MLA_KV_RANK))
            st_krope.append(krb[:, MLA_NOPE:MLA_NOPE + MLA_ROPE].reshape(BATCH, SEQ, MLA_ROPE))
        else:
            w_in = swa_w_in[i].astype(BF16)
            w_out = swa_w_out[i].astype(BF16)
            sink = swa_sink[i]
            xp, (qp, kdp, vdp, gsp, kst, vst) = stage(
                xp, False, pend_p, _swa_in_part(n_p, mod, gpre, w_in, False, None))
            xs, (qs, kds, vds, gss) = stage(
                xs, True, pend_s, _swa_in_part(n_s, mod, gpre, w_in, True, swa_tabs))

            def dup(t):
                t = t[:, i].reshape(DEC_BATCH * PAST_LEN, SWA_KV_HEADS, SWA_HEAD_DIM)
                return jnp.concatenate([t, t], axis=-1).reshape(DEC_BATCH * PAST_LEN, SWA_KV_DUP_W).astype(BF16)

            ap = _swa_dense_call(sink, qp, kdp, vdp, gsp)
            as_ = _swa_band_call(sink, qs, kds, vds, dup(cache_k), dup(cache_v), gss)
            pend_p = _swa_out_part(ap, mod, gpost, w_out, False)
            pend_s = _swa_out_part(as_, mod, gpost, w_out, True)
            st_k.append(kst.reshape(BATCH, SEQ, SWA_KV_HEADS, SWA_HEAD_DIM))
            st_v.append(vst.reshape(BATCH, SEQ, SWA_KV_HEADS, SWA_HEAD_DIM))
    xp, _ = stage(xp, False, pend_p, None)
    xs, _ = stage(xs, True, pend_s, None)
    return (
        xp.reshape(BATCH, SEQ, D_MODEL),
        xs.reshape(DEC_BATCH, DEC_SEQ, D_MODEL),
        jnp.stack(st_ckv, axis=1),
        jnp.stack(st_krope, axis=1),
        jnp.stack(st_k, axis=1),
        jnp.stack(st_v, axis=1),
    )
```

```python
import functools

import numpy as np
import jax
import jax.numpy as jnp
from jax import lax
from jax.experimental import pallas as pl
from jax.experimental.pallas import tpu as pltpu

F32 = jnp.float32
BF16 = jnp.bfloat16

D_MODEL = 1024
BATCH = 32
SEQ = 256
DEPTH = 4
DEC_BATCH = 8
DEC_SEQ = 2048
PAST_LEN = 256
GRID_W = 64
ROPE_BASE = 10000.0
EPS = 1e-6
NEG_INF = -1e30

MLA_HEADS = 8
MLA_NOPE = 64
MLA_ROPE = 32
MLA_V = 64
MLA_Q_RANK = 384
MLA_KV_RANK = 256
MLA_WIDTH = MLA_HEADS * MLA_V
MLA_SCALE = (MLA_NOPE + MLA_ROPE) ** -0.5
POOL_WINDOWS = (2, 4, 8, 16)
POOL_GROUPS = 4
POOL_GROUP_W = D_MODEL // 8
POOL_WIDTH = POOL_GROUPS * POOL_GROUP_W
SWA_HEADS = 16
SWA_KV_HEADS = 4
SWA_HEAD_DIM = 64
SWA_WINDOW = 128
SWA_WIDTH = SWA_HEADS * SWA_HEAD_DIM
SWA_KV_W = SWA_KV_HEADS * SWA_HEAD_DIM
SWA_SCALE = SWA_HEAD_DIM ** -0.5

LANES = 128
HALF = LANES // 2
BF16_SUBLANES = 16
MOD_ROWS = 16
TOK_TILE = 1024
STAGE_CHUNKS = 1
LOG2E = 1.4426950408889634
POOL_BLOCK = 128
POOL_HALO = 64
VMEM_LIMIT = 56 * 1024 * 1024

_CQ = (0, MLA_Q_RANK)
_CKV = (_CQ[1], _CQ[1] + MLA_KV_RANK)
_KR = (_CKV[1], _CKV[1] + LANES)
_GA = (_KR[1], _KR[1] + MLA_WIDTH)
_VP = (_GA[1], _GA[1] + POOL_WIDTH)
_GB = (_VP[1], _VP[1] + POOL_WIDTH)
MLA_IN_COLS = _GB[1]
MLA_HEAD_PAD = LANES
MLA_QK_W = MLA_HEADS * MLA_HEAD_PAD
MLA_Q_TILE = 512


def _params():
    return pltpu.CompilerParams(dimension_semantics=("arbitrary",), vmem_limit_bytes=VMEM_LIMIT)


def _params2():
    return pltpu.CompilerParams(dimension_semantics=("arbitrary", "arbitrary"), vmem_limit_bytes=VMEM_LIMIT)


def _dot(a, b):
    return jnp.dot(a, b, preferred_element_type=F32)


def _dot_nt(a, b):
    return lax.dot_general(a, b, (((1,), (1,)), ((), ())), preferred_element_type=F32)


def _silu(x):
    return x / (1.0 + jnp.exp(-x))


def _rms(x):
    return x * lax.rsqrt(jnp.mean(x * x, axis=-1, keepdims=True) + EPS)


def _lane_iota(shape):
    return lax.broadcasted_iota(jnp.int32, shape, len(shape) - 1)


def _full(shape):
    nd = len(shape)
    return pl.BlockSpec(shape, lambda *_: (0,) * nd)


def _ada_body(c_ref, w_ref, b_ref, o_ref):
    s = _silu(c_ref[...]).astype(BF16)
    o_ref[0] = _dot(s, w_ref[0].astype(BF16)) + b_ref[0]


def _ada_call(cvec, ada_w, ada_b):
    tn = 768
    return pl.pallas_call(
        _ada_body,
        grid=(DEPTH, 3 * D_MODEL // tn),
        in_specs=[
            pl.BlockSpec((MOD_ROWS, D_MODEL), lambda l, j: (0, 0)),
            pl.BlockSpec((1, D_MODEL, tn), lambda l, j: (l, 0, j)),
            pl.BlockSpec((1, 1, tn), lambda l, j: (l, 0, j)),
        ],
        out_specs=pl.BlockSpec((1, MOD_ROWS, tn), lambda l, j: (l, 0, j)),
        out_shape=jax.ShapeDtypeStruct((DEPTH, MOD_ROWS, 3 * D_MODEL), F32),
        compiler_params=_params2(),
        name="ada_mod",
    )(cvec, ada_w, ada_b.reshape(DEPTH, 1, 3 * D_MODEL))


def _modulated(x, mod_ref, g_ref):
    m = mod_ref[0]
    gain = g_ref[...] * (1.0 + m[1:2, :])
    return (_rms(x) * gain + m[0:1, :]).astype(BF16)


def _rope_block(b, cos, sin, partner):
    first = (_lane_iota(b.shape) & partner) == 0
    sw = jnp.where(first, pltpu.roll(b, LANES - partner, 1), pltpu.roll(b, partner, 1))
    return b * cos + sw * sin


def _mod_row_fn(latent, tile):
    if latent:
        per_seq = DEC_SEQ // tile
        return lambda i: (1 + i // per_seq, 0, 0)
    return lambda i: (0, 0, 0)


def _mla_in_compute(x, ins, refs, *, rope, state, r0):
    mod_ref, g_ref, win_ref, gq_ref, gkv_ref, wuq_ref, wuk_ref, wuv_ref = ins[:8]
    if rope:
        cos_ref, sin_ref = ins[8:10]
    q_ref, kc_ref, v_ref, ga_ref, vp_ref, gb_ref = refs[:6]
    h = _modulated(x, mod_ref, g_ref)

    def proj(c):
        return _dot(h, win_ref[:, c[0]:c[1]])

    cqn = (_rms(proj(_CQ)) * gq_ref[...]).astype(BF16)
    q = _dot(cqn, wuq_ref[...]) * (MLA_SCALE * LOG2E)
    ckvn = _rms(proj(_CKV)) * gkv_ref[...]
    ckvb = ckvn.astype(BF16)
    kr = proj(_KR)
    if state:
        refs[6][...] = ckvn
        refs[7][...] = kr
    if rope:
        cos = cos_ref[...]
        sin = sin_ref[...]
        kr = _rope_block(kr, cos, sin, MLA_ROPE // 4)
    kc = _dot(ckvb, wuk_ref[...])
    for hh in range(MLA_HEADS):
        blk = slice(hh * MLA_HEAD_PAD, (hh + 1) * MLA_HEAD_PAD)
        qb = q[:, blk]
        if rope:
            qb = _rope_block(qb, cos, sin, MLA_ROPE // 4)
        q_ref[:, blk] = qb.astype(BF16)
        kc_ref[:, blk] = (kc[:, blk] + kr).astype(BF16)
    v_ref[...] = _dot(ckvb, wuv_ref[...]).astype(BF16)
    ga_ref[...] = _silu(proj(_GA)).astype(BF16)
    vp_ref[...] = proj(_VP).astype(BF16)
    gb_ref[...] = _silu(proj(_GB)).astype(BF16)


class _Part:
    def __init__(self, name, compute, in_specs, args, row_tiled, out_specs, out_shape):
        self.name = name
        self.compute = compute
        self.in_specs = in_specs
        self.args = args
        self.row_tiled = row_tiled
        self.out_specs = out_specs
        self.out_shape = out_shape

    def views(self, refs, r0, rows):
        return [r.at[pl.ds(r0, rows)] if t else r for r, t in zip(refs, self.row_tiled)]


def _mla_in_part(n, mod, gpre, w, latent, rope_tabs):
    tm = TOK_TILE
    row = lambda i: (i, 0)
    in_specs = [
        pl.BlockSpec((1, 3, D_MODEL), _mod_row_fn(latent, tm)),
        _full((1, D_MODEL)),
        _full((D_MODEL, MLA_IN_COLS)),
        _full((1, MLA_Q_RANK)),
        _full((1, MLA_KV_RANK)),
        _full((MLA_Q_RANK, MLA_QK_W)),
        _full((MLA_KV_RANK, MLA_QK_W)),
        _full((MLA_KV_RANK, MLA_WIDTH)),
    ]
    args = [mod, gpre, w["w_in"], w["g_qn"], w["g_kvn"], w["w_uq"], w["w_uk"], w["w_uv"]]
    if latent:
        per_seq = DEC_SEQ // tm
        in_specs += [pl.BlockSpec((tm, LANES), lambda i: (i % per_seq, 0))] * 2
        args += list(rope_tabs)
    widths = [MLA_QK_W, MLA_QK_W, MLA_WIDTH, MLA_WIDTH, POOL_WIDTH, POOL_WIDTH]
    out_shape = [jax.ShapeDtypeStruct((n, c), BF16) for c in widths]
    out_specs = [pl.BlockSpec((tm, c), row) for c in widths]
    if not latent:
        out_shape += [jax.ShapeDtypeStruct((n, MLA_KV_RANK), F32), jax.ShapeDtypeStruct((n, LANES), F32)]
        out_specs += [pl.BlockSpec((tm, MLA_KV_RANK), row), pl.BlockSpec((tm, LANES), row)]
    compute = functools.partial(_mla_in_compute, rope=latent, state=not latent)
    tiled = [False] * 8 + [True] * (len(in_specs) - 8)
    return _Part("mla_in", compute, in_specs, args, tiled, out_specs, out_shape)


def _stage_body(*refs, out_part, in_part):
    x_ref = refs[0]
    n_out_in = len(out_part.in_specs) if out_part else 0
    n_in_in = len(in_part.in_specs) if in_part else 0
    out_ins = refs[1:1 + n_out_in]
    in_ins = refs[1 + n_out_in:1 + n_out_in + n_in_in]
    outs = refs[1 + n_out_in + n_in_in:]
    rows = x_ref.shape[0] // STAGE_CHUNKS
    for ci in range(STAGE_CHUNKS):
        r0 = ci * rows
        x = x_ref[r0:r0 + rows, :]
        o = outs
        if out_part:
            x = out_part.compute(x, out_part.views(out_ins, r0, rows), r0=r0)
            o[0][r0:r0 + rows, :] = x
            o = o[1:]
        if in_part:
            in_part.compute(x, in_part.views(in_ins, r0, rows), [r.at[pl.ds(r0, rows)] for r in o], r0=r0)


def _stage_call(x, latent, out_part, in_part):
    n = x.shape[0]
    tm = TOK_TILE
    row = lambda i: (i, 0)
    in_specs = [pl.BlockSpec((tm, D_MODEL), row)]
    args = [x]
    out_specs, out_shape = [], []
    names = []
    if out_part:
        in_specs += out_part.in_specs
        args += out_part.args
        out_specs.append(pl.BlockSpec((tm, D_MODEL), row))
        out_shape.append(jax.ShapeDtypeStruct((n, D_MODEL), F32))
        names.append(out_part.name)
    if in_part:
        in_specs += in_part.in_specs
        args += in_part.args
        out_specs += in_part.out_specs
        out_shape += in_part.out_shape
        names.append(in_part.name)
    names.append("latent" if latent else "context")
    return pl.pallas_call(
        functools.partial(_stage_body, out_part=out_part, in_part=in_part),
        grid=(n // tm,),
        in_specs=in_specs,
        out_specs=out_specs,
        out_shape=out_shape,
        compiler_params=_params(),
        name="_".join(names),
    )(*args)


def _mla_ctx_body(ckv_ref, kr_ref, wuk_ref, wuv_ref, kc_ref, v_ref):
    ckvb = ckv_ref[...].astype(BF16)
    kr = kr_ref[...]
    kc = _dot(ckvb, wuk_ref[...])
    for hh in range(MLA_HEADS):
        blk = slice(hh * MLA_HEAD_PAD, (hh + 1) * MLA_HEAD_PAD)
        kc_ref[:, blk] = (kc[:, blk] + kr).astype(BF16)
    v_ref[...] = _dot(ckvb, wuv_ref[...]).astype(BF16)


def _mla_ctx_call(ckv, kr_blk, w):
    n = ckv.shape[0]
    tm = PAST_LEN
    row = lambda i: (i, 0)
    return pl.pallas_call(
        _mla_ctx_body,
        grid=(n // tm,),
        in_specs=[
            pl.BlockSpec((tm, MLA_KV_RANK), row),
            pl.BlockSpec((tm, LANES), row),
            _full((MLA_KV_RANK, MLA_QK_W)),
            _full((MLA_KV_RANK, MLA_WIDTH)),
        ],
        out_specs=[pl.BlockSpec((tm, MLA_QK_W), row), pl.BlockSpec((tm, MLA_WIDTH), row)],
        out_shape=[jax.ShapeDtypeStruct((n, MLA_QK_W), BF16), jax.ShapeDtypeStruct((n, MLA_WIDTH), BF16)],
        compiler_params=_params(),
        name="mla_ctx_kv",
    )(ckv, kr_blk, w["w_uk"], w["w_uv"])


def _mla_attn_body(*refs, has_ctx):
    if has_ctx:
        q_ref, kc_ref, v_ref, kcc_ref, vc_ref, ga_ref, o_ref, va_ref, vca_ref = refs
        aug = ((v_ref, va_ref), (vc_ref, vca_ref))
    else:
        q_ref, kc_ref, v_ref, ga_ref, o_ref, va_ref = refs
        aug = ((v_ref, va_ref),)

    @pl.when(pl.program_id(1) == 0)
    def _():
        for src, dst in aug:
            for j in range(MLA_HEADS // 2):
                dst[:, 2 * j * LANES:(2 * j + 1) * LANES] = src[:, j * LANES:(j + 1) * LANES]
                dst[:, (2 * j + 1) * LANES:(2 * j + 2) * LANES] = jnp.ones((src.shape[0], LANES), BF16)

    def lane_tile_max(t, s):
        for cb in range(s.shape[1] // LANES):
            blk = s[:, cb * LANES:(cb + 1) * LANES]
            t = blk if t is None else jnp.maximum(t, blk)
        return t

    low = _lane_iota((q_ref.shape[0], LANES)) < HALF
    for j in range(MLA_HEADS // 2):
        pair = slice(j * LANES, (j + 1) * LANES)
        vaug = slice(2 * j * LANES, (2 * j + 2) * LANES)
        outs = []
        for hh in (2 * j, 2 * j + 1):
            blk = slice(hh * MLA_HEAD_PAD, (hh + 1) * MLA_HEAD_PAD)
            qh = q_ref[:, blk]
            s = _dot_nt(qh, kc_ref[:, blk])
            t = lane_tile_max(None, s)
            if has_ctx:
                sc = _dot_nt(qh, kcc_ref[:, blk])
                t = lane_tile_max(t, sc)
            m = jnp.max(t, axis=-1, keepdims=True)
            acc = _dot(jnp.exp2(s - m).astype(BF16), va_ref[:, vaug])
            if has_ctx:
                acc = acc + _dot(jnp.exp2(sc - m).astype(BF16), vca_ref[:, vaug])
            outs.append(acc[:, :LANES] / acc[:, LANES:])
        o_pair = jnp.where(low, outs[0], outs[1])
        o_ref[:, pair] = (o_pair * ga_ref[:, pair].astype(F32)).astype(BF16)


def _mla_attn_call(q, kc, v, ga, nbatch, seq, ctx=None):
    tq = min(seq, MLA_Q_TILE)
    scratch = [pltpu.VMEM((seq, 2 * MLA_WIDTH), BF16)]
    if ctx is not None:
        scratch.append(pltpu.VMEM((PAST_LEN, 2 * MLA_WIDTH), BF16))
    nq = seq // tq
    qrow = lambda b, i: (b * nq + i, 0)
    brow = lambda b, i: (b, 0)
    in_specs = [
        pl.BlockSpec((tq, MLA_QK_W), qrow),
        pl.BlockSpec((seq, MLA_QK_W), brow),
        pl.BlockSpec((seq, MLA_WIDTH), brow),
    ]
    args = [q, kc, v]
    if ctx is not None:
        in_specs += [pl.BlockSpec((PAST_LEN, MLA_QK_W), brow), pl.BlockSpec((PAST_LEN, MLA_WIDTH), brow)]
        args += list(ctx)
    in_specs.append(pl.BlockSpec((tq, MLA_WIDTH), qrow))
    args.append(ga)
    return pl.pallas_call(
        functools.partial(_mla_attn_body, has_ctx=ctx is not None),
        grid=(nbatch, nq),
        in_specs=in_specs,
        out_specs=pl.BlockSpec((tq, MLA_WIDTH), qrow),
        out_shape=jax.ShapeDtypeStruct((nbatch * seq, MLA_WIDTH), BF16),
        scratch_shapes=scratch,
        compiler_params=_params2(),
        name="mla_attn_latent" if ctx is not None else "mla_attn_context",
    )(*args)


def _post_residual(out, x, mod_ref, gpost_ref):
    m = mod_ref[0]
    return x + _rms(out) * (m[2:3, :] * gpost_ref[...])


def _mla_out_compute(x, ins, *, seq, r0):
    a_ref, vp_ref, vprev_ref, vnext_ref, gb_ref, pm_ref, wp_ref, sp_ref, wo_ref, mod_ref, gpost_ref = ins
    tm = vp_ref.shape[0]
    cur = vp_ref[...]
    slab = jnp.concatenate([vprev_ref[...], cur, vnext_ref[...]], axis=0)
    srow = lax.broadcasted_iota(jnp.int32, (POOL_BLOCK + 2 * POOL_HALO, POOL_WIDTH), 0)
    trow = lax.broadcasted_iota(jnp.int32, (POOL_BLOCK, LANES), 0)
    zero = jnp.zeros((POOL_BLOCK + 2 * POOL_HALO, POOL_WIDTH), BF16)
    pooled = [[] for _ in POOL_WINDOWS]
    for sb in range(r0 // POOL_BLOCK, (r0 + a_ref.shape[0]) // POOL_BLOCK):
        pos0 = (pl.program_id(0) * tm + sb * POOL_BLOCK) % seq
        lo_cut = jnp.where(pos0 == 0, POOL_HALO, 0)
        hi_cut = jnp.where(pos0 == seq - POOL_BLOCK, POOL_BLOCK + POOL_HALO, POOL_BLOCK + 2 * POOL_HALO)
        chunk = slab[sb * POOL_BLOCK:(sb + 1) * POOL_BLOCK + 2 * POOL_HALO]
        chunk = jnp.where(srow >= lo_cut, chunk, zero)
        chunk = jnp.where(srow < hi_cut, chunk, zero)
        t = pos0 + trow
        for g, w in enumerate(POOL_WINDOWS):
            cols = slice(g * POOL_GROUP_W, (g + 1) * POOL_GROUP_W)
            tot = _dot(pm_ref[g], chunk[:, cols])
            cnt = jnp.minimum(t + w // 2, seq) - jnp.maximum(t - w // 2, 0)
            own = cur[sb * POOL_BLOCK:(sb + 1) * POOL_BLOCK, cols].astype(F32)
            pooled[g].append((tot / cnt.astype(F32) - own).astype(BF16))
    bs = []
    for g in range(POOL_GROUPS):
        cols = slice(g * POOL_GROUP_W, (g + 1) * POOL_GROUP_W)
        y = _dot(jnp.concatenate(pooled[g], axis=0), wp_ref[g]) * sp_ref[:, cols]
        bs.append((y * gb_ref[:, cols].astype(F32)).astype(BF16))
    b = jnp.concatenate(bs, axis=1)
    out = _dot(a_ref[...], wo_ref[0:MLA_WIDTH, :]) + _dot(b, wo_ref[MLA_WIDTH:, :])
    return _post_residual(out, x, mod_ref, gpost_ref)


def _pool_masks():
    r = np.arange(POOL_BLOCK)[:, None]
    c = np.arange(POOL_BLOCK + 2 * POOL_HALO)[None, :]
    rel = c - POOL_HALO - r
    ms = [((rel >= -(w // 2)) & (rel <= w // 2 - 1)) for w in POOL_WINDOWS]
    return jnp.asarray(np.stack(ms).astype(np.float32), dtype=BF16)


def _mla_out_part(n, a, vp, gb, mod, gpost, w, latent):
    tm = TOK_TILE
    seq = DEC_SEQ if latent else SEQ
    row = lambda i: (i, 0)
    hb = tm // POOL_HALO
    nhb = n // POOL_HALO
    in_specs = [
        pl.BlockSpec((tm, MLA_WIDTH), row),
        pl.BlockSpec((tm, POOL_WIDTH), row),
        pl.BlockSpec((POOL_HALO, POOL_WIDTH), lambda i: (jnp.maximum(i * hb - 1, 0), 0)),
        pl.BlockSpec((POOL_HALO, POOL_WIDTH), lambda i: (jnp.minimum((i + 1) * hb, nhb - 1), 0)),
        pl.BlockSpec((tm, POOL_WIDTH), row),
        _full((POOL_GROUPS, POOL_BLOCK, POOL_BLOCK + 2 * POOL_HALO)),
        _full((POOL_GROUPS, POOL_GROUP_W, POOL_GROUP_W)),
        _full((1, POOL_WIDTH)),
        _full((MLA_WIDTH + POOL_WIDTH, D_MODEL)),
        pl.BlockSpec((1, 3, D_MODEL), _mod_row_fn(latent, tm)),
        _full((1, D_MODEL)),
    ]
    args = [a, vp, vp, vp, gb, _pool_masks(), w["w_pool"], w["s_pool"], w["w_out"], mod, gpost]
    tiled = [True, False, False, False, True] + [False] * 6
    return _Part("mla_out", functools.partial(_mla_out_compute, seq=seq), in_specs, args, tiled, None, None)


def _swa_out_compute(x, ins, *, r0):
    a_ref, wo_ref, mod_ref, gpost_ref = ins
    return _post_residual(_dot(a_ref[...], wo_ref[...]), x, mod_ref, gpost_ref)


def _swa_out_part(a, mod, gpost, w_out, latent):
    tm = TOK_TILE
    in_specs = [
        pl.BlockSpec((tm, SWA_WIDTH), lambda i: (i, 0)),
        _full((SWA_WIDTH, D_MODEL)),
        pl.BlockSpec((1, 3, D_MODEL), _mod_row_fn(latent, tm)),
        _full((1, D_MODEL)),
    ]
    tiled = [True, False, False, False]
    return _Part("swa_out", _swa_out_compute, in_specs, [a, w_out, mod, gpost], tiled, None, None)


_SQ = (0, SWA_WIDTH)
_SK = (_SQ[1], _SQ[1] + SWA_KV_W)
_SV = (_SK[1], _SK[1] + SWA_KV_W)
_SG = (_SV[1], _SV[1] + SWA_WIDTH)
SWA_IN_COLS = _SG[1]
SWA_KV_DUP_W = 2 * SWA_KV_W


def _dup_heads(x, o_ref):
    for b in range(SWA_KV_W // LANES):
        blk = x[:, b * LANES:(b + 1) * LANES]
        rot = pltpu.roll(blk, HALF, 1)
        low = _lane_iota(blk.shape) < HALF
        o_ref[:, (2 * b) * LANES:(2 * b + 1) * LANES] = jnp.where(low, blk, rot).astype(BF16)
        o_ref[:, (2 * b + 1) * LANES:(2 * b + 2) * LANES] = jnp.where(low, rot, blk).astype(BF16)


def _swa_in_compute(x, ins, refs, *, rope, state, r0):
    mod_ref, g_ref, win_ref = ins[:3]
    if rope:
        cos_ref, sin_ref = ins[3:5]
    q_ref, kd_ref, vd_ref, gs_ref = refs[:4]
    h = _modulated(x, mod_ref, g_ref)

    def proj(c):
        return _dot(h, win_ref[:, c[0]:c[1]])

    q = proj(_SQ) * (SWA_SCALE * LOG2E)
    k = proj(_SK)
    v = proj(_SV)
    if state:
        refs[4][...] = k
        refs[5][...] = v
    if rope:
        cos = cos_ref[...]
        sin = sin_ref[...]
        for b in range(SWA_WIDTH // LANES):
            blk = slice(b * LANES, (b + 1) * LANES)
            q_ref[:, blk] = _rope_block(q[:, blk], cos, sin, SWA_HEAD_DIM // 4).astype(BF16)
        k = jnp.concatenate(
            [_rope_block(k[:, b * LANES:(b + 1) * LANES], cos, sin, SWA_HEAD_DIM // 4)
             for b in range(SWA_KV_W // LANES)], axis=1)
    else:
        q_ref[...] = q.astype(BF16)
    _dup_heads(k, kd_ref)
    _dup_heads(v, vd_ref)
    gs_ref[...] = _silu(proj(_SG)).astype(BF16)


def _swa_in_part(n, mod, gpre, w_in, latent, rope_tabs):
    tm = TOK_TILE
    row = lambda i: (i, 0)
    in_specs = [
        pl.BlockSpec((1, 3, D_MODEL), _mod_row_fn(latent, tm)),
        _full((1, D_MODEL)),
        _full((D_MODEL, SWA_IN_COLS)),
    ]
    args = [mod, gpre, w_in]
    if latent:
        per_seq = DEC_SEQ // tm
        in_specs += [pl.BlockSpec((tm, LANES), lambda i: (i % per_seq, 0))] * 2
        args += list(rope_tabs)
    widths = [SWA_WIDTH, SWA_KV_DUP_W, SWA_KV_DUP_W, SWA_WIDTH]
    out_shape = [jax.ShapeDtypeStruct((n, c), BF16) for c in widths]
    out_specs = [pl.BlockSpec((tm, c), row) for c in widths]
    if not latent:
        out_shape += [jax.ShapeDtypeStruct((n, SWA_KV_W), F32)] * 2
        out_specs += [pl.BlockSpec((tm, SWA_KV_W), row)] * 2
    compute = functools.partial(_swa_in_compute, rope=latent, state=not latent)
    tiled = [False] * 3 + [True] * (len(in_specs) - 3)
    return _Part("swa_in", compute, in_specs, args, tiled, out_specs, out_shape)


SWA_GROUP = SWA_HEADS // SWA_KV_HEADS


def _swa_group_attend(qa, qb, parts, sinks):
    rows = qa.shape[0]
    low = _lane_iota(qa.shape) < HALF
    zero = jnp.zeros_like(qa)
    qst = jnp.concatenate(
        [jnp.where(low, qa, zero), jnp.where(low, zero, qa), jnp.where(low, qb, zero), jnp.where(low, zero, qb)],
        axis=0)
    sink_t = jnp.concatenate([jnp.full((rows, LANES), s * LOG2E, F32) for s in sinks], axis=0)
    scores = []
    t = sink_t
    for k, _, bias in parts:
        s = _dot_nt(qst, k)
        if bias is not None:
            s = s + bias
        scores.append(s)
        for cb in range(s.shape[1] // LANES):
            t = jnp.maximum(t, s[:, cb * LANES:(cb + 1) * LANES])
    m = jnp.max(t, axis=-1, keepdims=True)
    acc = None
    for s, (_, v, _) in zip(scores, parts):
        p = jnp.exp2(s - m).astype(BF16)
        pv = _dot(p, jnp.concatenate([v, jnp.ones_like(v)], axis=1))
        acc = pv if acc is None else acc + pv
    o = acc[:, :LANES] / (acc[:, LANES:] + jnp.exp2(sink_t - m))
    return (jnp.where(low, o[0:rows], o[rows:2 * rows]),
            jnp.where(low, o[2 * rows:3 * rows], o[3 * rows:4 * rows]))


def _swa_dense_body(sink_ref, q_ref, kd_ref, vd_ref, gs_ref, o_ref):
    for kh in range(SWA_KV_HEADS):
        kv = slice(kh * LANES, (kh + 1) * LANES)
        ta = slice(2 * kh * LANES, (2 * kh + 1) * LANES)
        tb = slice((2 * kh + 1) * LANES, (2 * kh + 2) * LANES)
        sinks = [sink_ref[SWA_GROUP * kh + e] for e in range(SWA_GROUP)]
        oa, ob = _swa_group_attend(q_ref[:, ta], q_ref[:, tb], [(kd_ref[:, kv], vd_ref[:, kv], None)], sinks)
        o_ref[:, ta] = (oa * gs_ref[:, ta].astype(F32)).astype(BF16)
        o_ref[:, tb] = (ob * gs_ref[:, tb].astype(F32)).astype(BF16)


def _swa_dense_call(sink, q, kd, vd, gs):
    n = q.shape[0]
    row = lambda b: (b, 0)
    return pl.pallas_call(
        _swa_dense_body,
        grid=(n // SEQ,),
        in_specs=[
            pl.BlockSpec(memory_space=pltpu.SMEM),
            pl.BlockSpec((SEQ, SWA_WIDTH), row),
            pl.BlockSpec((SEQ, SWA_KV_DUP_W), row),
            pl.BlockSpec((SEQ, SWA_KV_DUP_W), row),
            pl.BlockSpec((SEQ, SWA_WIDTH), row),
        ],
        out_specs=pl.BlockSpec((SEQ, SWA_WIDTH), row),
        out_shape=jax.ShapeDtypeStruct((n, SWA_WIDTH), BF16),
        compiler_params=_params(),
        name="swa_attn_context",
    )(sink, q, kd, vd, gs)


SWA_Q_TILE = 512
SWA_SPAN = 3 * SWA_WINDOW


def _swa_band_body(sink_ref, q_ref, kd_ref, vd_ref, kx_ref, vx_ref, gs_ref, o_ref):
    w = SWA_WINDOW
    nsub = q_ref.shape[0] // w
    step = pl.program_id(1)
    r = lax.broadcasted_iota(jnp.int32, (w, SWA_SPAN), 0)
    c = lax.broadcasted_iota(jnp.int32, (w, SWA_SPAN), 1)

    def block(sb, carry):
        qbase = (step * nsub + sb) * w
        start = pl.multiple_of(jnp.clip(qbase - w, 0, DEC_SEQ - SWA_SPAN), w)
        rel = c - r + (start - qbase)
        bias = jnp.where(jnp.abs(rel) <= w, 0.0, NEG_INF).astype(F32)
        bias = jnp.concatenate([bias] * SWA_GROUP, axis=0)
        rows = pl.ds(pl.multiple_of(sb * w, w), w)
        span = pl.ds(start, SWA_SPAN)
        for kh in range(SWA_KV_HEADS):
            kv = slice(kh * LANES, (kh + 1) * LANES)
            ta = slice(2 * kh * LANES, (2 * kh + 1) * LANES)
            tb = slice((2 * kh + 1) * LANES, (2 * kh + 2) * LANES)
            sinks = [sink_ref[SWA_GROUP * kh + e] for e in range(SWA_GROUP)]
            parts = [(kd_ref[span, kv], vd_ref[span, kv], bias), (kx_ref[:, kv], vx_ref[:, kv], None)]
            oa, ob = _swa_group_attend(q_ref[rows, ta], q_ref[rows, tb], parts, sinks)
            o_ref[rows, ta] = (oa * gs_ref[rows, ta].astype(F32)).astype(BF16)
            o_ref[rows, tb] = (ob * gs_ref[rows, tb].astype(F32)).astype(BF16)
        return carry

    lax.fori_loop(0, nsub, block, 0, unroll=True)


def _swa_band_call(sink, q, kd, vd, kx, vx, gs):
    tq = SWA_Q_TILE
    nq = DEC_SEQ // tq
    qrow = lambda b, i: (b * nq + i, 0)
    brow = lambda b, i: (b, 0)
    return pl.pallas_call(
        _swa_band_body,
        grid=(DEC_BATCH, nq),
        in_specs=[
            pl.BlockSpec(memory_space=pltpu.SMEM),
            pl.BlockSpec((tq, SWA_WIDTH), qrow),
            pl.BlockSpec((DEC_SEQ, SWA_KV_DUP_W), brow),
            pl.BlockSpec((DEC_SEQ, SWA_KV_DUP_W), brow),
            pl.BlockSpec((PAST_LEN, SWA_KV_DUP_W), brow),
            pl.BlockSpec((PAST_LEN, SWA_KV_DUP_W), brow),
            pl.BlockSpec((tq, SWA_WIDTH), qrow),
        ],
        out_specs=pl.BlockSpec((tq, SWA_WIDTH), qrow),
        out_shape=jax.ShapeDtypeStruct((DEC_BATCH * DEC_SEQ, SWA_WIDTH), BF16),
        compiler_params=_params2(),
        name="swa_attn_latent",
    )(sink, q, kd, vd, kx, vx, gs)


def _rope_tables(rope_dims, lane_period, lane_off):
    n = rope_dims // 2
    hf = n // 2
    t = jnp.arange(DEC_SEQ)
    pos = jnp.stack([(t // GRID_W).astype(F32), (t % GRID_W).astype(F32)], axis=0)
    inv = ROPE_BASE ** (-jnp.arange(0, n, 2, dtype=F32) / n)
    ang = pos[:, :, None] * inv[None, None, :]
    cos = jnp.cos(ang)
    sin = jnp.sin(ang)
    cos_d = jnp.concatenate([cos, cos], axis=-1)
    sin_d = jnp.concatenate([-sin, sin], axis=-1)
    cos_r = jnp.concatenate([cos_d[0], cos_d[1]], axis=-1)
    sin_r = jnp.concatenate([sin_d[0], sin_d[1]], axis=-1)
    reps = LANES // lane_period
    pad = ((0, 0), (lane_off, lane_period - lane_off - rope_dims))
    cos_p = jnp.tile(jnp.pad(cos_r, pad, constant_values=1.0), (1, reps))
    sin_p = jnp.tile(jnp.pad(sin_r, pad), (1, reps))
    return cos_p, sin_p


def _prep_mla(i, mla_w_in, mla_g_qn, mla_g_kvn, mla_w_uq, mla_w_ukv, pool_w, pool_scale, mixa_w_out):
    w_in = mla_w_in[i]
    i1 = MLA_Q_RANK
    i2 = i1 + MLA_KV_RANK
    i3 = i2 + MLA_ROPE
    kr = jnp.pad(w_in[:, i2:i3], ((0, 0), (MLA_NOPE, LANES - MLA_NOPE - MLA_ROPE)))
    w_in_r = jnp.concatenate([w_in[:, :i2], kr, w_in[:, i3:]], axis=1).astype(BF16)
    dq = MLA_NOPE + MLA_ROPE
    w_uq = jnp.pad(mla_w_uq[i].reshape(MLA_Q_RANK, MLA_HEADS, dq),
                   ((0, 0), (0, 0), (0, MLA_HEAD_PAD - dq))).reshape(MLA_Q_RANK, MLA_QK_W).astype(BF16)
    ukv = mla_w_ukv[i].reshape(MLA_KV_RANK, MLA_HEADS, MLA_NOPE + MLA_V)
    w_uk = jnp.pad(ukv[:, :, :MLA_NOPE], ((0, 0), (0, 0), (0, MLA_HEAD_PAD - MLA_NOPE)))
    w_uk = w_uk.reshape(MLA_KV_RANK, MLA_QK_W).astype(BF16)
    w_uv = ukv[:, :, MLA_NOPE:].reshape(MLA_KV_RANK, MLA_WIDTH).astype(BF16)
    return {
        "w_in": w_in_r, "w_uq": w_uq, "w_uk": w_uk, "w_uv": w_uv,
        "g_qn": mla_g_qn[i].reshape(1, -1), "g_kvn": mla_g_kvn[i].reshape(1, -1),
        "w_pool": pool_w[i].astype(BF16), "s_pool": pool_scale[i].reshape(1, -1),
        "w_out": mixa_w_out[i].astype(BF16),
    }


def kernel(x_prompt, x_sample, cache_ckv, cache_krope, cache_k, cache_v, c, c_ctx, ada_w, ada_b, norm_pre, norm_post, mla_w_in, mla_g_qn, mla_g_kvn, mla_w_uq, mla_w_ukv, pool_w, pool_scale, mixa_w_out, swa_w_in, swa_sink, swa_w_out):
    n_p = BATCH * SEQ
    n_s = DEC_BATCH * DEC_SEQ
    xp = x_prompt.reshape(n_p, D_MODEL)
    xs = x_sample.reshape(n_s, D_MODEL)
    cvec = jnp.concatenate(
        [c_ctx[None], c, jnp.zeros((MOD_ROWS - 1 - DEC_BATCH, D_MODEL), F32)], axis=0)
    mods = _ada_call(cvec, ada_w, ada_b).reshape(DEPTH, MOD_ROWS, 3, D_MODEL)
    mla_tabs = _rope_tables(MLA_ROPE, LANES, MLA_NOPE)
    swa_tabs = _rope_tables(SWA_HEAD_DIM, SWA_HEAD_DIM, 0)
    st_ckv, st_krope, st_k, st_v = [], [], [], []
    pend_p = pend_s = None

    def stage(x, latent, out_part, in_part):
        res = list(_stage_call(x, latent, out_part, in_part))
        if out_part is not None:
            return res[0], res[1:]
        return x, res

    for l in range(DEPTH):
        i = l // 2
        mod = mods[l]
        gpre = norm_pre[l].reshape(1, D_MODEL)
        gpost = norm_post[l].reshape(1, D_MODEL)
        if l % 2 == 0:
            w = _prep_mla(i, mla_w_in, mla_g_qn, mla_g_kvn, mla_w_uq, mla_w_ukv, pool_w, pool_scale,
                          mixa_w_out)
            xp, (qp, kcp, vp, gap, vpp, gbp, ckv, krb) = stage(
                xp, False, pend_p, _mla_in_part(n_p, mod, gpre, w, False, None))
            xs, (qs, kcs, vs, gas, vps, gbs) = stage(
                xs, True, pend_s, _mla_in_part(n_s, mod, gpre, w, True, mla_tabs))
            ctx_kr = jnp.pad(cache_krope[:, i].reshape(DEC_BATCH * PAST_LEN, MLA_ROPE),
                             ((0, 0), (MLA_NOPE, LANES - MLA_NOPE - MLA_ROPE)))
            ctx = _mla_ctx_call(cache_ckv[:, i].reshape(DEC_BATCH * PAST_LEN, MLA_KV_RANK), ctx_kr, w)
            ap = _mla_attn_call(qp, kcp, vp, gap, BATCH, SEQ)
            as_ = _mla_attn_call(qs, kcs, vs, gas, DEC_BATCH, DEC_SEQ, ctx)
            pend_p = _mla_out_part(n_p, ap, vpp, gbp, mod, gpost, w, False)
            pend_s = _mla_out_part(n_s, as_, vps, gbs, mod, gpost, w, True)
            st_ckv.append(ckv.reshape(BATCH, SEQ, MLA_KV_RANK))
            st_krope.append(krb[:, MLA_NOPE:MLA_NOPE + MLA_ROPE].reshape(BATCH, SEQ, MLA_ROPE))
        else:
            w_in = swa_w_in[i].astype(BF16)
            w_out = swa_w_out[i].astype(BF16)
            sink = swa_sink[i]
            xp, (qp, kdp, vdp, gsp, kst, vst) = stage(
                xp, False, pend_p, _swa_in_part(n_p, mod, gpre, w_in, False, None))
            xs, (qs, kds, vds, gss) = stage(
                xs, True, pend_s, _swa_in_part(n_s, mod, gpre, w_in, True, swa_tabs))

            def dup(t):
                t = t[:, i].reshape(DEC_BATCH * PAST_LEN, SWA_KV_HEADS, SWA_HEAD_DIM)
                return jnp.concatenate([t, t], axis=-1).reshape(DEC_BATCH * PAST_LEN, SWA_KV_DUP_W).astype(BF16)

            ap = _swa_dense_call(sink, qp, kdp, vdp, gsp)
            as_ = _swa_band_call(sink, qs, kds, vds, dup(cache_k), dup(cache_v), gss)
            pend_p = _swa_out_part(ap, mod, gpost, w_out, False)
            pend_s = _swa_out_part(as_, mod, gpost, w_out, True)
            st_k.append(kst.reshape(BATCH, SEQ, SWA_KV_HEADS, SWA_HEAD_DIM))
            st_v.append(vst.reshape(BATCH, SEQ, SWA_KV_HEADS, SWA_HEAD_DIM))
    xp, _ = stage(xp, False, pend_p, None)
    xs, _ = stage(xs, True, pend_s, None)
    return (
        xp.reshape(BATCH, SEQ, D_MODEL),
        xs.reshape(DEC_BATCH, DEC_SEQ, D_MODEL),
        jnp.stack(st_ckv, axis=1),
        jnp.stack(st_krope, axis=1),
        jnp.stack(st_k, axis=1),
        jnp.stack(st_v, axis=1),
    )
```

```python
import functools

import numpy as np
import jax
import jax.numpy as jnp
from jax import lax
from jax.experimental import pallas as pl
from jax.experimental.pallas import tpu as pltpu

F32 = jnp.float32
BF16 = jnp.bfloat16

D_MODEL = 1024
BATCH = 32
SEQ = 256
DEPTH = 4
DEC_BATCH = 8
DEC_SEQ = 2048
PAST_LEN = 256
GRID_W = 64
ROPE_BASE = 10000.0
EPS = 1e-6
NEG_INF = -1e30

MLA_HEADS = 8
MLA_NOPE = 64
MLA_ROPE = 32
MLA_V = 64
MLA_Q_RANK = 384
MLA_KV_RANK = 256
MLA_WIDTH = MLA_HEADS * MLA_V
MLA_SCALE = (MLA_NOPE + MLA_ROPE) ** -0.5
POOL_WINDOWS = (2, 4, 8, 16)
POOL_GROUPS = 4
POOL_GROUP_W = D_MODEL // 8
POOL_WIDTH = POOL_GROUPS * POOL_GROUP_W
SWA_HEADS = 16
SWA_KV_HEADS = 4
SWA_HEAD_DIM = 64
SWA_WINDOW = 128
SWA_WIDTH = SWA_HEADS * SWA_HEAD_DIM
SWA_KV_W = SWA_KV_HEADS * SWA_HEAD_DIM
SWA_SCALE = SWA_HEAD_DIM ** -0.5

LANES = 128
HALF = LANES // 2
BF16_SUBLANES = 16
MOD_ROWS = 16
TOK_TILE = 1024
LOG2E = 1.4426950408889634
POOL_BLOCK = 128
POOL_HALO = 64
VMEM_LIMIT = 60 * 1024 * 1024

_CQ = (0, MLA_Q_RANK)
_CKV = (_CQ[1], _CQ[1] + MLA_KV_RANK)
_KR = (_CKV[1], _CKV[1] + LANES)
_GA = (_KR[1], _KR[1] + MLA_WIDTH)
_VP = (_GA[1], _GA[1] + POOL_WIDTH)
_GB = (_VP[1], _VP[1] + POOL_WIDTH)
MLA_IN_COLS = _GB[1]
MLA_HEAD_PAD = LANES
MLA_QK_W = MLA_HEADS * MLA_HEAD_PAD
MLA_Q_TILE = 1024


def _params():
    return pltpu.CompilerParams(dimension_semantics=("arbitrary",), vmem_limit_bytes=VMEM_LIMIT)


def _params2():
    return pltpu.CompilerParams(dimension_semantics=("arbitrary", "arbitrary"), vmem_limit_bytes=VMEM_LIMIT)


def _dot(a, b):
    return jnp.dot(a, b, preferred_element_type=F32)


def _dot_nt(a, b):
    return lax.dot_general(a, b, (((1,), (1,)), ((), ())), preferred_element_type=F32)


def _silu(x):
    return x / (1.0 + jnp.exp(-x))


def _rms(x):
    return x * lax.rsqrt(jnp.mean(x * x, axis=-1, keepdims=True) + EPS)


def _lane_iota(shape):
    return lax.broadcasted_iota(jnp.int32, shape, len(shape) - 1)


def _full(shape):
    nd = len(shape)
    return pl.BlockSpec(shape, lambda *_: (0,) * nd, pipeline_mode=pl.Buffered(1))


def _ada_body(c_ref, w_ref, b_ref, o_ref):
    s = _silu(c_ref[...]).astype(BF16)
    o_ref[0] = _dot(s, w_ref[0].astype(BF16)) + b_ref[0]


def _ada_call(cvec, ada_w, ada_b):
    tn = 768
    return pl.pallas_call(
        _ada_body,
        grid=(DEPTH, 3 * D_MODEL // tn),
        in_specs=[
            pl.BlockSpec((MOD_ROWS, D_MODEL), lambda l, j: (0, 0)),
            pl.BlockSpec((1, D_MODEL, tn), lambda l, j: (l, 0, j)),
            pl.BlockSpec((1, 1, tn), lambda l, j: (l, 0, j)),
        ],
        out_specs=pl.BlockSpec((1, MOD_ROWS, tn), lambda l, j: (l, 0, j)),
        out_shape=jax.ShapeDtypeStruct((DEPTH, MOD_ROWS, 3 * D_MODEL), F32),
        compiler_params=_params2(),
        name="ada_mod",
    )(cvec, ada_w, ada_b.reshape(DEPTH, 1, 3 * D_MODEL))


def _modulated(x, mod_ref, g_ref):
    m = mod_ref[0]
    gain = g_ref[...] * (1.0 + m[1:2, :])
    return (_rms(x) * gain + m[0:1, :]).astype(BF16)


def _rope_block(b, cos, sin, partner):
    first = (_lane_iota(b.shape) & partner) == 0
    sw = jnp.where(first, pltpu.roll(b, LANES - partner, 1), pltpu.roll(b, partner, 1))
    return b * cos + sw * sin


def _emit_states(slot, values, prev_refs, out_refs):
    for idx, (val, out_ref) in enumerate(zip(values, out_refs)):
        if slot == 0:
            out_ref[...] = val
        else:
            prev_ref = prev_refs[idx]
            nb = val.shape[0] // SEQ
            out_ref[:, 0:SEQ, :] = prev_ref[...].reshape(nb, SEQ, val.shape[1])
            out_ref[:, SEQ:2 * SEQ, :] = val.reshape(nb, SEQ, val.shape[1])


def _state_specs(n, widths, slot, prev):
    tm = TOK_TILE
    if slot == 0:
        specs = [pl.BlockSpec((tm, c), lambda i: (i, 0)) for c in widths]
        return [], [], specs, [jax.ShapeDtypeStruct((n, c), F32) for c in widths]
    nb = tm // SEQ
    ins = [pl.BlockSpec((tm, c), lambda i: (i, 0)) for c in widths]
    outs = [pl.BlockSpec((nb, 2 * SEQ, c), lambda i: (i, 0, 0)) for c in widths]
    return ins, list(prev), outs, [jax.ShapeDtypeStruct((n // SEQ, 2 * SEQ, c), F32) for c in widths]


def _mod_row_fn(latent, tile):
    if latent:
        per_seq = DEC_SEQ // tile
        return lambda i: (1 + i // per_seq, 0, 0)
    return lambda i: (0, 0, 0)


def _mla_in_compute(x, ins, refs, *, rope, state):
    mod_ref, g_ref, win_ref, gq_ref, gkv_ref, wuq_ref, wuk_ref, wuv_ref = ins[:8]
    if rope:
        cos_ref, sin_ref = ins[8:10]
    q_ref, kc_ref, v_ref, ga_ref, vp_ref, gb_ref = refs[:6]
    h = _modulated(x, mod_ref, g_ref)

    def proj(c):
        return _dot(h, win_ref[:, c[0]:c[1]])

    cqn = (_rms(proj(_CQ)) * gq_ref[...]).astype(BF16)
    q = _dot(cqn, wuq_ref[...]) * (MLA_SCALE * LOG2E)
    ckvn = _rms(proj(_CKV)) * gkv_ref[...]
    ckvb = ckvn.astype(BF16)
    kr = proj(_KR)
    if state is not None:
        kr_state = pltpu.roll(kr, LANES - MLA_NOPE, 1)[:, :MLA_ROPE]
        _emit_states(state, (ckvn, kr_state), ins[8:10], refs[6:8])
    if rope:
        cos = cos_ref[...]
        sin = sin_ref[...]
        kr = _rope_block(kr, cos, sin, MLA_ROPE // 4)
    kc = _dot(ckvb, wuk_ref[...])
    for hh in range(MLA_HEADS):
        blk = slice(hh * MLA_HEAD_PAD, (hh + 1) * MLA_HEAD_PAD)
        qb = q[:, blk]
        if rope:
            qb = _rope_block(qb, cos, sin, MLA_ROPE // 4)
        q_ref[:, blk] = qb.astype(BF16)
        kc_ref[:, blk] = (kc[:, blk] + kr).astype(BF16)
    v_ref[...] = _dot(ckvb, wuv_ref[...]).astype(BF16)
    ga_ref[...] = _silu(proj(_GA)).astype(BF16)
    vp_ref[...] = proj(_VP).astype(BF16)
    gb_ref[...] = _silu(proj(_GB)).astype(BF16)


class _Part:
    def __init__(self, name, compute, in_specs, args, out_specs, out_shape):
        self.name = name
        self.compute = compute
        self.in_specs = in_specs
        self.args = args
        self.out_specs = out_specs
        self.out_shape = out_shape


def _mla_in_part(n, mod, gpre, w, latent, rope_tabs, slot=None, prev=()):
    tm = TOK_TILE
    row = lambda i: (i, 0)
    in_specs = [
        pl.BlockSpec((1, 3, D_MODEL), _mod_row_fn(latent, tm)),
        _full((1, D_MODEL)),
        _full((D_MODEL, MLA_IN_COLS)),
        _full((1, MLA_Q_RANK)),
        _full((1, MLA_KV_RANK)),
        _full((MLA_Q_RANK, MLA_QK_W)),
        _full((MLA_KV_RANK, MLA_QK_W)),
        _full((MLA_KV_RANK, MLA_WIDTH)),
    ]
    args = [mod, gpre, w["w_in"], w["g_qn"], w["g_kvn"], w["w_uq"], w["w_uk"], w["w_uv"]]
    if latent:
        per_seq = DEC_SEQ // tm
        in_specs += [pl.BlockSpec((tm, LANES), lambda i: (i % per_seq, 0))] * 2
        args += list(rope_tabs)
    widths = [MLA_QK_W, MLA_QK_W, MLA_WIDTH, MLA_WIDTH, POOL_WIDTH, POOL_WIDTH]
    out_shape = [jax.ShapeDtypeStruct((n, c), BF16) for c in widths]
    out_specs = [pl.BlockSpec((tm, c), row) for c in widths]
    if not latent:
        s_in, s_args, s_out, s_shape = _state_specs(n, (MLA_KV_RANK, MLA_ROPE), slot, prev)
        in_specs += s_in
        args += s_args
        out_specs += s_out
        out_shape += s_shape
    compute = functools.partial(_mla_in_compute, rope=latent, state=slot)
    return _Part("mla_in", compute, in_specs, args, out_specs, out_shape)


def _stage_body(*refs, out_part, in_part):
    x_ref = refs[0]
    n_out_in = len(out_part.in_specs) if out_part else 0
    n_in_in = len(in_part.in_specs) if in_part else 0
    out_ins = refs[1:1 + n_out_in]
    in_ins = refs[1 + n_out_in:1 + n_out_in + n_in_in]
    outs = refs[1 + n_out_in + n_in_in:]
    x = x_ref[...]
    if out_part:
        x = out_part.compute(x, out_ins)
        outs[0][...] = x
        outs = outs[1:]
    if in_part:
        in_part.compute(x, in_ins, outs)


def _stage_call(x, latent, out_part, in_part):
    n = x.shape[0]
    tm = TOK_TILE
    row = lambda i: (i, 0)
    in_specs = [pl.BlockSpec((tm, D_MODEL), row)]
    args = [x]
    out_specs, out_shape = [], []
    names = []
    if out_part:
        in_specs += out_part.in_specs
        args += out_part.args
        out_specs.append(pl.BlockSpec((tm, D_MODEL), row))
        out_shape.append(jax.ShapeDtypeStruct((n, D_MODEL), F32))
        names.append(out_part.name)
    if in_part:
        in_specs += in_part.in_specs
        args += in_part.args
        out_specs += in_part.out_specs
        out_shape += in_part.out_shape
        names.append(in_part.name)
    names.append("latent" if latent else "context")
    return pl.pallas_call(
        functools.partial(_stage_body, out_part=out_part, in_part=in_part),
        grid=(n // tm,),
        in_specs=in_specs,
        out_specs=out_specs,
        out_shape=out_shape,
        compiler_params=_params(),
        name="_".join(names),
    )(*args)


def _mla_attn_body(*refs, has_ctx):
    if has_ctx:
        q_ref, kc_ref, v_ref, ckv_ref, kr_ref, wuk_ref, wuv_ref, ga_ref, o_ref, va_ref, kcc_ref, vca_ref = refs
    else:
        q_ref, kc_ref, v_ref, ga_ref, o_ref, va_ref = refs

    def augment(v, dst):
        for j in range(MLA_HEADS // 2):
            dst[:, 2 * j * LANES:(2 * j + 1) * LANES] = v[:, j * LANES:(j + 1) * LANES]
            dst[:, (2 * j + 1) * LANES:(2 * j + 2) * LANES] = jnp.ones((v.shape[0], LANES), BF16)

    @pl.when(pl.program_id(1) == 0)
    def _():
        augment(v_ref[...], va_ref)
        if has_ctx:
            ckvb = ckv_ref[0, 0].astype(BF16)
            kr = kr_ref[...]
            kc = _dot(ckvb, wuk_ref[...])
            for hh in range(MLA_HEADS):
                blk = slice(hh * MLA_HEAD_PAD, (hh + 1) * MLA_HEAD_PAD)
                kcc_ref[:, blk] = (kc[:, blk] + kr).astype(BF16)
            augment(_dot(ckvb, wuv_ref[...]).astype(BF16), vca_ref)

    def lane_tile_max(t, s):
        for cb in range(s.shape[1] // LANES):
            blk = s[:, cb * LANES:(cb + 1) * LANES]
            t = blk if t is None else jnp.maximum(t, blk)
        return t

    low = _lane_iota((q_ref.shape[0], LANES)) < HALF
    for j in range(MLA_HEADS // 2):
        pair = slice(j * LANES, (j + 1) * LANES)
        vaug = slice(2 * j * LANES, (2 * j + 2) * LANES)
        outs = []
        for hh in (2 * j, 2 * j + 1):
            blk = slice(hh * MLA_HEAD_PAD, (hh + 1) * MLA_HEAD_PAD)
            qh = q_ref[:, blk]
            s = _dot_nt(qh, kc_ref[:, blk])
            t = lane_tile_max(None, s)
            if has_ctx:
                sc = _dot_nt(qh, kcc_ref[:, blk])
                t = lane_tile_max(t, sc)
            m = jnp.max(t, axis=-1, keepdims=True)
            acc = _dot(jnp.exp2(s - m).astype(BF16), va_ref[:, vaug])
            if has_ctx:
                acc = acc + _dot(jnp.exp2(sc - m).astype(BF16), vca_ref[:, vaug])
            outs.append(acc[:, :LANES] / acc[:, LANES:])
        o_pair = jnp.where(low, outs[0], outs[1])
        o_ref[:, pair] = (o_pair * ga_ref[:, pair].astype(F32)).astype(BF16)


def _mla_attn_call(q, kc, v, ga, nbatch, seq, ctx=None):
    tq = min(seq, MLA_Q_TILE)
    scratch = [pltpu.VMEM((seq, 2 * MLA_WIDTH), BF16)]
    if ctx is not None:
        scratch += [pltpu.VMEM((PAST_LEN, MLA_QK_W), BF16), pltpu.VMEM((PAST_LEN, 2 * MLA_WIDTH), BF16)]
    nq = seq // tq
    qrow = lambda b, i: (b * nq + i, 0)
    brow = lambda b, i: (b, 0)
    in_specs = [
        pl.BlockSpec((tq, MLA_QK_W), qrow),
        pl.BlockSpec((seq, MLA_QK_W), brow),
        pl.BlockSpec((seq, MLA_WIDTH), brow),
    ]
    args = [q, kc, v]
    if ctx is not None:
        cache, slot, ctx_kr, w_uk, w_uv = ctx
        in_specs += [
            pl.BlockSpec((1, 1, PAST_LEN, MLA_KV_RANK), lambda b, i: (b, slot, 0, 0)),
            pl.BlockSpec((PAST_LEN, LANES), brow),
            _full((MLA_KV_RANK, MLA_QK_W)),
            _full((MLA_KV_RANK, MLA_WIDTH)),
        ]
        args += [cache, ctx_kr, w_uk, w_uv]
    in_specs.append(pl.BlockSpec((tq, MLA_WIDTH), qrow))
    args.append(ga)
    return pl.pallas_call(
        functools.partial(_mla_attn_body, has_ctx=ctx is not None),
        grid=(nbatch, nq),
        in_specs=in_specs,
        out_specs=pl.BlockSpec((tq, MLA_WIDTH), qrow),
        out_shape=jax.ShapeDtypeStruct((nbatch * seq, MLA_WIDTH), BF16),
        scratch_shapes=scratch,
        compiler_params=_params2(),
        name="mla_attn_latent" if ctx is not None else "mla_attn_context",
    )(*args)


def _post_residual(out, x, mod_ref, gpost_ref):
    m = mod_ref[0]
    return x + _rms(out) * (m[2:3, :] * gpost_ref[...])


def _mla_out_compute(x, ins, *, seq):
    a_ref, vp_ref, vprev_ref, vnext_ref, gb_ref, pm_ref, wp_ref, sp_ref, wo_ref, mod_ref, gpost_ref = ins
    tm = vp_ref.shape[0]
    cur = vp_ref[...]
    slab = jnp.concatenate([vprev_ref[...], cur, vnext_ref[...]], axis=0)
    srow = lax.broadcasted_iota(jnp.int32, (POOL_BLOCK + 2 * POOL_HALO, POOL_WIDTH), 0)
    trow = lax.broadcasted_iota(jnp.int32, (POOL_BLOCK, LANES), 0)
    zero = jnp.zeros((POOL_BLOCK + 2 * POOL_HALO, POOL_WIDTH), BF16)
    pooled = [[] for _ in POOL_WINDOWS]
    for sb in range(tm // POOL_BLOCK):
        pos0 = (pl.program_id(0) * tm + sb * POOL_BLOCK) % seq
        lo_cut = jnp.where(pos0 == 0, POOL_HALO, 0)
        hi_cut = jnp.where(pos0 == seq - POOL_BLOCK, POOL_BLOCK + POOL_HALO, POOL_BLOCK + 2 * POOL_HALO)
        chunk = slab[sb * POOL_BLOCK:(sb + 1) * POOL_BLOCK + 2 * POOL_HALO]
        chunk = jnp.where(srow >= lo_cut, chunk, zero)
        chunk = jnp.where(srow < hi_cut, chunk, zero)
        t = pos0 + trow
        for g, w in enumerate(POOL_WINDOWS):
            cols = slice(g * POOL_GROUP_W, (g + 1) * POOL_GROUP_W)
            tot = _dot(pm_ref[g], chunk[:, cols])
            cnt = jnp.minimum(t + w // 2, seq) - jnp.maximum(t - w // 2, 0)
            own = cur[sb * POOL_BLOCK:(sb + 1) * POOL_BLOCK, cols].astype(F32)
            pooled[g].append((tot / cnt.astype(F32) - own).astype(BF16))
    bs = []
    for g in range(POOL_GROUPS):
        cols = slice(g * POOL_GROUP_W, (g + 1) * POOL_GROUP_W)
        y = _dot(jnp.concatenate(pooled[g], axis=0), wp_ref[g]) * sp_ref[:, cols]
        bs.append((y * gb_ref[:, cols].astype(F32)).astype(BF16))
    b = jnp.concatenate(bs, axis=1)
    out = _dot(a_ref[...], wo_ref[0:MLA_WIDTH, :]) + _dot(b, wo_ref[MLA_WIDTH:, :])
    return _post_residual(out, x, mod_ref, gpost_ref)


def _pool_masks():
    r = np.arange(POOL_BLOCK)[:, None]
    c = np.arange(POOL_BLOCK + 2 * POOL_HALO)[None, :]
    rel = c - POOL_HALO - r
    ms = [((rel >= -(w // 2)) & (rel <= w // 2 - 1)) for w in POOL_WINDOWS]
    return jnp.asarray(np.stack(ms).astype(np.float32), dtype=BF16)


def _mla_out_part(n, a, vp, gb, mod, gpost, w, latent):
    tm = TOK_TILE
    seq = DEC_SEQ if latent else SEQ
    row = lambda i: (i, 0)
    hb = tm // POOL_HALO
    nhb = n // POOL_HALO
    in_specs = [
        pl.BlockSpec((tm, MLA_WIDTH), row),
        pl.BlockSpec((tm, POOL_WIDTH), row),
        pl.BlockSpec((POOL_HALO, POOL_WIDTH), lambda i: (jnp.maximum(i * hb - 1, 0), 0)),
        pl.BlockSpec((POOL_HALO, POOL_WIDTH), lambda i: (jnp.minimum((i + 1) * hb, nhb - 1), 0)),
        pl.BlockSpec((tm, POOL_WIDTH), row),
        _full((POOL_GROUPS, POOL_BLOCK, POOL_BLOCK + 2 * POOL_HALO)),
        _full((POOL_GROUPS, POOL_GROUP_W, POOL_GROUP_W)),
        _full((1, POOL_WIDTH)),
        _full((MLA_WIDTH + POOL_WIDTH, D_MODEL)),
        pl.BlockSpec((1, 3, D_MODEL), _mod_row_fn(latent, tm)),
        _full((1, D_MODEL)),
    ]
    args = [a, vp, vp, vp, gb, _pool_masks(), w["w_pool"], w["s_pool"], w["w_out"], mod, gpost]
    return _Part("mla_out", functools.partial(_mla_out_compute, seq=seq), in_specs, args, None, None)


def _swa_out_compute(x, ins):
    a_ref, wo_ref, mod_ref, gpost_ref = ins
    return _post_residual(_dot(a_ref[...], wo_ref[...]), x, mod_ref, gpost_ref)


def _swa_out_part(a, mod, gpost, w_out, latent):
    tm = TOK_TILE
    in_specs = [
        pl.BlockSpec((tm, SWA_WIDTH), lambda i: (i, 0)),
        _full((SWA_WIDTH, D_MODEL)),
        pl.BlockSpec((1, 3, D_MODEL), _mod_row_fn(latent, tm)),
        _full((1, D_MODEL)),
    ]
    return _Part("swa_out", _swa_out_compute, in_specs, [a, w_out, mod, gpost], None, None)


_SQ = (0, SWA_WIDTH)
_SK = (_SQ[1], _SQ[1] + SWA_KV_W)
_SV = (_SK[1], _SK[1] + SWA_KV_W)
_SG = (_SV[1], _SV[1] + SWA_WIDTH)
SWA_IN_COLS = _SG[1]
SWA_KV_DUP_W = 2 * SWA_KV_W


def _dup_heads(x, o_ref):
    for b in range(SWA_KV_W // LANES):
        blk = x[:, b * LANES:(b + 1) * LANES]
        rot = pltpu.roll(blk, HALF, 1)
        low = _lane_iota(blk.shape) < HALF
        o_ref[:, (2 * b) * LANES:(2 * b + 1) * LANES] = jnp.where(low, blk, rot).astype(BF16)
        o_ref[:, (2 * b + 1) * LANES:(2 * b + 2) * LANES] = jnp.where(low, rot, blk).astype(BF16)


def _swa_in_compute(x, ins, refs, *, rope, state):
    mod_ref, g_ref, win_ref = ins[:3]
    if rope:
        cos_ref, sin_ref = ins[3:5]
    q_ref, kd_ref, vd_ref, gs_ref = refs[:4]
    h = _modulated(x, mod_ref, g_ref)

    def proj(c):
        return _dot(h, win_ref[:, c[0]:c[1]])

    q = proj(_SQ) * (SWA_SCALE * LOG2E)
    k = proj(_SK)
    v = proj(_SV)
    if state is not None:
        _emit_states(state, (k, v), ins[3:5], refs[4:6])
    if rope:
        cos = cos_ref[...]
        sin = sin_ref[...]
        for b in range(SWA_WIDTH // LANES):
            blk = slice(b * LANES, (b + 1) * LANES)
            q_ref[:, blk] = _rope_block(q[:, blk], cos, sin, SWA_HEAD_DIM // 4).astype(BF16)
        k = jnp.concatenate(
            [_rope_block(k[:, b * LANES:(b + 1) * LANES], cos, sin, SWA_HEAD_DIM // 4)
             for b in range(SWA_KV_W // LANES)], axis=1)
    else:
        q_ref[...] = q.astype(BF16)
    _dup_heads(k, kd_ref)
    _dup_heads(v, vd_ref)
    gs_ref[...] = _silu(proj(_SG)).astype(BF16)


def _swa_in_part(n, mod, gpre, w_in, latent, rope_tabs, slot=None, prev=()):
    tm = TOK_TILE
    row = lambda i: (i, 0)
    in_specs = [
        pl.BlockSpec((1, 3, D_MODEL), _mod_row_fn(latent, tm)),
        _full((1, D_MODEL)),
        _full((D_MODEL, SWA_IN_COLS)),
    ]
    args = [mod, gpre, w_in]
    if latent:
        per_seq = DEC_SEQ // tm
        in_specs += [pl.BlockSpec((tm, LANES), lambda i: (i % per_seq, 0))] * 2
        args += list(rope_tabs)
    widths = [SWA_WIDTH, SWA_KV_DUP_W, SWA_KV_DUP_W, SWA_WIDTH]
    out_shape = [jax.ShapeDtypeStruct((n, c), BF16) for c in widths]
    out_specs = [pl.BlockSpec((tm, c), row) for c in widths]
    if not latent:
        s_in, s_args, s_out, s_shape = _state_specs(n, (SWA_KV_W, SWA_KV_W), slot, prev)
        in_specs += s_in
        args += s_args
        out_specs += s_out
        out_shape += s_shape
    compute = functools.partial(_swa_in_compute, rope=latent, state=slot)
    return _Part("swa_in", compute, in_specs, args, out_specs, out_shape)


SWA_GROUP = SWA_HEADS // SWA_KV_HEADS


def _swa_group_attend(qa, qb, parts, sinks):
    rows = qa.shape[0]
    low = _lane_iota(qa.shape) < HALF
    zero = jnp.zeros_like(qa)
    qst = jnp.concatenate(
        [jnp.where(low, qa, zero), jnp.where(low, zero, qa), jnp.where(low, qb, zero), jnp.where(low, zero, qb)],
        axis=0)
    sink_t = jnp.concatenate([jnp.full((rows, LANES), s * LOG2E, F32) for s in sinks], axis=0)
    scores = []
    t = sink_t
    for k, _, bias in parts:
        s = _dot_nt(qst, k)
        if bias is not None:
            s = s + bias
        scores.append(s)
        for cb in range(s.shape[1] // LANES):
            t = jnp.maximum(t, s[:, cb * LANES:(cb + 1) * LANES])
    m = jnp.max(t, axis=-1, keepdims=True)
    acc = None
    for s, (_, v, _) in zip(scores, parts):
        p = jnp.exp2(s - m).astype(BF16)
        pv = _dot(p, jnp.concatenate([v, jnp.ones_like(v)], axis=1))
        acc = pv if acc is None else acc + pv
    o = acc[:, :LANES] / (acc[:, LANES:] + jnp.exp2(sink_t - m))
    return (jnp.where(low, o[0:rows], o[rows:2 * rows]),
            jnp.where(low, o[2 * rows:3 * rows], o[3 * rows:4 * rows]))


def _swa_dense_body(sink_ref, q_ref, kd_ref, vd_ref, gs_ref, o_ref):
    for kh in range(SWA_KV_HEADS):
        kv = slice(kh * LANES, (kh + 1) * LANES)
        ta = slice(2 * kh * LANES, (2 * kh + 1) * LANES)
        tb = slice((2 * kh + 1) * LANES, (2 * kh + 2) * LANES)
        sinks = [sink_ref[SWA_GROUP * kh + e] for e in range(SWA_GROUP)]
        oa, ob = _swa_group_attend(q_ref[:, ta], q_ref[:, tb], [(kd_ref[:, kv], vd_ref[:, kv], None)], sinks)
        o_ref[:, ta] = (oa * gs_ref[:, ta].astype(F32)).astype(BF16)
        o_ref[:, tb] = (ob * gs_ref[:, tb].astype(F32)).astype(BF16)


def _swa_dense_call(sink, q, kd, vd, gs):
    n = q.shape[0]
    row = lambda b: (b, 0)
    return pl.pallas_call(
        _swa_dense_body,
        grid=(n // SEQ,),
        in_specs=[
            pl.BlockSpec(memory_space=pltpu.SMEM),
            pl.BlockSpec((SEQ, SWA_WIDTH), row),
            pl.BlockSpec((SEQ, SWA_KV_DUP_W), row),
            pl.BlockSpec((SEQ, SWA_KV_DUP_W), row),
            pl.BlockSpec((SEQ, SWA_WIDTH), row),
        ],
        out_specs=pl.BlockSpec((SEQ, SWA_WIDTH), row),
        out_shape=jax.ShapeDtypeStruct((n, SWA_WIDTH), BF16),
        compiler_params=_params(),
        name="swa_attn_context",
    )(sink, q, kd, vd, gs)


SWA_Q_TILE = 1024
SWA_SPAN = 3 * SWA_WINDOW


def _swa_band_body(sink_ref, q_ref, kd_ref, vd_ref, ck_ref, cv_ref, gs_ref, o_ref, kx_ref, vx_ref):
    w = SWA_WINDOW
    nsub = q_ref.shape[0] // w
    step = pl.program_id(1)

    @pl.when(step == 0)
    def _():
        _dup_heads(ck_ref[0, 0], kx_ref)
        _dup_heads(cv_ref[0, 0], vx_ref)

    r = lax.broadcasted_iota(jnp.int32, (w, SWA_SPAN), 0)
    c = lax.broadcasted_iota(jnp.int32, (w, SWA_SPAN), 1)

    def block(sb, carry):
        qbase = (step * nsub + sb) * w
        start = pl.multiple_of(jnp.clip(qbase - w, 0, DEC_SEQ - SWA_SPAN), w)
        rel = c - r + (start - qbase)
        bias = jnp.where(jnp.abs(rel) <= w, 0.0, NEG_INF).astype(F32)
        bias = jnp.concatenate([bias] * SWA_GROUP, axis=0)
        rows = pl.ds(pl.multiple_of(sb * w, w), w)
        span = pl.ds(start, SWA_SPAN)
        for kh in range(SWA_KV_HEADS):
            kv = slice(kh * LANES, (kh + 1) * LANES)
            ta = slice(2 * kh * LANES, (2 * kh + 1) * LANES)
            tb = slice((2 * kh + 1) * LANES, (2 * kh + 2) * LANES)
            sinks = [sink_ref[SWA_GROUP * kh + e] for e in range(SWA_GROUP)]
            parts = [(kd_ref[span, kv], vd_ref[span, kv], bias), (kx_ref[:, kv], vx_ref[:, kv], None)]
            oa, ob = _swa_group_attend(q_ref[rows, ta], q_ref[rows, tb], parts, sinks)
            o_ref[rows, ta] = (oa * gs_ref[rows, ta].astype(F32)).astype(BF16)
            o_ref[rows, tb] = (ob * gs_ref[rows, tb].astype(F32)).astype(BF16)
        return carry

    lax.fori_loop(0, nsub, block, 0, unroll=True)


def _swa_band_call(sink, q, kd, vd, cache_k, cache_v, slot, gs):
    tq = SWA_Q_TILE
    nq = DEC_SEQ // tq
    qrow = lambda b, i: (b * nq + i, 0)
    brow = lambda b, i: (b, 0)
    cache = pl.BlockSpec((1, 1, PAST_LEN, SWA_KV_W), lambda b, i: (b, slot, 0, 0))
    return pl.pallas_call(
        _swa_band_body,
        grid=(DEC_BATCH, nq),
        in_specs=[
            pl.BlockSpec(memory_space=pltpu.SMEM),
            pl.BlockSpec((tq, SWA_WIDTH), qrow),
            pl.BlockSpec((DEC_SEQ, SWA_KV_DUP_W), brow),
            pl.BlockSpec((DEC_SEQ, SWA_KV_DUP_W), brow),
            cache,
            cache,
            pl.BlockSpec((tq, SWA_WIDTH), qrow),
        ],
        out_specs=pl.BlockSpec((tq, SWA_WIDTH), qrow),
        out_shape=jax.ShapeDtypeStruct((DEC_BATCH * DEC_SEQ, SWA_WIDTH), BF16),
        scratch_shapes=[pltpu.VMEM((PAST_LEN, SWA_KV_DUP_W), BF16)] * 2,
        compiler_params=_params2(),
        name="swa_attn_latent",
    )(sink, q, kd, vd, cache_k, cache_v, gs)


def _rope_tables(rope_dims, lane_period, lane_off):
    n = rope_dims // 2
    hf = n // 2
    t = jnp.arange(DEC_SEQ)
    pos = jnp.stack([(t // GRID_W).astype(F32), (t % GRID_W).astype(F32)], axis=0)
    inv = ROPE_BASE ** (-jnp.arange(0, n, 2, dtype=F32) / n)
    ang = pos[:, :, None] * inv[None, None, :]
    cos = jnp.cos(ang)
    sin = jnp.sin(ang)
    cos_d = jnp.concatenate([cos, cos], axis=-1)
    sin_d = jnp.concatenate([-sin, sin], axis=-1)
    cos_r = jnp.concatenate([cos_d[0], cos_d[1]], axis=-1)
    sin_r = jnp.concatenate([sin_d[0], sin_d[1]], axis=-1)
    reps = LANES // lane_period
    pad = ((0, 0), (lane_off, lane_period - lane_off - rope_dims))
    cos_p = jnp.tile(jnp.pad(cos_r, pad, constant_values=1.0), (1, reps))
    sin_p = jnp.tile(jnp.pad(sin_r, pad), (1, reps))
    return cos_p, sin_p


def _prep_mla(i, mla_w_in, mla_g_qn, mla_g_kvn, mla_w_uq, mla_w_ukv, pool_w, pool_scale, mixa_w_out):
    w_in = mla_w_in[i]
    i1 = MLA_Q_RANK
    i2 = i1 + MLA_KV_RANK
    i3 = i2 + MLA_ROPE
    kr = jnp.pad(w_in[:, i2:i3], ((0, 0), (MLA_NOPE, LANES - MLA_NOPE - MLA_ROPE)))
    w_in_r = jnp.concatenate([w_in[:, :i2], kr, w_in[:, i3:]], axis=1).astype(BF16)
    dq = MLA_NOPE + MLA_ROPE
    w_uq = jnp.pad(mla_w_uq[i].reshape(MLA_Q_RANK, MLA_HEADS, dq),
                   ((0, 0), (0, 0), (0, MLA_HEAD_PAD - dq))).reshape(MLA_Q_RANK, MLA_QK_W).astype(BF16)
    ukv = mla_w_ukv[i].reshape(MLA_KV_RANK, MLA_HEADS, MLA_NOPE + MLA_V)
    w_uk = jnp.pad(ukv[:, :, :MLA_NOPE], ((0, 0), (0, 0), (0, MLA_HEAD_PAD - MLA_NOPE)))
    w_uk = w_uk.reshape(MLA_KV_RANK, MLA_QK_W).astype(BF16)
    w_uv = ukv[:, :, MLA_NOPE:].reshape(MLA_KV_RANK, MLA_WIDTH).astype(BF16)
    return {
        "w_in": w_in_r, "w_uq": w_uq, "w_uk": w_uk, "w_uv": w_uv,
        "g_qn": mla_g_qn[i].reshape(1, -1), "g_kvn": mla_g_kvn[i].reshape(1, -1),
        "w_pool": pool_w[i].astype(BF16), "s_pool": pool_scale[i].reshape(1, -1),
        "w_out": mixa_w_out[i].astype(BF16),
    }


def kernel(x_prompt, x_sample, cache_ckv, cache_krope, cache_k, cache_v, c, c_ctx, ada_w, ada_b, norm_pre, norm_post, mla_w_in, mla_g_qn, mla_g_kvn, mla_w_uq, mla_w_ukv, pool_w, pool_scale, mixa_w_out, swa_w_in, swa_sink, swa_w_out):
    n_p = BATCH * SEQ
    n_s = DEC_BATCH * DEC_SEQ
    xp = x_prompt.reshape(n_p, D_MODEL)
    xs = x_sample.reshape(n_s, D_MODEL)
    cvec = jnp.concatenate(
        [c_ctx[None], c, jnp.zeros((MOD_ROWS - 1 - DEC_BATCH, D_MODEL), F32)], axis=0)
    mods = _ada_call(cvec, ada_w, ada_b).reshape(DEPTH, MOD_ROWS, 3, D_MODEL)
    mla_tabs = _rope_tables(MLA_ROPE, LANES, MLA_NOPE)
    swa_tabs = _rope_tables(SWA_HEAD_DIM, SWA_HEAD_DIM, 0)
    ck_view = cache_k.reshape(DEC_BATCH, DEPTH // 2, PAST_LEN, SWA_KV_W)
    cv_view = cache_v.reshape(DEC_BATCH, DEPTH // 2, PAST_LEN, SWA_KV_W)
    mla_state = swa_state = ()
    pend_p = pend_s = None

    def stage(x, latent, out_part, in_part):
        res = list(_stage_call(x, latent, out_part, in_part))
        if out_part is not None:
            return res[0], res[1:]
        return x, res

    for l in range(DEPTH):
        i = l // 2
        mod = mods[l]
        gpre = norm_pre[l].reshape(1, D_MODEL)
        gpost = norm_post[l].reshape(1, D_MODEL)
        if l % 2 == 0:
            w = _prep_mla(i, mla_w_in, mla_g_qn, mla_g_kvn, mla_w_uq, mla_w_ukv, pool_w, pool_scale,
                          mixa_w_out)
            xp, (qp, kcp, vp, gap, vpp, gbp, *mla_state) = stage(
                xp, False, pend_p, _mla_in_part(n_p, mod, gpre, w, False, None, i, mla_state))
            xs, (qs, kcs, vs, gas, vps, gbs) = stage(
                xs, True, pend_s, _mla_in_part(n_s, mod, gpre, w, True, mla_tabs))
            ctx_kr = jnp.pad(cache_krope[:, i].reshape(DEC_BATCH * PAST_LEN, MLA_ROPE),
                             ((0, 0), (MLA_NOPE, LANES - MLA_NOPE - MLA_ROPE)))
            ctx = (cache_ckv, i, ctx_kr, w["w_uk"], w["w_uv"])
            ap = _mla_attn_call(qp, kcp, vp, gap, BATCH, SEQ)
            as_ = _mla_attn_call(qs, kcs, vs, gas, DEC_BATCH, DEC_SEQ, ctx)
            pend_p = _mla_out_part(n_p, ap, vpp, gbp, mod, gpost, w, False)
            pend_s = _mla_out_part(n_s, as_, vps, gbs, mod, gpost, w, True)
        else:
            w_in = swa_w_in[i].astype(BF16)
            w_out = swa_w_out[i].astype(BF16)
            sink = swa_sink[i]
            xp, (qp, kdp, vdp, gsp, *swa_state) = stage(
                xp, False, pend_p, _swa_in_part(n_p, mod, gpre, w_in, False, None, i, swa_state))
            xs, (qs, kds, vds, gss) = stage(
                xs, True, pend_s, _swa_in_part(n_s, mod, gpre, w_in, True, swa_tabs))

            ap = _swa_dense_call(sink, qp, kdp, vdp, gsp)
            as_ = _swa_band_call(sink, qs, kds, vds, ck_view, cv_view, i, gss)
            pend_p = _swa_out_part(ap, mod, gpost, w_out, False)
            pend_s = _swa_out_part(as_, mod, gpost, w_out, True)
    xp, _ = stage(xp, False, pend_p, None)
    xs, _ = stage(xs, True, pend_s, None)
    return (
        xp.reshape(BATCH, SEQ, D_MODEL),
        xs.reshape(DEC_BATCH, DEC_SEQ, D_MODEL),
        mla_state[0].reshape(BATCH, DEPTH // 2, SEQ, MLA_KV_RANK),
        mla_state[1].reshape(BATCH, DEPTH // 2, SEQ, MLA_ROPE),
        swa_state[0].reshape(BATCH, DEPTH // 2, SEQ, SWA_KV_HEADS, SWA_HEAD_DIM),
        swa_state[1].reshape(BATCH, DEPTH // 2, SEQ, SWA_KV_HEADS, SWA_HEAD_DIM),
    )
```

```python
import functools

import numpy as np
import jax
import jax.numpy as jnp
from jax import lax
from jax.experimental import pallas as pl
from jax.experimental.pallas import tpu as pltpu

F32 = jnp.float32
BF16 = jnp.bfloat16

D_MODEL = 1024
BATCH = 32
SEQ = 256
DEPTH = 4
DEC_BATCH = 8
DEC_SEQ = 2048
PAST_LEN = 256
GRID_W = 64
ROPE_BASE = 10000.0
EPS = 1e-6
NEG_INF = -1e30

MLA_HEADS = 8
MLA_NOPE = 64
MLA_ROPE = 32
MLA_V = 64
MLA_Q_RANK = 384
MLA_KV_RANK = 256
MLA_WIDTH = MLA_HEADS * MLA_V
MLA_SCALE = (MLA_NOPE + MLA_ROPE) ** -0.5
POOL_WINDOWS = (2, 4, 8, 16)
POOL_GROUPS = 4
POOL_GROUP_W = D_MODEL // 8
POOL_WIDTH = POOL_GROUPS * POOL_GROUP_W
SWA_HEADS = 16
SWA_KV_HEADS = 4
SWA_HEAD_DIM = 64
SWA_WINDOW = 128
SWA_WIDTH = SWA_HEADS * SWA_HEAD_DIM
SWA_KV_W = SWA_KV_HEADS * SWA_HEAD_DIM
SWA_SCALE = SWA_HEAD_DIM ** -0.5

LANES = 128
HALF = LANES // 2
BF16_SUBLANES = 16
MOD_ROWS = 16
TOK_TILE = 1024
LOG2E = 1.4426950408889634
POOL_BLOCK = 128
POOL_HALO = 64
VMEM_LIMIT = 60 * 1024 * 1024

_CQ = (0, MLA_Q_RANK)
_CKV = (_CQ[1], _CQ[1] + MLA_KV_RANK)
_KR = (_CKV[1], _CKV[1] + LANES)
_GA = (_KR[1], _KR[1] + MLA_WIDTH)
_VP = (_GA[1], _GA[1] + POOL_WIDTH)
_GB = (_VP[1], _VP[1] + POOL_WIDTH)
MLA_IN_COLS = _GB[1]
MLA_HEAD_PAD = LANES
MLA_QK_W = MLA_HEADS * MLA_HEAD_PAD
MLA_Q_TILE = 1024
CTX_SEQS_PER_STEP = 4


def _params():
    return pltpu.CompilerParams(dimension_semantics=("arbitrary",), vmem_limit_bytes=VMEM_LIMIT)


def _params2():
    return pltpu.CompilerParams(dimension_semantics=("arbitrary", "arbitrary"), vmem_limit_bytes=VMEM_LIMIT)


def _dot(a, b):
    return jnp.dot(a, b, preferred_element_type=F32)


def _dot_nt(a, b):
    return lax.dot_general(a, b, (((1,), (1,)), ((), ())), preferred_element_type=F32)


def _silu(x):
    return x / (1.0 + jnp.exp(-x))


def _rms(x):
    return x * lax.rsqrt(jnp.mean(x * x, axis=-1, keepdims=True) + EPS)


def _lane_iota(shape):
    return lax.broadcasted_iota(jnp.int32, shape, len(shape) - 1)


def _full(shape):
    nd = len(shape)
    return pl.BlockSpec(shape, lambda *_: (0,) * nd, pipeline_mode=pl.Buffered(1))


def _ada_body(c_ref, w_ref, b_ref, o_ref):
    s = _silu(c_ref[...]).astype(BF16)
    o_ref[0] = _dot(s, w_ref[0].astype(BF16)) + b_ref[0]


def _ada_call(cvec, ada_w, ada_b):
    tn = 768
    return pl.pallas_call(
        _ada_body,
        grid=(DEPTH, 3 * D_MODEL // tn),
        in_specs=[
            pl.BlockSpec((MOD_ROWS, D_MODEL), lambda l, j: (0, 0)),
            pl.BlockSpec((1, D_MODEL, tn), lambda l, j: (l, 0, j)),
            pl.BlockSpec((1, 1, tn), lambda l, j: (l, 0, j)),
        ],
        out_specs=pl.BlockSpec((1, MOD_ROWS, tn), lambda l, j: (l, 0, j)),
        out_shape=jax.ShapeDtypeStruct((DEPTH, MOD_ROWS, 3 * D_MODEL), F32),
        compiler_params=_params2(),
        name="ada_mod",
    )(cvec, ada_w, ada_b.reshape(DEPTH, 1, 3 * D_MODEL))


def _modulated(x, mod_ref, g_ref):
    m = mod_ref[0]
    gain = g_ref[...] * (1.0 + m[1:2, :])
    return (_rms(x) * gain + m[0:1, :]).astype(BF16)


def _rope_block(b, cos, sin, partner):
    first = (_lane_iota(b.shape) & partner) == 0
    sw = jnp.where(first, pltpu.roll(b, LANES - partner, 1), pltpu.roll(b, partner, 1))
    return b * cos + sw * sin


def _emit_states(slot, values, prev_refs, out_refs):
    for idx, (val, out_ref) in enumerate(zip(values, out_refs)):
        if slot == 0:
            out_ref[...] = val
        else:
            prev_ref = prev_refs[idx]
            nb = val.shape[0] // SEQ
            out_ref[:, 0:SEQ, :] = prev_ref[...].reshape(nb, SEQ, val.shape[1])
            out_ref[:, SEQ:2 * SEQ, :] = val.reshape(nb, SEQ, val.shape[1])


def _state_specs(n, widths, slot, prev):
    tm = TOK_TILE
    if slot == 0:
        specs = [pl.BlockSpec((tm, c), lambda i: (i, 0)) for c in widths]
        return [], [], specs, [jax.ShapeDtypeStruct((n, c), F32) for c in widths]
    nb = tm // SEQ
    ins = [pl.BlockSpec((tm, c), lambda i: (i, 0)) for c in widths]
    outs = [pl.BlockSpec((nb, 2 * SEQ, c), lambda i: (i, 0, 0)) for c in widths]
    return ins, list(prev), outs, [jax.ShapeDtypeStruct((n // SEQ, 2 * SEQ, c), F32) for c in widths]


def _mod_row_fn(latent, tile):
    if latent:
        per_seq = DEC_SEQ // tile
        return lambda i: (1 + i // per_seq, 0, 0)
    return lambda i: (0, 0, 0)


def _mla_in_compute(x, ins, refs, *, rope, state):
    mod_ref, g_ref, win_ref, gq_ref, gkv_ref, wuq_ref, wuk_ref, wuv_ref = ins[:8]
    if rope:
        cos_ref, sin_ref = ins[8:10]
    q_ref, kc_ref, v_ref, ga_ref, vp_ref, gb_ref = refs[:6]
    h = _modulated(x, mod_ref, g_ref)

    def proj(c):
        return _dot(h, win_ref[:, c[0]:c[1]])

    cqn = (_rms(proj(_CQ)) * gq_ref[...]).astype(BF16)
    q = _dot(cqn, wuq_ref[...]) * (MLA_SCALE * LOG2E)
    ckvn = _rms(proj(_CKV)) * gkv_ref[...]
    ckvb = ckvn.astype(BF16)
    kr = proj(_KR)
    if state is not None:
        kr_state = pltpu.roll(kr, LANES - MLA_NOPE, 1)[:, :MLA_ROPE]
        _emit_states(state, (ckvn, kr_state), ins[8:10], refs[6:8])
    if rope:
        cos = cos_ref[...]
        sin = sin_ref[...]
        kr = _rope_block(kr, cos, sin, MLA_ROPE // 4)
    kc = _dot(ckvb, wuk_ref[...])
    for hh in range(MLA_HEADS):
        blk = slice(hh * MLA_HEAD_PAD, (hh + 1) * MLA_HEAD_PAD)
        qb = q[:, blk]
        if rope:
            qb = _rope_block(qb, cos, sin, MLA_ROPE // 4)
        q_ref[:, blk] = qb.astype(BF16)
        kc_ref[:, blk] = (kc[:, blk] + kr).astype(BF16)
    v_ref[...] = _dot(ckvb, wuv_ref[...]).astype(BF16)
    ga_ref[...] = _silu(proj(_GA)).astype(BF16)
    vp_ref[...] = proj(_VP).astype(BF16)
    gb_ref[...] = _silu(proj(_GB)).astype(BF16)


class _Part:
    def __init__(self, name, compute, in_specs, args, out_specs, out_shape):
        self.name = name
        self.compute = compute
        self.in_specs = in_specs
        self.args = args
        self.out_specs = out_specs
        self.out_shape = out_shape


def _mla_in_part(n, mod, gpre, w, latent, rope_tabs, slot=None, prev=()):
    tm = TOK_TILE
    row = lambda i: (i, 0)
    in_specs = [
        pl.BlockSpec((1, 3, D_MODEL), _mod_row_fn(latent, tm)),
        _full((1, D_MODEL)),
        _full((D_MODEL, MLA_IN_COLS)),
        _full((1, MLA_Q_RANK)),
        _full((1, MLA_KV_RANK)),
        _full((MLA_Q_RANK, MLA_QK_W)),
        _full((MLA_KV_RANK, MLA_QK_W)),
        _full((MLA_KV_RANK, MLA_WIDTH)),
    ]
    args = [mod, gpre, w["w_in"], w["g_qn"], w["g_kvn"], w["w_uq"], w["w_uk"], w["w_uv"]]
    if latent:
        per_seq = DEC_SEQ // tm
        in_specs += [pl.BlockSpec((tm, LANES), lambda i: (i % per_seq, 0))] * 2
        args += list(rope_tabs)
    widths = [MLA_QK_W, MLA_QK_W, MLA_WIDTH, MLA_WIDTH, POOL_WIDTH, POOL_WIDTH]
    out_shape = [jax.ShapeDtypeStruct((n, c), BF16) for c in widths]
    out_specs = [pl.BlockSpec((tm, c), row) for c in widths]
    if not latent:
        s_in, s_args, s_out, s_shape = _state_specs(n, (MLA_KV_RANK, MLA_ROPE), slot, prev)
        in_specs += s_in
        args += s_args
        out_specs += s_out
        out_shape += s_shape
    compute = functools.partial(_mla_in_compute, rope=latent, state=slot)
    return _Part("mla_in", compute, in_specs, args, out_specs, out_shape)


def _stage_body(*refs, out_part, in_part):
    x_ref = refs[0]
    n_out_in = len(out_part.in_specs) if out_part else 0
    n_in_in = len(in_part.in_specs) if in_part else 0
    out_ins = refs[1:1 + n_out_in]
    in_ins = refs[1 + n_out_in:1 + n_out_in + n_in_in]
    outs = refs[1 + n_out_in + n_in_in:]
    x = x_ref[...]
    if out_part:
        x = out_part.compute(x, out_ins)
        outs[0][...] = x
        outs = outs[1:]
    if in_part:
        in_part.compute(x, in_ins, outs)


def _stage_call(x, latent, out_part, in_part):
    n = x.shape[0]
    tm = TOK_TILE
    row = lambda i: (i, 0)
    in_specs = [pl.BlockSpec((tm, D_MODEL), row)]
    args = [x]
    out_specs, out_shape = [], []
    names = []
    if out_part:
        in_specs += out_part.in_specs
        args += out_part.args
        out_specs.append(pl.BlockSpec((tm, D_MODEL), row))
        out_shape.append(jax.ShapeDtypeStruct((n, D_MODEL), F32))
        names.append(out_part.name)
    if in_part:
        in_specs += in_part.in_specs
        args += in_part.args
        out_specs += in_part.out_specs
        out_shape += in_part.out_shape
        names.append(in_part.name)
    names.append("latent" if latent else "context")
    return pl.pallas_call(
        functools.partial(_stage_body, out_part=out_part, in_part=in_part),
        grid=(n // tm,),
        in_specs=in_specs,
        out_specs=out_specs,
        out_shape=out_shape,
        compiler_params=_params(),
        name="_".join(names),
    )(*args)


def _mla_attn_body(*refs, has_ctx, nseq):
    if has_ctx:
        q_ref, kc_ref, v_ref, ckv_ref, kr_ref, wuk_ref, wuv_ref, ga_ref, o_ref, va_ref, kcc_ref, vca_ref = refs
    else:
        q_ref, kc_ref, v_ref, ga_ref, o_ref, va_ref = refs

    def augment(v, dst):
        for j in range(MLA_HEADS // 2):
            dst[:, 2 * j * LANES:(2 * j + 1) * LANES] = v[:, j * LANES:(j + 1) * LANES]
            dst[:, (2 * j + 1) * LANES:(2 * j + 2) * LANES] = jnp.ones((v.shape[0], LANES), BF16)

    @pl.when(pl.program_id(1) == 0)
    def _():
        augment(v_ref[...], va_ref)
        if has_ctx:
            ckvb = ckv_ref[0, 0].astype(BF16)
            kr = kr_ref[...]
            kc = _dot(ckvb, wuk_ref[...])
            for hh in range(MLA_HEADS):
                blk = slice(hh * MLA_HEAD_PAD, (hh + 1) * MLA_HEAD_PAD)
                kcc_ref[:, blk] = (kc[:, blk] + kr).astype(BF16)
            augment(_dot(ckvb, wuv_ref[...]).astype(BF16), vca_ref)

    def lane_tile_max(t, s):
        for cb in range(s.shape[1] // LANES):
            blk = s[:, cb * LANES:(cb + 1) * LANES]
            t = blk if t is None else jnp.maximum(t, blk)
        return t

    qrows = q_ref.shape[0] // nseq
    krows = kc_ref.shape[0] // nseq
    low = _lane_iota((qrows, LANES)) < HALF
    for si in range(nseq):
        qr = slice(si * qrows, (si + 1) * qrows)
        kr_ = slice(si * krows, (si + 1) * krows)
        for j in range(MLA_HEADS // 2):
            pair = slice(j * LANES, (j + 1) * LANES)
            vaug = slice(2 * j * LANES, (2 * j + 2) * LANES)
            outs = []
            for hh in (2 * j, 2 * j + 1):
                blk = slice(hh * MLA_HEAD_PAD, (hh + 1) * MLA_HEAD_PAD)
                qh = q_ref[qr, blk]
                s = _dot_nt(qh, kc_ref[kr_, blk])
                t = lane_tile_max(None, s)
                if has_ctx:
                    sc = _dot_nt(qh, kcc_ref[:, blk])
                    t = lane_tile_max(t, sc)
                m = jnp.max(t, axis=-1, keepdims=True)
                acc = _dot(jnp.exp2(s - m).astype(BF16), va_ref[kr_, vaug])
                if has_ctx:
                    acc = acc + _dot(jnp.exp2(sc - m).astype(BF16), vca_ref[:, vaug])
                outs.append(acc[:, :LANES] / acc[:, LANES:])
            o_pair = jnp.where(low, outs[0], outs[1])
            o_ref[qr, pair] = (o_pair * ga_ref[qr, pair].astype(F32)).astype(BF16)


def _mla_attn_call(q, kc, v, ga, nbatch, seq, ctx=None):
    nseq = 1 if ctx is not None else CTX_SEQS_PER_STEP
    tq = min(seq, MLA_Q_TILE) * nseq
    seq = seq * nseq
    nbatch = nbatch // nseq
    scratch = [pltpu.VMEM((seq, 2 * MLA_WIDTH), BF16)]
    if ctx is not None:
        scratch += [pltpu.VMEM((PAST_LEN, MLA_QK_W), BF16), pltpu.VMEM((PAST_LEN, 2 * MLA_WIDTH), BF16)]
    nq = seq // tq
    qrow = lambda b, i: (b * nq + i, 0)
    brow = lambda b, i: (b, 0)
    in_specs = [
        pl.BlockSpec((tq, MLA_QK_W), qrow),
        pl.BlockSpec((seq, MLA_QK_W), brow),
        pl.BlockSpec((seq, MLA_WIDTH), brow),
    ]
    args = [q, kc, v]
    if ctx is not None:
        cache, slot, ctx_kr, w_uk, w_uv = ctx
        in_specs += [
            pl.BlockSpec((1, 1, PAST_LEN, MLA_KV_RANK), lambda b, i: (b, slot, 0, 0)),
            pl.BlockSpec((PAST_LEN, LANES), brow),
            _full((MLA_KV_RANK, MLA_QK_W)),
            _full((MLA_KV_RANK, MLA_WIDTH)),
        ]
        args += [cache, ctx_kr, w_uk, w_uv]
    in_specs.append(pl.BlockSpec((tq, MLA_WIDTH), qrow))
    args.append(ga)
    return pl.pallas_call(
        functools.partial(_mla_attn_body, has_ctx=ctx is not None, nseq=nseq),
        grid=(nbatch, nq),
        in_specs=in_specs,
        out_specs=pl.BlockSpec((tq, MLA_WIDTH), qrow),
        out_shape=jax.ShapeDtypeStruct((nbatch * seq, MLA_WIDTH), BF16),
        scratch_shapes=scratch,
        compiler_params=_params2(),
        name="mla_attn_latent" if ctx is not None else "mla_attn_context",
    )(*args)


def _post_residual(out, x, mod_ref, gpost_ref):
    m = mod_ref[0]
    return x + _rms(out) * (m[2:3, :] * gpost_ref[...])


def _mla_out_compute(x, ins, *, seq):
    a_ref, vp_ref, vprev_ref, vnext_ref, gb_ref, pm_ref, wp_ref, sp_ref, wo_ref, mod_ref, gpost_ref = ins
    tm = vp_ref.shape[0]
    cur = vp_ref[...]
    slab = jnp.concatenate([vprev_ref[...], cur, vnext_ref[...]], axis=0)
    srow = lax.broadcasted_iota(jnp.int32, (POOL_BLOCK + 2 * POOL_HALO, POOL_WIDTH), 0)
    trow = lax.broadcasted_iota(jnp.int32, (POOL_BLOCK, LANES), 0)
    zero = jnp.zeros((POOL_BLOCK + 2 * POOL_HALO, POOL_WIDTH), BF16)
    pooled = [[] for _ in POOL_WINDOWS]
    for sb in range(tm // POOL_BLOCK):
        pos0 = (pl.program_id(0) * tm + sb * POOL_BLOCK) % seq
        lo_cut = jnp.where(pos0 == 0, POOL_HALO, 0)
        hi_cut = jnp.where(pos0 == seq - POOL_BLOCK, POOL_BLOCK + POOL_HALO, POOL_BLOCK + 2 * POOL_HALO)
        chunk = slab[sb * POOL_BLOCK:(sb + 1) * POOL_BLOCK + 2 * POOL_HALO]
        chunk = jnp.where(srow >= lo_cut, chunk, zero)
        chunk = jnp.where(srow < hi_cut, chunk, zero)
        t = pos0 + trow
        for g, w in enumerate(POOL_WINDOWS):
            cols = slice(g * POOL_GROUP_W, (g + 1) * POOL_GROUP_W)
            tot = _dot(pm_ref[g], chunk[:, cols])
            cnt = jnp.minimum(t + w // 2, seq) - jnp.maximum(t - w // 2, 0)
            own = cur[sb * POOL_BLOCK:(sb + 1) * POOL_BLOCK, cols].astype(F32)
            pooled[g].append((tot / cnt.astype(F32) - own).astype(BF16))
    bs = []
    for g in range(POOL_GROUPS):
        cols = slice(g * POOL_GROUP_W, (g + 1) * POOL_GROUP_W)
        y = _dot(jnp.concatenate(pooled[g], axis=0), wp_ref[g]) * sp_ref[:, cols]
        bs.append((y * gb_ref[:, cols].astype(F32)).astype(BF16))
    b = jnp.concatenate(bs, axis=1)
    out = _dot(a_ref[...], wo_ref[0:MLA_WIDTH, :]) + _dot(b, wo_ref[MLA_WIDTH:, :])
    return _post_residual(out, x, mod_ref, gpost_ref)


def _pool_masks():
    r = np.arange(POOL_BLOCK)[:, None]
    c = np.arange(POOL_BLOCK + 2 * POOL_HALO)[None, :]
    rel = c - POOL_HALO - r
    ms = [((rel >= -(w // 2)) & (rel <= w // 2 - 1)) for w in POOL_WINDOWS]
    return jnp.asarray(np.stack(ms).astype(np.float32), dtype=BF16)


def _mla_out_part(n, a, vp, gb, mod, gpost, w, latent):
    tm = TOK_TILE
    seq = DEC_SEQ if latent else SEQ
    row = lambda i: (i, 0)
    hb = tm // POOL_HALO
    nhb = n // POOL_HALO
    in_specs = [
        pl.BlockSpec((tm, MLA_WIDTH), row),
        pl.BlockSpec((tm, POOL_WIDTH), row),
        pl.BlockSpec((POOL_HALO, POOL_WIDTH), lambda i: (jnp.maximum(i * hb - 1, 0), 0)),
        pl.BlockSpec((POOL_HALO, POOL_WIDTH), lambda i: (jnp.minimum((i + 1) * hb, nhb - 1), 0)),
        pl.BlockSpec((tm, POOL_WIDTH), row),
        _full((POOL_GROUPS, POOL_BLOCK, POOL_BLOCK + 2 * POOL_HALO)),
        _full((POOL_GROUPS, POOL_GROUP_W, POOL_GROUP_W)),
        _full((1, POOL_WIDTH)),
        _full((MLA_WIDTH + POOL_WIDTH, D_MODEL)),
        pl.BlockSpec((1, 3, D_MODEL), _mod_row_fn(latent, tm)),
        _full((1, D_MODEL)),
    ]
    args = [a, vp, vp, vp, gb, _pool_masks(), w["w_pool"], w["s_pool"], w["w_out"], mod, gpost]
    return _Part("mla_out", functools.partial(_mla_out_compute, seq=seq), in_specs, args, None, None)


def _swa_out_compute(x, ins):
    a_ref, wo_ref, mod_ref, gpost_ref = ins
    return _post_residual(_dot(a_ref[...], wo_ref[...]), x, mod_ref, gpost_ref)


def _swa_out_part(a, mod, gpost, w_out, latent):
    tm = TOK_TILE
    in_specs = [
        pl.BlockSpec((tm, SWA_WIDTH), lambda i: (i, 0)),
        _full((SWA_WIDTH, D_MODEL)),
        pl.BlockSpec((1, 3, D_MODEL), _mod_row_fn(latent, tm)),
        _full((1, D_MODEL)),
    ]
    return _Part("swa_out", _swa_out_compute, in_specs, [a, w_out, mod, gpost], None, None)


_SQ = (0, SWA_WIDTH)
_SK = (_SQ[1], _SQ[1] + SWA_KV_W)
_SV = (_SK[1], _SK[1] + SWA_KV_W)
_SG = (_SV[1], _SV[1] + SWA_WIDTH)
SWA_IN_COLS = _SG[1]
SWA_KV_DUP_W = 2 * SWA_KV_W


def _dup_heads(x, o_ref):
    for b in range(SWA_KV_W // LANES):
        blk = x[:, b * LANES:(b + 1) * LANES]
        rot = pltpu.roll(blk, HALF, 1)
        low = _lane_iota(blk.shape) < HALF
        o_ref[:, (2 * b) * LANES:(2 * b + 1) * LANES] = jnp.where(low, blk, rot).astype(BF16)
        o_ref[:, (2 * b + 1) * LANES:(2 * b + 2) * LANES] = jnp.where(low, rot, blk).astype(BF16)


def _swa_in_compute(x, ins, refs, *, rope, state):
    mod_ref, g_ref, win_ref = ins[:3]
    if rope:
        cos_ref, sin_ref = ins[3:5]
    q_ref, kd_ref, vd_ref, gs_ref = refs[:4]
    h = _modulated(x, mod_ref, g_ref)

    def proj(c):
        return _dot(h, win_ref[:, c[0]:c[1]])

    q = proj(_SQ) * (SWA_SCALE * LOG2E)
    k = proj(_SK)
    v = proj(_SV)
    if state is not None:
        _emit_states(state, (k, v), ins[3:5], refs[4:6])
    if rope:
        cos = cos_ref[...]
        sin = sin_ref[...]
        for b in range(SWA_WIDTH // LANES):
            blk = slice(b * LANES, (b + 1) * LANES)
            q_ref[:, blk] = _rope_block(q[:, blk], cos, sin, SWA_HEAD_DIM // 4).astype(BF16)
        k = jnp.concatenate(
            [_rope_block(k[:, b * LANES:(b + 1) * LANES], cos, sin, SWA_HEAD_DIM // 4)
             for b in range(SWA_KV_W // LANES)], axis=1)
    else:
        q_ref[...] = q.astype(BF16)
    _dup_heads(k, kd_ref)
    _dup_heads(v, vd_ref)
    gs_ref[...] = _silu(proj(_SG)).astype(BF16)


def _swa_in_part(n, mod, gpre, w_in, latent, rope_tabs, slot=None, prev=()):
    tm = TOK_TILE
    row = lambda i: (i, 0)
    in_specs = [
        pl.BlockSpec((1, 3, D_MODEL), _mod_row_fn(latent, tm)),
        _full((1, D_MODEL)),
        _full((D_MODEL, SWA_IN_COLS)),
    ]
    args = [mod, gpre, w_in]
    if latent:
        per_seq = DEC_SEQ // tm
        in_specs += [pl.BlockSpec((tm, LANES), lambda i: (i % per_seq, 0))] * 2
        args += list(rope_tabs)
    widths = [SWA_WIDTH, SWA_KV_DUP_W, SWA_KV_DUP_W, SWA_WIDTH]
    out_shape = [jax.ShapeDtypeStruct((n, c), BF16) for c in widths]
    out_specs = [pl.BlockSpec((tm, c), row) for c in widths]
    if not latent:
        s_in, s_args, s_out, s_shape = _state_specs(n, (SWA_KV_W, SWA_KV_W), slot, prev)
        in_specs += s_in
        args += s_args
        out_specs += s_out
        out_shape += s_shape
    compute = functools.partial(_swa_in_compute, rope=latent, state=slot)
    return _Part("swa_in", compute, in_specs, args, out_specs, out_shape)


SWA_GROUP = SWA_HEADS // SWA_KV_HEADS


def _swa_group_attend(qa, qb, parts, sinks):
    rows = qa.shape[0]
    low = _lane_iota(qa.shape) < HALF
    zero = jnp.zeros_like(qa)
    qst = jnp.concatenate(
        [jnp.where(low, qa, zero), jnp.where(low, zero, qa), jnp.where(low, qb, zero), jnp.where(low, zero, qb)],
        axis=0)
    sink_t = jnp.concatenate([jnp.full((rows, LANES), s * LOG2E, F32) for s in sinks], axis=0)
    scores = []
    t = sink_t
    for k, _, bias in parts:
        s = _dot_nt(qst, k)
        if bias is not None:
            s = s + bias
        scores.append(s)
        for cb in range(s.shape[1] // LANES):
            t = jnp.maximum(t, s[:, cb * LANES:(cb + 1) * LANES])
    m = jnp.max(t, axis=-1, keepdims=True)
    acc = None
    for s, (_, v, _) in zip(scores, parts):
        p = jnp.exp2(s - m).astype(BF16)
        pv = _dot(p, jnp.concatenate([v, jnp.ones_like(v)], axis=1))
        acc = pv if acc is None else acc + pv
    o = acc[:, :LANES] / (acc[:, LANES:] + jnp.exp2(sink_t - m))
    return (jnp.where(low, o[0:rows], o[rows:2 * rows]),
            jnp.where(low, o[2 * rows:3 * rows], o[3 * rows:4 * rows]))


def _swa_dense_body(sink_ref, q_ref, kd_ref, vd_ref, gs_ref, o_ref):
    for si in range(q_ref.shape[0] // SEQ):
        rows = slice(si * SEQ, (si + 1) * SEQ)
        for kh in range(SWA_KV_HEADS):
            kv = slice(kh * LANES, (kh + 1) * LANES)
            ta = slice(2 * kh * LANES, (2 * kh + 1) * LANES)
            tb = slice((2 * kh + 1) * LANES, (2 * kh + 2) * LANES)
            sinks = [sink_ref[SWA_GROUP * kh + e] for e in range(SWA_GROUP)]
            parts = [(kd_ref[rows, kv], vd_ref[rows, kv], None)]
            oa, ob = _swa_group_attend(q_ref[rows, ta], q_ref[rows, tb], parts, sinks)
            o_ref[rows, ta] = (oa * gs_ref[rows, ta].astype(F32)).astype(BF16)
            o_ref[rows, tb] = (ob * gs_ref[rows, tb].astype(F32)).astype(BF16)


def _swa_dense_call(sink, q, kd, vd, gs):
    n = q.shape[0]
    tr = CTX_SEQS_PER_STEP * SEQ
    row = lambda b: (b, 0)
    return pl.pallas_call(
        _swa_dense_body,
        grid=(n // tr,),
        in_specs=[
            pl.BlockSpec(memory_space=pltpu.SMEM),
            pl.BlockSpec((tr, SWA_WIDTH), row),
            pl.BlockSpec((tr, SWA_KV_DUP_W), row),
            pl.BlockSpec((tr, SWA_KV_DUP_W), row),
            pl.BlockSpec((tr, SWA_WIDTH), row),
        ],
        out_specs=pl.BlockSpec((tr, SWA_WIDTH), row),
        out_shape=jax.ShapeDtypeStruct((n, SWA_WIDTH), BF16),
        compiler_params=_params(),
        name="swa_attn_context",
    )(sink, q, kd, vd, gs)


SWA_Q_TILE = 1024
SWA_SPAN = 3 * SWA_WINDOW


def _swa_band_body(sink_ref, q_ref, kd_ref, vd_ref, ck_ref, cv_ref, gs_ref, o_ref, kx_ref, vx_ref):
    w = SWA_WINDOW
    nsub = q_ref.shape[0] // w
    step = pl.program_id(1)

    @pl.when(step == 0)
    def _():
        _dup_heads(ck_ref[0, 0], kx_ref)
        _dup_heads(cv_ref[0, 0], vx_ref)

    r = lax.broadcasted_iota(jnp.int32, (w, SWA_SPAN), 0)
    c = lax.broadcasted_iota(jnp.int32, (w, SWA_SPAN), 1)

    def block(sb, carry):
        qbase = (step * nsub + sb) * w
        start = pl.multiple_of(jnp.clip(qbase - w, 0, DEC_SEQ - SWA_SPAN), w)
        rel = c - r + (start - qbase)
        bias = jnp.where(jnp.abs(rel) <= w, 0.0, NEG_INF).astype(F32)
        bias = jnp.concatenate([bias] * SWA_GROUP, axis=0)
        rows = pl.ds(pl.multiple_of(sb * w, w), w)
        span = pl.ds(start, SWA_SPAN)
        for kh in range(SWA_KV_HEADS):
            kv = slice(kh * LANES, (kh + 1) * LANES)
            ta = slice(2 * kh * LANES, (2 * kh + 1) * LANES)
            tb = slice((2 * kh + 1) * LANES, (2 * kh + 2) * LANES)
            sinks = [sink_ref[SWA_GROUP * kh + e] for e in range(SWA_GROUP)]
            parts = [(kd_ref[span, kv], vd_ref[span, kv], bias), (kx_ref[:, kv], vx_ref[:, kv], None)]
            oa, ob = _swa_group_attend(q_ref[rows, ta], q_ref[rows, tb], parts, sinks)
            o_ref[rows, ta] = (oa * gs_ref[rows, ta].astype(F32)).astype(BF16)
            o_ref[rows, tb] = (ob * gs_ref[rows, tb].astype(F32)).astype(BF16)
        return carry

    lax.fori_loop(0, nsub, block, 0, unroll=True)


def _swa_band_call(sink, q, kd, vd, cache_k, cache_v, slot, gs):
    tq = SWA_Q_TILE
    nq = DEC_SEQ // tq
    qrow = lambda b, i: (b * nq + i, 0)
    brow = lambda b, i: (b, 0)
    cache = pl.BlockSpec((1, 1, PAST_LEN, SWA_KV_W), lambda b, i: (b, slot, 0, 0))
    return pl.pallas_call(
        _swa_band_body,
        grid=(DEC_BATCH, nq),
        in_specs=[
            pl.BlockSpec(memory_space=pltpu.SMEM),
            pl.BlockSpec((tq, SWA_WIDTH), qrow),
            pl.BlockSpec((DEC_SEQ, SWA_KV_DUP_W), brow),
            pl.BlockSpec((DEC_SEQ, SWA_KV_DUP_W), brow),
            cache,
            cache,
            pl.BlockSpec((tq, SWA_WIDTH), qrow),
        ],
        out_specs=pl.BlockSpec((tq, SWA_WIDTH), qrow),
        out_shape=jax.ShapeDtypeStruct((DEC_BATCH * DEC_SEQ, SWA_WIDTH), BF16),
        scratch_shapes=[pltpu.VMEM((PAST_LEN, SWA_KV_DUP_W), BF16)] * 2,
        compiler_params=_params2(),
        name="swa_attn_latent",
    )(sink, q, kd, vd, cache_k, cache_v, gs)


def _rope_tables(rope_dims, lane_period, lane_off):
    n = rope_dims // 2
    hf = n // 2
    t = jnp.arange(DEC_SEQ)
    pos = jnp.stack([(t // GRID_W).astype(F32), (t % GRID_W).astype(F32)], axis=0)
    inv = ROPE_BASE ** (-jnp.arange(0, n, 2, dtype=F32) / n)
    ang = pos[:, :, None] * inv[None, None, :]
    cos = jnp.cos(ang)
    sin = jnp.sin(ang)
    cos_d = jnp.concatenate([cos, cos], axis=-1)
    sin_d = jnp.concatenate([-sin, sin], axis=-1)
    cos_r = jnp.concatenate([cos_d[0], cos_d[1]], axis=-1)
    sin_r = jnp.concatenate([sin_d[0], sin_d[1]], axis=-1)
    reps = LANES // lane_period
    pad = ((0, 0), (lane_off, lane_period - lane_off - rope_dims))
    cos_p = jnp.tile(jnp.pad(cos_r, pad, constant_values=1.0), (1, reps))
    sin_p = jnp.tile(jnp.pad(sin_r, pad), (1, reps))
    return cos_p, sin_p


def _prep_mla(i, mla_w_in, mla_g_qn, mla_g_kvn, mla_w_uq, mla_w_ukv, pool_w, pool_scale, mixa_w_out):
    w_in = mla_w_in[i]
    i1 = MLA_Q_RANK
    i2 = i1 + MLA_KV_RANK
    i3 = i2 + MLA_ROPE
    kr = jnp.pad(w_in[:, i2:i3], ((0, 0), (MLA_NOPE, LANES - MLA_NOPE - MLA_ROPE)))
    w_in_r = jnp.concatenate([w_in[:, :i2], kr, w_in[:, i3:]], axis=1).astype(BF16)
    dq = MLA_NOPE + MLA_ROPE
    w_uq = jnp.pad(mla_w_uq[i].reshape(MLA_Q_RANK, MLA_HEADS, dq),
                   ((0, 0), (0, 0), (0, MLA_HEAD_PAD - dq))).reshape(MLA_Q_RANK, MLA_QK_W).astype(BF16)
    ukv = mla_w_ukv[i].reshape(MLA_KV_RANK, MLA_HEADS, MLA_NOPE + MLA_V)
    w_uk = jnp.pad(ukv[:, :, :MLA_NOPE], ((0, 0), (0, 0), (0, MLA_HEAD_PAD - MLA_NOPE)))
    w_uk = w_uk.reshape(MLA_KV_RANK, MLA_QK_W).astype(BF16)
    w_uv = ukv[:, :, MLA_NOPE:].reshape(MLA_KV_RANK, MLA_WIDTH).astype(BF16)
    return {
        "w_in": w_in_r, "w_uq": w_uq, "w_uk": w_uk, "w_uv": w_uv,
        "g_qn": mla_g_qn[i].reshape(1, -1), "g_kvn": mla_g_kvn[i].reshape(1, -1),
        "w_pool": pool_w[i].astype(BF16), "s_pool": pool_scale[i].reshape(1, -1),
        "w_out": mixa_w_out[i].astype(BF16),
    }


def kernel(x_prompt, x_sample, cache_ckv, cache_krope, cache_k, cache_v, c, c_ctx, ada_w, ada_b, norm_pre, norm_post, mla_w_in, mla_g_qn, mla_g_kvn, mla_w_uq, mla_w_ukv, pool_w, pool_scale, mixa_w_out, swa_w_in, swa_sink, swa_w_out):
    n_p = BATCH * SEQ
    n_s = DEC_BATCH * DEC_SEQ
    xp = x_prompt.reshape(n_p, D_MODEL)
    xs = x_sample.reshape(n_s, D_MODEL)
    cvec = jnp.concatenate(
        [c_ctx[None], c, jnp.zeros((MOD_ROWS - 1 - DEC_BATCH, D_MODEL), F32)], axis=0)
    mods = _ada_call(cvec, ada_w, ada_b).reshape(DEPTH, MOD_ROWS, 3, D_MODEL)
    mla_tabs = _rope_tables(MLA_ROPE, LANES, MLA_NOPE)
    swa_tabs = _rope_tables(SWA_HEAD_DIM, SWA_HEAD_DIM, 0)
    ck_view = cache_k.reshape(DEC_BATCH, DEPTH // 2, PAST_LEN, SWA_KV_W)
    cv_view = cache_v.reshape(DEC_BATCH, DEPTH // 2, PAST_LEN, SWA_KV_W)
    mla_state = swa_state = ()
    pend_p = pend_s = None

    def stage(x, latent, out_part, in_part):
        res = list(_stage_call(x, latent, out_part, in_part))
        if out_part is not None:
            return res[0], res[1:]
        return x, res

    for l in range(DEPTH):
        i = l // 2
        mod = mods[l]
        gpre = norm_pre[l].reshape(1, D_MODEL)
        gpost = norm_post[l].reshape(1, D_MODEL)
        if l % 2 == 0:
            w = _prep_mla(i, mla_w_in, mla_g_qn, mla_g_kvn, mla_w_uq, mla_w_ukv, pool_w, pool_scale,
                          mixa_w_out)
            xp, (qp, kcp, vp, gap, vpp, gbp, *mla_state) = stage(
                xp, False, pend_p, _mla_in_part(n_p, mod, gpre, w, False, None, i, mla_state))
            xs, (qs, kcs, vs, gas, vps, gbs) = stage(
                xs, True, pend_s, _mla_in_part(n_s, mod, gpre, w, True, mla_tabs))
            ctx_kr = jnp.pad(cache_krope[:, i].reshape(DEC_BATCH * PAST_LEN, MLA_ROPE),
                             ((0, 0), (MLA_NOPE, LANES - MLA_NOPE - MLA_ROPE)))
            ctx = (cache_ckv, i, ctx_kr, w["w_uk"], w["w_uv"])
            ap = _mla_attn_call(qp, kcp, vp, gap, BATCH, SEQ)
            as_ = _mla_attn_call(qs, kcs, vs, gas, DEC_BATCH, DEC_SEQ, ctx)
            pend_p = _mla_out_part(n_p, ap, vpp, gbp, mod, gpost, w, False)
            pend_s = _mla_out_part(n_s, as_, vps, gbs, mod, gpost, w, True)
        else:
            w_in = swa_w_in[i].astype(BF16)
            w_out = swa_w_out[i].astype(BF16)
            sink = swa_sink[i]
            xp, (qp, kdp, vdp, gsp, *swa_state) = stage(
                xp, False, pend_p, _swa_in_part(n_p, mod, gpre, w_in, False, None, i, swa_state))
            xs, (qs, kds, vds, gss) = stage(
                xs, True, pend_s, _swa_in_part(n_s, mod, gpre, w_in, True, swa_tabs))

            ap = _swa_dense_call(sink, qp, kdp, vdp, gsp)
            as_ = _swa_band_call(sink, qs, kds, vds, ck_view, cv_view, i, gss)
            pend_p = _swa_out_part(ap, mod, gpost, w_out, False)
            pend_s = _swa_out_part(as_, mod, gpost, w_out, True)
    xp, _ = stage(xp, False, pend_p, None)
    xs, _ = stage(xs, True, pend_s, None)
    return (
        xp.reshape(BATCH, SEQ, D_MODEL),
        xs.reshape(DEC_BATCH, DEC_SEQ, D_MODEL),
        mla_state[0].reshape(BATCH, DEPTH // 2, SEQ, MLA_KV_RANK),
        mla_state[1].reshape(BATCH, DEPTH // 2, SEQ, MLA_ROPE),
        swa_state[0].reshape(BATCH, DEPTH // 2, SEQ, SWA_KV_HEADS, SWA_HEAD_DIM),
        swa_state[1].reshape(BATCH, DEPTH // 2, SEQ, SWA_KV_HEADS, SWA_HEAD_DIM),
    )
```

```python
import functools

import numpy as np
import jax
import jax.numpy as jnp
from jax import lax
from jax.experimental import pallas as pl
from jax.experimental.pallas import tpu as pltpu

F32 = jnp.float32
BF16 = jnp.bfloat16

D_MODEL = 1024
BATCH = 32
SEQ = 256
DEPTH = 4
DEC_BATCH = 8
DEC_SEQ = 2048
PAST_LEN = 256
GRID_W = 64
ROPE_BASE = 10000.0
EPS = 1e-6
NEG_INF = -1e30

MLA_HEADS = 8
MLA_NOPE = 64
MLA_ROPE = 32
MLA_V = 64
MLA_Q_RANK = 384
MLA_KV_RANK = 256
MLA_WIDTH = MLA_HEADS * MLA_V
MLA_SCALE = (MLA_NOPE + MLA_ROPE) ** -0.5
POOL_WINDOWS = (2, 4, 8, 16)
POOL_GROUPS = 4
POOL_GROUP_W = D_MODEL // 8
POOL_WIDTH = POOL_GROUPS * POOL_GROUP_W
SWA_HEADS = 16
SWA_KV_HEADS = 4
SWA_HEAD_DIM = 64
SWA_WINDOW = 128
SWA_WIDTH = SWA_HEADS * SWA_HEAD_DIM
SWA_KV_W = SWA_KV_HEADS * SWA_HEAD_DIM
SWA_SCALE = SWA_HEAD_DIM ** -0.5

LANES = 128
HALF = LANES // 2
BF16_SUBLANES = 16
MOD_ROWS = 16
TOK_TILE = 1024
LOG2E = 1.4426950408889634
POOL_BLOCK = 128
POOL_HALO = 64
VMEM_LIMIT = 60 * 1024 * 1024

_CQ = (0, MLA_Q_RANK)
_CKV = (_CQ[1], _CQ[1] + MLA_KV_RANK)
_KR = (_CKV[1], _CKV[1] + LANES)
_GA = (_KR[1], _KR[1] + MLA_WIDTH)
_VP = (_GA[1], _GA[1] + POOL_WIDTH)
_GB = (_VP[1], _VP[1] + POOL_WIDTH)
MLA_IN_COLS = _GB[1]
MLA_HEAD_PAD = LANES
MLA_QK_W = MLA_HEADS * MLA_HEAD_PAD
MLA_Q_TILE = 1024
CTX_SEQS_PER_STEP = 4


def _params():
    return pltpu.CompilerParams(dimension_semantics=("arbitrary",), vmem_limit_bytes=VMEM_LIMIT)


def _params2():
    return pltpu.CompilerParams(dimension_semantics=("arbitrary", "arbitrary"), vmem_limit_bytes=VMEM_LIMIT)


def _dot(a, b):
    return jnp.dot(a, b, preferred_element_type=F32)


def _dot_nt(a, b):
    return lax.dot_general(a, b, (((1,), (1,)), ((), ())), preferred_element_type=F32)


def _silu(x):
    return x / (1.0 + jnp.exp(-x))


def _rms(x):
    return x * lax.rsqrt(jnp.mean(x * x, axis=-1, keepdims=True) + EPS)


def _lane_iota(shape):
    return lax.broadcasted_iota(jnp.int32, shape, len(shape) - 1)


def _full(shape):
    nd = len(shape)
    return pl.BlockSpec(shape, lambda *_: (0,) * nd, pipeline_mode=pl.Buffered(1))


def _ada_body(c_ref, w_ref, b_ref, o_ref):
    s = _silu(c_ref[...]).astype(BF16)
    o_ref[0] = _dot(s, w_ref[0].astype(BF16)) + b_ref[0]


def _ada_call(cvec, ada_w, ada_b):
    tn = 768
    return pl.pallas_call(
        _ada_body,
        grid=(DEPTH, 3 * D_MODEL // tn),
        in_specs=[
            pl.BlockSpec((MOD_ROWS, D_MODEL), lambda l, j: (0, 0)),
            pl.BlockSpec((1, D_MODEL, tn), lambda l, j: (l, 0, j)),
            pl.BlockSpec((1, 1, tn), lambda l, j: (l, 0, j)),
        ],
        out_specs=pl.BlockSpec((1, MOD_ROWS, tn), lambda l, j: (l, 0, j)),
        out_shape=jax.ShapeDtypeStruct((DEPTH, MOD_ROWS, 3 * D_MODEL), F32),
        compiler_params=_params2(),
        name="ada_mod",
    )(cvec, ada_w, ada_b.reshape(DEPTH, 1, 3 * D_MODEL))


def _modulated(x, mod_ref, g_ref):
    m = mod_ref[0]
    gain = g_ref[...] * (1.0 + m[1:2, :])
    return (_rms(x) * gain + m[0:1, :]).astype(BF16)


def _rope_block(b, cos, sin, partner):
    first = (_lane_iota(b.shape) & partner) == 0
    sw = jnp.where(first, pltpu.roll(b, LANES - partner, 1), pltpu.roll(b, partner, 1))
    return b * cos + sw * sin


def _emit_states(slot, values, prev_refs, out_refs):
    for idx, (val, out_ref) in enumerate(zip(values, out_refs)):
        if slot == 0:
            out_ref[...] = val
        else:
            prev_ref = prev_refs[idx]
            nb = val.shape[0] // SEQ
            out_ref[:, 0:SEQ, :] = prev_ref[...].reshape(nb, SEQ, val.shape[1])
            out_ref[:, SEQ:2 * SEQ, :] = val.reshape(nb, SEQ, val.shape[1])


def _state_specs(n, widths, slot, prev):
    tm = TOK_TILE
    if slot == 0:
        specs = [pl.BlockSpec((tm, c), lambda i: (i, 0)) for c in widths]
        return [], [], specs, [jax.ShapeDtypeStruct((n, c), F32) for c in widths]
    nb = tm // SEQ
    ins = [pl.BlockSpec((tm, c), lambda i: (i, 0)) for c in widths]
    outs = [pl.BlockSpec((nb, 2 * SEQ, c), lambda i: (i, 0, 0)) for c in widths]
    return ins, list(prev), outs, [jax.ShapeDtypeStruct((n // SEQ, 2 * SEQ, c), F32) for c in widths]


def _mod_row_fn(latent, tile):
    if latent:
        per_seq = DEC_SEQ // tile
        return lambda i: (1 + i // per_seq, 0, 0)
    return lambda i: (0, 0, 0)


def _mla_in_compute(x, ins, refs, *, rope, state):
    mod_ref, g_ref, win_ref, gq_ref, gkv_ref, wuq_ref, wuk_ref, wuv_ref = ins[:8]
    if rope:
        cos_ref, sin_ref = ins[8:10]
    q_ref, kc_ref, v_ref, ga_ref, vp_ref, gb_ref = refs[:6]
    h = _modulated(x, mod_ref, g_ref)

    def proj(c):
        return _dot(h, win_ref[:, c[0]:c[1]])

    cqn = (_rms(proj(_CQ)) * gq_ref[...]).astype(BF16)
    q = _dot(cqn, wuq_ref[...]) * (MLA_SCALE * LOG2E)
    ckvn = _rms(proj(_CKV)) * gkv_ref[...]
    ckvb = ckvn.astype(BF16)
    kr = proj(_KR)
    if state is not None:
        kr_state = pltpu.roll(kr, LANES - MLA_NOPE, 1)[:, :MLA_ROPE]
        _emit_states(state, (ckvn, kr_state), ins[8:10], refs[6:8])
    if rope:
        cos = cos_ref[...]
        sin = sin_ref[...]
        kr = _rope_block(kr, cos, sin, MLA_ROPE // 4)
    kc = _dot(ckvb, wuk_ref[...])
    for hh in range(MLA_HEADS):
        blk = slice(hh * MLA_HEAD_PAD, (hh + 1) * MLA_HEAD_PAD)
        qb = q[:, blk]
        if rope:
            qb = _rope_block(qb, cos, sin, MLA_ROPE // 4)
        q_ref[:, blk] = qb.astype(BF16)
        kc_ref[:, blk] = (kc[:, blk] + kr).astype(BF16)
    v_ref[...] = _dot(ckvb, wuv_ref[...]).astype(BF16)
    ga_ref[...] = _silu(proj(_GA)).astype(BF16)
    vp_ref[...] = proj(_VP).astype(BF16)
    gb_ref[...] = _silu(proj(_GB)).astype(BF16)


class _Part:
    def __init__(self, name, compute, in_specs, args, out_specs, out_shape):
        self.name = name
        self.compute = compute
        self.in_specs = in_specs
        self.args = args
        self.out_specs = out_specs
        self.out_shape = out_shape


def _mla_in_part(n, mod, gpre, w, latent, rope_tabs, slot=None, prev=()):
    tm = TOK_TILE
    row = lambda i: (i, 0)
    in_specs = [
        pl.BlockSpec((1, 3, D_MODEL), _mod_row_fn(latent, tm)),
        _full((1, D_MODEL)),
        _full((D_MODEL, MLA_IN_COLS)),
        _full((1, MLA_Q_RANK)),
        _full((1, MLA_KV_RANK)),
        _full((MLA_Q_RANK, MLA_QK_W)),
        _full((MLA_KV_RANK, MLA_QK_W)),
        _full((MLA_KV_RANK, MLA_WIDTH)),
    ]
    args = [mod, gpre, w["w_in"], w["g_qn"], w["g_kvn"], w["w_uq"], w["w_uk"], w["w_uv"]]
    if latent:
        per_seq = DEC_SEQ // tm
        in_specs += [pl.BlockSpec((tm, LANES), lambda i: (i % per_seq, 0))] * 2
        args += list(rope_tabs)
    widths = [MLA_QK_W, MLA_QK_W, MLA_WIDTH, MLA_WIDTH, POOL_WIDTH, POOL_WIDTH]
    out_shape = [jax.ShapeDtypeStruct((n, c), BF16) for c in widths]
    out_specs = [pl.BlockSpec((tm, c), row) for c in widths]
    if not latent:
        s_in, s_args, s_out, s_shape = _state_specs(n, (MLA_KV_RANK, MLA_ROPE), slot, prev)
        in_specs += s_in
        args += s_args
        out_specs += s_out
        out_shape += s_shape
    compute = functools.partial(_mla_in_compute, rope=latent, state=slot)
    return _Part("mla_in", compute, in_specs, args, out_specs, out_shape)


def _stage_body(*refs, out_part, in_part):
    x_ref = refs[0]
    n_out_in = len(out_part.in_specs) if out_part else 0
    n_in_in = len(in_part.in_specs) if in_part else 0
    out_ins = refs[1:1 + n_out_in]
    in_ins = refs[1 + n_out_in:1 + n_out_in + n_in_in]
    outs = refs[1 + n_out_in + n_in_in:]
    x = x_ref[...]
    if out_part:
        x = out_part.compute(x, out_ins)
        outs[0][...] = x
        outs = outs[1:]
    if in_part:
        in_part.compute(x, in_ins, outs)


def _stage_call(x, latent, out_part, in_part):
    n = x.shape[0]
    tm = TOK_TILE
    row = lambda i: (i, 0)
    in_specs = [pl.BlockSpec((tm, D_MODEL), row)]
    args = [x]
    out_specs, out_shape = [], []
    names = []
    if out_part:
        in_specs += out_part.in_specs
        args += out_part.args
        out_specs.append(pl.BlockSpec((tm, D_MODEL), row))
        out_shape.append(jax.ShapeDtypeStruct((n, D_MODEL), F32))
        names.append(out_part.name)
    if in_part:
        in_specs += in_part.in_specs
        args += in_part.args
        out_specs += in_part.out_specs
        out_shape += in_part.out_shape
        names.append(in_part.name)
    names.append("latent" if latent else "context")
    return pl.pallas_call(
        functools.partial(_stage_body, out_part=out_part, in_part=in_part),
        grid=(n // tm,),
        in_specs=in_specs,
        out_specs=out_specs,
        out_shape=out_shape,
        compiler_params=_params(),
        name="_".join(names),
    )(*args)


def _mla_attn_body(*refs, has_ctx, nseq):
    if has_ctx:
        q_ref, kc_ref, v_ref, ckv_ref, kr_ref, wuk_ref, wuv_ref, ga_ref, o_ref, va_ref, kcc_ref, vca_ref = refs
    else:
        q_ref, kc_ref, v_ref, ga_ref, o_ref, va_ref = refs

    def augment(v, dst):
        for j in range(MLA_HEADS // 2):
            dst[:, 2 * j * LANES:(2 * j + 1) * LANES] = v[:, j * LANES:(j + 1) * LANES]
            dst[:, (2 * j + 1) * LANES:(2 * j + 2) * LANES] = jnp.ones((v.shape[0], LANES), BF16)

    @pl.when(pl.program_id(1) == 0)
    def _():
        augment(v_ref[...], va_ref)
        if has_ctx:
            ckvb = ckv_ref[0, 0].astype(BF16)
            kr = kr_ref[...]
            kc = _dot(ckvb, wuk_ref[...])
            for hh in range(MLA_HEADS):
                blk = slice(hh * MLA_HEAD_PAD, (hh + 1) * MLA_HEAD_PAD)
                kcc_ref[:, blk] = (kc[:, blk] + kr).astype(BF16)
            augment(_dot(ckvb, wuv_ref[...]).astype(BF16), vca_ref)

    def lane_tile_max(t, s):
        for cb in range(s.shape[1] // LANES):
            blk = s[:, cb * LANES:(cb + 1) * LANES]
            t = blk if t is None else jnp.maximum(t, blk)
        return t

    qrows = q_ref.shape[0] // nseq
    krows = kc_ref.shape[0] // nseq
    low = _lane_iota((qrows, LANES)) < HALF
    for si in range(nseq):
        qr = slice(si * qrows, (si + 1) * qrows)
        kr_ = slice(si * krows, (si + 1) * krows)
        for j in range(MLA_HEADS // 2):
            pair = slice(j * LANES, (j + 1) * LANES)
            vaug = slice(2 * j * LANES, (2 * j + 2) * LANES)
            outs = []
            for hh in (2 * j, 2 * j + 1):
                blk = slice(hh * MLA_HEAD_PAD, (hh + 1) * MLA_HEAD_PAD)
                qh = q_ref[qr, blk]
                s = _dot_nt(qh, kc_ref[kr_, blk])
                t = lane_tile_max(None, s)
                if has_ctx:
                    sc = _dot_nt(qh, kcc_ref[:, blk])
                    t = lane_tile_max(t, sc)
                m = jnp.max(t, axis=-1, keepdims=True)
                acc = _dot(jnp.exp2(s - m).astype(BF16), va_ref[kr_, vaug])
                if has_ctx:
                    acc = acc + _dot(jnp.exp2(sc - m).astype(BF16), vca_ref[:, vaug])
                outs.append(acc[:, :LANES] / acc[:, LANES:])
            o_pair = jnp.where(low, outs[0], outs[1])
            o_ref[qr, pair] = (o_pair * ga_ref[qr, pair].astype(F32)).astype(BF16)


def _mla_attn_call(q, kc, v, ga, nbatch, seq, ctx=None):
    nseq = 1 if ctx is not None else CTX_SEQS_PER_STEP
    tq = min(seq, MLA_Q_TILE) * nseq
    seq = seq * nseq
    nbatch = nbatch // nseq
    scratch = [pltpu.VMEM((seq, 2 * MLA_WIDTH), BF16)]
    if ctx is not None:
        scratch += [pltpu.VMEM((PAST_LEN, MLA_QK_W), BF16), pltpu.VMEM((PAST_LEN, 2 * MLA_WIDTH), BF16)]
    nq = seq // tq
    qrow = lambda b, i: (b * nq + i, 0)
    brow = lambda b, i: (b, 0)
    in_specs = [
        pl.BlockSpec((tq, MLA_QK_W), qrow),
        pl.BlockSpec((seq, MLA_QK_W), brow),
        pl.BlockSpec((seq, MLA_WIDTH), brow),
    ]
    args = [q, kc, v]
    if ctx is not None:
        cache, slot, ctx_kr, w_uk, w_uv = ctx
        in_specs += [
            pl.BlockSpec((1, 1, PAST_LEN, MLA_KV_RANK), lambda b, i: (b, slot, 0, 0)),
            pl.BlockSpec((PAST_LEN, LANES), brow),
            _full((MLA_KV_RANK, MLA_QK_W)),
            _full((MLA_KV_RANK, MLA_WIDTH)),
        ]
        args += [cache, ctx_kr, w_uk, w_uv]
    in_specs.append(pl.BlockSpec((tq, MLA_WIDTH), qrow))
    args.append(ga)
    return pl.pallas_call(
        functools.partial(_mla_attn_body, has_ctx=ctx is not None, nseq=nseq),
        grid=(nbatch, nq),
        in_specs=in_specs,
        out_specs=pl.BlockSpec((tq, MLA_WIDTH), qrow),
        out_shape=jax.ShapeDtypeStruct((nbatch * seq, MLA_WIDTH), BF16),
        scratch_shapes=scratch,
        compiler_params=_params2(),
        name="mla_attn_latent" if ctx is not None else "mla_attn_context",
    )(*args)


def _post_residual(out, x, mod_ref, gpost_ref):
    m = mod_ref[0]
    return x + _rms(out) * (m[2:3, :] * gpost_ref[...])


def _mla_out_compute(x, ins, *, seq):
    a_ref, vp_ref, vprev_ref, vnext_ref, gb_ref, pm_ref, wp_ref, sp_ref, wo_ref, mod_ref, gpost_ref = ins
    tm = vp_ref.shape[0]
    cur = vp_ref[...]
    slab = jnp.concatenate([vprev_ref[...], cur, vnext_ref[...]], axis=0)
    srow = lax.broadcasted_iota(jnp.int32, (POOL_BLOCK + 2 * POOL_HALO, POOL_WIDTH), 0)
    trow = lax.broadcasted_iota(jnp.int32, (POOL_BLOCK, LANES), 0)
    zero = jnp.zeros((POOL_BLOCK + 2 * POOL_HALO, POOL_WIDTH), BF16)
    pooled = [[] for _ in POOL_WINDOWS]
    for sb in range(tm // POOL_BLOCK):
        pos0 = (pl.program_id(0) * tm + sb * POOL_BLOCK) % seq
        lo_cut = jnp.where(pos0 == 0, POOL_HALO, 0)
        hi_cut = jnp.where(pos0 == seq - POOL_BLOCK, POOL_BLOCK + POOL_HALO, POOL_BLOCK + 2 * POOL_HALO)
        chunk = slab[sb * POOL_BLOCK:(sb + 1) * POOL_BLOCK + 2 * POOL_HALO]
        chunk = jnp.where(srow >= lo_cut, chunk, zero)
        chunk = jnp.where(srow < hi_cut, chunk, zero)
        t = pos0 + trow
        for g, w in enumerate(POOL_WINDOWS):
            cols = slice(g * POOL_GROUP_W, (g + 1) * POOL_GROUP_W)
            tot = _dot(pm_ref[g], chunk[:, cols])
            cnt = jnp.minimum(t + w // 2, seq) - jnp.maximum(t - w // 2, 0)
            own = cur[sb * POOL_BLOCK:(sb + 1) * POOL_BLOCK, cols].astype(F32)
            pooled[g].append((tot / cnt.astype(F32) - own).astype(BF16))
    bs = []
    for g in range(POOL_GROUPS):
        cols = slice(g * POOL_GROUP_W, (g + 1) * POOL_GROUP_W)
        y = _dot(jnp.concatenate(pooled[g], axis=0), wp_ref[g]) * sp_ref[:, cols]
        bs.append((y * gb_ref[:, cols].astype(F32)).astype(BF16))
    b = jnp.concatenate(bs, axis=1)
    out = _dot(a_ref[...], wo_ref[0:MLA_WIDTH, :]) + _dot(b, wo_ref[MLA_WIDTH:, :])
    return _post_residual(out, x, mod_ref, gpost_ref)


def _pool_masks():
    r = np.arange(POOL_BLOCK)[:, None]
    c = np.arange(POOL_BLOCK + 2 * POOL_HALO)[None, :]
    rel = c - POOL_HALO - r
    ms = [((rel >= -(w // 2)) & (rel <= w // 2 - 1)) for w in POOL_WINDOWS]
    return jnp.asarray(np.stack(ms).astype(np.float32), dtype=BF16)


def _mla_out_part(n, a, vp, gb, mod, gpost, w, latent):
    tm = TOK_TILE
    seq = DEC_SEQ if latent else SEQ
    row = lambda i: (i, 0)
    hb = tm // POOL_HALO
    nhb = n // POOL_HALO
    in_specs = [
        pl.BlockSpec((tm, MLA_WIDTH), row),
        pl.BlockSpec((tm, POOL_WIDTH), row),
        pl.BlockSpec((POOL_HALO, POOL_WIDTH), lambda i: (jnp.maximum(i * hb - 1, 0), 0)),
        pl.BlockSpec((POOL_HALO, POOL_WIDTH), lambda i: (jnp.minimum((i + 1) * hb, nhb - 1), 0)),
        pl.BlockSpec((tm, POOL_WIDTH), row),
        _full((POOL_GROUPS, POOL_BLOCK, POOL_BLOCK + 2 * POOL_HALO)),
        _full((POOL_GROUPS, POOL_GROUP_W, POOL_GROUP_W)),
        _full((1, POOL_WIDTH)),
        _full((MLA_WIDTH + POOL_WIDTH, D_MODEL)),
        pl.BlockSpec((1, 3, D_MODEL), _mod_row_fn(latent, tm)),
        _full((1, D_MODEL)),
    ]
    args = [a, vp, vp, vp, gb, _pool_masks(), w["w_pool"], w["s_pool"], w["w_out"], mod, gpost]
    return _Part("mla_out", functools.partial(_mla_out_compute, seq=seq), in_specs, args, None, None)


def _swa_out_compute(x, ins):
    a_ref, wo_ref, mod_ref, gpost_ref = ins
    return _post_residual(_dot(a_ref[...], wo_ref[...]), x, mod_ref, gpost_ref)


def _swa_out_part(a, mod, gpost, w_out, latent):
    tm = TOK_TILE
    in_specs = [
        pl.BlockSpec((tm, SWA_WIDTH), lambda i: (i, 0)),
        _full((SWA_WIDTH, D_MODEL)),
        pl.BlockSpec((1, 3, D_MODEL), _mod_row_fn(latent, tm)),
        _full((1, D_MODEL)),
    ]
    return _Part("swa_out", _swa_out_compute, in_specs, [a, w_out, mod, gpost], None, None)


_SQ = (0, SWA_WIDTH)
_SK = (_SQ[1], _SQ[1] + SWA_KV_W)
_SV = (_SK[1], _SK[1] + SWA_KV_W)
_SG = (_SV[1], _SV[1] + SWA_WIDTH)
SWA_IN_COLS = _SG[1]
SWA_KV_DUP_W = 2 * SWA_KV_W


def _dup_heads(x, o_ref, with_ones=False):
    step = 2 if with_ones else 1
    for b in range(SWA_KV_W // LANES):
        blk = x[:, b * LANES:(b + 1) * LANES]
        rot = pltpu.roll(blk, HALF, 1)
        low = _lane_iota(blk.shape) < HALF
        for e, tile in enumerate((jnp.where(low, blk, rot), jnp.where(low, rot, blk))):
            at = (2 * b + e) * step * LANES
            o_ref[:, at:at + LANES] = tile.astype(BF16)
            if with_ones:
                o_ref[:, at + LANES:at + 2 * LANES] = jnp.ones(blk.shape, BF16)


def _swa_in_compute(x, ins, refs, *, rope, state):
    mod_ref, g_ref, win_ref = ins[:3]
    if rope:
        cos_ref, sin_ref = ins[3:5]
    q_ref, kd_ref, vd_ref, gs_ref = refs[:4]
    h = _modulated(x, mod_ref, g_ref)

    def proj(c):
        return _dot(h, win_ref[:, c[0]:c[1]])

    q = proj(_SQ) * (SWA_SCALE * LOG2E)
    k = proj(_SK)
    v = proj(_SV)
    if state is not None:
        _emit_states(state, (k, v), ins[3:5], refs[4:6])
    if rope:
        cos = cos_ref[...]
        sin = sin_ref[...]
        for b in range(SWA_WIDTH // LANES):
            blk = slice(b * LANES, (b + 1) * LANES)
            q_ref[:, blk] = _rope_block(q[:, blk], cos, sin, SWA_HEAD_DIM // 4).astype(BF16)
        k = jnp.concatenate(
            [_rope_block(k[:, b * LANES:(b + 1) * LANES], cos, sin, SWA_HEAD_DIM // 4)
             for b in range(SWA_KV_W // LANES)], axis=1)
    else:
        q_ref[...] = q.astype(BF16)
    _dup_heads(k, kd_ref)
    _dup_heads(v, vd_ref)
    gs_ref[...] = _silu(proj(_SG)).astype(BF16)


def _swa_in_part(n, mod, gpre, w_in, latent, rope_tabs, slot=None, prev=()):
    tm = TOK_TILE
    row = lambda i: (i, 0)
    in_specs = [
        pl.BlockSpec((1, 3, D_MODEL), _mod_row_fn(latent, tm)),
        _full((1, D_MODEL)),
        _full((D_MODEL, SWA_IN_COLS)),
    ]
    args = [mod, gpre, w_in]
    if latent:
        per_seq = DEC_SEQ // tm
        in_specs += [pl.BlockSpec((tm, LANES), lambda i: (i % per_seq, 0))] * 2
        args += list(rope_tabs)
    widths = [SWA_WIDTH, SWA_KV_DUP_W, SWA_KV_DUP_W, SWA_WIDTH]
    out_shape = [jax.ShapeDtypeStruct((n, c), BF16) for c in widths]
    out_specs = [pl.BlockSpec((tm, c), row) for c in widths]
    if not latent:
        s_in, s_args, s_out, s_shape = _state_specs(n, (SWA_KV_W, SWA_KV_W), slot, prev)
        in_specs += s_in
        args += s_args
        out_specs += s_out
        out_shape += s_shape
    compute = functools.partial(_swa_in_compute, rope=latent, state=slot)
    return _Part("swa_in", compute, in_specs, args, out_specs, out_shape)


SWA_GROUP = SWA_HEADS // SWA_KV_HEADS


def _swa_group_attend(qa, qb, parts, sinks):
    rows = qa.shape[0]
    low = _lane_iota(qa.shape) < HALF
    zero = jnp.zeros_like(qa)
    qst = jnp.concatenate(
        [jnp.where(low, qa, zero), jnp.where(low, zero, qa), jnp.where(low, qb, zero), jnp.where(low, zero, qb)],
        axis=0)
    sink_t = jnp.concatenate([jnp.full((rows, LANES), s * LOG2E, F32) for s in sinks], axis=0)
    scores = []
    t = sink_t
    for k, _, bias in parts:
        s = _dot_nt(qst, k)
        if bias is not None:
            s = s + bias
        scores.append(s)
        for cb in range(s.shape[1] // LANES):
            t = jnp.maximum(t, s[:, cb * LANES:(cb + 1) * LANES])
    m = jnp.max(t, axis=-1, keepdims=True)
    acc = None
    for s, (_, v, _) in zip(scores, parts):
        pv = _dot(jnp.exp2(s - m).astype(BF16), v)
        acc = pv if acc is None else acc + pv
    o = acc[:, :LANES] / (acc[:, LANES:] + jnp.exp2(sink_t - m))
    return (jnp.where(low, o[0:rows], o[rows:2 * rows]),
            jnp.where(low, o[2 * rows:3 * rows], o[3 * rows:4 * rows]))


def _swa_dense_body(sink_ref, q_ref, kd_ref, vd_ref, gs_ref, o_ref):
    for si in range(q_ref.shape[0] // SEQ):
        rows = slice(si * SEQ, (si + 1) * SEQ)
        for kh in range(SWA_KV_HEADS):
            kv = slice(kh * LANES, (kh + 1) * LANES)
            ta = slice(2 * kh * LANES, (2 * kh + 1) * LANES)
            tb = slice((2 * kh + 1) * LANES, (2 * kh + 2) * LANES)
            sinks = [sink_ref[SWA_GROUP * kh + e] for e in range(SWA_GROUP)]
            v = vd_ref[rows, kv]
            parts = [(kd_ref[rows, kv], jnp.concatenate([v, jnp.ones_like(v)], axis=1), None)]
            oa, ob = _swa_group_attend(q_ref[rows, ta], q_ref[rows, tb], parts, sinks)
            o_ref[rows, ta] = (oa * gs_ref[rows, ta].astype(F32)).astype(BF16)
            o_ref[rows, tb] = (ob * gs_ref[rows, tb].astype(F32)).astype(BF16)


def _swa_dense_call(sink, q, kd, vd, gs):
    n = q.shape[0]
    tr = CTX_SEQS_PER_STEP * SEQ
    row = lambda b: (b, 0)
    return pl.pallas_call(
        _swa_dense_body,
        grid=(n // tr,),
        in_specs=[
            pl.BlockSpec(memory_space=pltpu.SMEM),
            pl.BlockSpec((tr, SWA_WIDTH), row),
            pl.BlockSpec((tr, SWA_KV_DUP_W), row),
            pl.BlockSpec((tr, SWA_KV_DUP_W), row),
            pl.BlockSpec((tr, SWA_WIDTH), row),
        ],
        out_specs=pl.BlockSpec((tr, SWA_WIDTH), row),
        out_shape=jax.ShapeDtypeStruct((n, SWA_WIDTH), BF16),
        compiler_params=_params(),
        name="swa_attn_context",
    )(sink, q, kd, vd, gs)


SWA_Q_TILE = 1024
SWA_SPAN = 3 * SWA_WINDOW


def _swa_band_body(sink_ref, q_ref, kd_ref, vd_ref, ck_ref, cv_ref, gs_ref, o_ref, kx_ref, vx_ref, va_ref):
    w = SWA_WINDOW
    nsub = q_ref.shape[0] // w
    step = pl.program_id(1)

    @pl.when(step == 0)
    def _():
        _dup_heads(ck_ref[0, 0], kx_ref)
        _dup_heads(cv_ref[0, 0], vx_ref, with_ones=True)
        for kh in range(SWA_KV_HEADS):
            va_ref[:, 2 * kh * LANES:(2 * kh + 1) * LANES] = vd_ref[:, kh * LANES:(kh + 1) * LANES]
            va_ref[:, (2 * kh + 1) * LANES:(2 * kh + 2) * LANES] = jnp.ones((vd_ref.shape[0], LANES), BF16)

    r = lax.broadcasted_iota(jnp.int32, (w, SWA_SPAN), 0)
    c = lax.broadcasted_iota(jnp.int32, (w, SWA_SPAN), 1)

    def block(sb, carry):
        qbase = (step * nsub + sb) * w
        start = pl.multiple_of(jnp.clip(qbase - w, 0, DEC_SEQ - SWA_SPAN), w)
        rel = c - r + (start - qbase)
        bias = jnp.where(jnp.abs(rel) <= w, 0.0, NEG_INF).astype(F32)
        bias = jnp.concatenate([bias] * SWA_GROUP, axis=0)
        rows = pl.ds(pl.multiple_of(sb * w, w), w)
        span = pl.ds(start, SWA_SPAN)
        for kh in range(SWA_KV_HEADS):
            kv = slice(kh * LANES, (kh + 1) * LANES)
            ta = slice(2 * kh * LANES, (2 * kh + 1) * LANES)
            tb = slice((2 * kh + 1) * LANES, (2 * kh + 2) * LANES)
            sinks = [sink_ref[SWA_GROUP * kh + e] for e in range(SWA_GROUP)]
            va = slice(2 * kh * LANES, (2 * kh + 2) * LANES)
            parts = [(kd_ref[span, kv], va_ref[span, va], bias), (kx_ref[:, kv], vx_ref[:, va], None)]
            oa, ob = _swa_group_attend(q_ref[rows, ta], q_ref[rows, tb], parts, sinks)
            o_ref[rows, ta] = (oa * gs_ref[rows, ta].astype(F32)).astype(BF16)
            o_ref[rows, tb] = (ob * gs_ref[rows, tb].astype(F32)).astype(BF16)
        return carry

    lax.fori_loop(0, nsub, block, 0, unroll=True)


def _swa_band_call(sink, q, kd, vd, cache_k, cache_v, slot, gs):
    tq = SWA_Q_TILE
    nq = DEC_SEQ // tq
    qrow = lambda b, i: (b * nq + i, 0)
    brow = lambda b, i: (b, 0)
    cache = pl.BlockSpec((1, 1, PAST_LEN, SWA_KV_W), lambda b, i: (b, slot, 0, 0))
    return pl.pallas_call(
        _swa_band_body,
        grid=(DEC_BATCH, nq),
        in_specs=[
            pl.BlockSpec(memory_space=pltpu.SMEM),
            pl.BlockSpec((tq, SWA_WIDTH), qrow),
            pl.BlockSpec((DEC_SEQ, SWA_KV_DUP_W), brow),
            pl.BlockSpec((DEC_SEQ, SWA_KV_DUP_W), brow),
            cache,
            cache,
            pl.BlockSpec((tq, SWA_WIDTH), qrow),
        ],
        out_specs=pl.BlockSpec((tq, SWA_WIDTH), qrow),
        out_shape=jax.ShapeDtypeStruct((DEC_BATCH * DEC_SEQ, SWA_WIDTH), BF16),
        scratch_shapes=[
            pltpu.VMEM((PAST_LEN, SWA_KV_DUP_W), BF16),
            pltpu.VMEM((PAST_LEN, 2 * SWA_KV_DUP_W), BF16),
            pltpu.VMEM((DEC_SEQ, 2 * SWA_KV_DUP_W), BF16),
        ],
        compiler_params=_params2(),
        name="swa_attn_latent",
    )(sink, q, kd, vd, cache_k, cache_v, gs)


def _rope_tables(rope_dims, lane_period, lane_off):
    n = rope_dims // 2
    hf = n // 2
    f32 = np.float32
    t = np.arange(DEC_SEQ)
    pos = np.stack([t // GRID_W, t % GRID_W], axis=0).astype(f32)
    inv = np.power(f32(ROPE_BASE), -np.arange(0, n, 2, dtype=f32) / f32(n)).astype(f32)
    ang = (pos[:, :, None] * inv[None, None, :]).astype(f32)
    cos = np.cos(ang).astype(f32)
    sin = np.sin(ang).astype(f32)
    cos_d = np.concatenate([cos, cos], axis=-1)
    sin_d = np.concatenate([-sin, sin], axis=-1)
    cos_r = np.concatenate([cos_d[0], cos_d[1]], axis=-1)
    sin_r = np.concatenate([sin_d[0], sin_d[1]], axis=-1)
    reps = LANES // lane_period
    pad = ((0, 0), (lane_off, lane_period - lane_off - rope_dims))
    cos_p = np.tile(np.pad(cos_r, pad, constant_values=1.0), (1, reps))
    sin_p = np.tile(np.pad(sin_r, pad), (1, reps))
    return jnp.asarray(cos_p), jnp.asarray(sin_p)


def _prep_mla(i, mla_w_in, mla_g_qn, mla_g_kvn, mla_w_uq, mla_w_ukv, pool_w, pool_scale, mixa_w_out):
    w_in = mla_w_in[i]
    i1 = MLA_Q_RANK
    i2 = i1 + MLA_KV_RANK
    i3 = i2 + MLA_ROPE
    kr = jnp.pad(w_in[:, i2:i3], ((0, 0), (MLA_NOPE, LANES - MLA_NOPE - MLA_ROPE)))
    w_in_r = jnp.concatenate([w_in[:, :i2], kr, w_in[:, i3:]], axis=1).astype(BF16)
    dq = MLA_NOPE + MLA_ROPE
    w_uq = jnp.pad(mla_w_uq[i].reshape(MLA_Q_RANK, MLA_HEADS, dq),
                   ((0, 0), (0, 0), (0, MLA_HEAD_PAD - dq))).reshape(MLA_Q_RANK, MLA_QK_W).astype(BF16)
    ukv = mla_w_ukv[i].reshape(MLA_KV_RANK, MLA_HEADS, MLA_NOPE + MLA_V)
    w_uk = jnp.pad(ukv[:, :, :MLA_NOPE], ((0, 0), (0, 0), (0, MLA_HEAD_PAD - MLA_NOPE)))
    w_uk = w_uk.reshape(MLA_KV_RANK, MLA_QK_W).astype(BF16)
    w_uv = ukv[:, :, MLA_NOPE:].reshape(MLA_KV_RANK, MLA_WIDTH).astype(BF16)
    return {
        "w_in": w_in_r, "w_uq": w_uq, "w_uk": w_uk, "w_uv": w_uv,
        "g_qn": mla_g_qn[i].reshape(1, -1), "g_kvn": mla_g_kvn[i].reshape(1, -1),
        "w_pool": pool_w[i].astype(BF16), "s_pool": pool_scale[i].reshape(1, -1),
        "w_out": mixa_w_out[i].astype(BF16),
    }


def kernel(x_prompt, x_sample, cache_ckv, cache_krope, cache_k, cache_v, c, c_ctx, ada_w, ada_b, norm_pre, norm_post, mla_w_in, mla_g_qn, mla_g_kvn, mla_w_uq, mla_w_ukv, pool_w, pool_scale, mixa_w_out, swa_w_in, swa_sink, swa_w_out):
    n_p = BATCH * SEQ
    n_s = DEC_BATCH * DEC_SEQ
    xp = x_prompt.reshape(n_p, D_MODEL)
    xs = x_sample.reshape(n_s, D_MODEL)
    cvec = jnp.concatenate(
        [c_ctx[None], c, jnp.zeros((MOD_ROWS - 1 - DEC_BATCH, D_MODEL), F32)], axis=0)
    mods = _ada_call(cvec, ada_w, ada_b).reshape(DEPTH, MOD_ROWS, 3, D_MODEL)
    mla_tabs = _rope_tables(MLA_ROPE, LANES, MLA_NOPE)
    swa_tabs = _rope_tables(SWA_HEAD_DIM, SWA_HEAD_DIM, 0)
    ck_view = cache_k.reshape(DEC_BATCH, DEPTH // 2, PAST_LEN, SWA_KV_W)
    cv_view = cache_v.reshape(DEC_BATCH, DEPTH // 2, PAST_LEN, SWA_KV_W)
    mla_state = swa_state = ()
    pend_p = pend_s = None

    def stage(x, latent, out_part, in_part):
        res = list(_stage_call(x, latent, out_part, in_part))
        if out_part is not None:
            return res[0], res[1:]
        return x, res

    for l in range(DEPTH):
        i = l // 2
        mod = mods[l]
        gpre = norm_pre[l].reshape(1, D_MODEL)
        gpost = norm_post[l].reshape(1, D_MODEL)
        if l % 2 == 0:
            w = _prep_mla(i, mla_w_in, mla_g_qn, mla_g_kvn, mla_w_uq, mla_w_ukv, pool_w, pool_scale,
                          mixa_w_out)
            xp, (qp, kcp, vp, gap, vpp, gbp, *mla_state) = stage(
                xp, False, pend_p, _mla_in_part(n_p, mod, gpre, w, False, None, i, mla_state))
            xs, (qs, kcs, vs, gas, vps, gbs) = stage(
                xs, True, pend_s, _mla_in_part(n_s, mod, gpre, w, True, mla_tabs))
            ctx_kr = jnp.pad(cache_krope[:, i].reshape(DEC_BATCH * PAST_LEN, MLA_ROPE),
                             ((0, 0), (MLA_NOPE, LANES - MLA_NOPE - MLA_ROPE)))
            ctx = (cache_ckv, i, ctx_kr, w["w_uk"], w["w_uv"])
            ap = _mla_attn_call(qp, kcp, vp, gap, BATCH, SEQ)
            as_ = _mla_attn_call(qs, kcs, vs, gas, DEC_BATCH, DEC_SEQ, ctx)
            pend_p = _mla_out_part(n_p, ap, vpp, gbp, mod, gpost, w, False)
            pend_s = _mla_out_part(n_s, as_, vps, gbs, mod, gpost, w, True)
        else:
            w_in = swa_w_in[i].astype(BF16)
            w_out = swa_w_out[i].astype(BF16)
            sink = swa_sink[i]
            xp, (qp, kdp, vdp, gsp, *swa_state) = stage(
                xp, False, pend_p, _swa_in_part(n_p, mod, gpre, w_in, False, None, i, swa_state))
            xs, (qs, kds, vds, gss) = stage(
                xs, True, pend_s, _swa_in_part(n_s, mod, gpre, w_in, True, swa_tabs))

            ap = _swa_dense_call(sink, qp, kdp, vdp, gsp)
            as_ = _swa_band_call(sink, qs, kds, vds, ck_view, cv_view, i, gss)
            pend_p = _swa_out_part(ap, mod, gpost, w_out, False)
            pend_s = _swa_out_part(as_, mod, gpost, w_out, True)
    xp, _ = stage(xp, False, pend_p, None)
    xs, _ = stage(xs, True, pend_s, None)
    return (
        xp.reshape(BATCH, SEQ, D_MODEL),
        xs.reshape(DEC_BATCH, DEC_SEQ, D_MODEL),
        mla_state[0].reshape(BATCH, DEPTH // 2, SEQ, MLA_KV_RANK),
        mla_state[1].reshape(BATCH, DEPTH // 2, SEQ, MLA_ROPE),
        swa_state[0].reshape(BATCH, DEPTH // 2, SEQ, SWA_KV_HEADS, SWA_HEAD_DIM),
        swa_state[1].reshape(BATCH, DEPTH // 2, SEQ, SWA_KV_HEADS, SWA_HEAD_DIM),
    )
```

```python
import functools

import numpy as np
import jax
import jax.numpy as jnp
from jax import lax
from jax.experimental import pallas as pl
from jax.experimental.pallas import tpu as pltpu

F32 = jnp.float32
BF16 = jnp.bfloat16

D_MODEL = 1024
BATCH = 32
SEQ = 256
DEPTH = 4
DEC_BATCH = 8
DEC_SEQ = 2048
PAST_LEN = 256
GRID_W = 64
ROPE_BASE = 10000.0
EPS = 1e-6
NEG_INF = -1e30

MLA_HEADS = 8
MLA_NOPE = 64
MLA_ROPE = 32
MLA_V = 64
MLA_Q_RANK = 384
MLA_KV_RANK = 256
MLA_WIDTH = MLA_HEADS * MLA_V
MLA_SCALE = (MLA_NOPE + MLA_ROPE) ** -0.5
POOL_WINDOWS = (2, 4, 8, 16)
POOL_GROUPS = 4
POOL_GROUP_W = D_MODEL // 8
POOL_WIDTH = POOL_GROUPS * POOL_GROUP_W
SWA_HEADS = 16
SWA_KV_HEADS = 4
SWA_HEAD_DIM = 64
SWA_WINDOW = 128
SWA_WIDTH = SWA_HEADS * SWA_HEAD_DIM
SWA_KV_W = SWA_KV_HEADS * SWA_HEAD_DIM
SWA_SCALE = SWA_HEAD_DIM ** -0.5

LANES = 128
HALF = LANES // 2
BF16_SUBLANES = 16
MOD_ROWS = 16
TOK_TILE = 1024
LOG2E = 1.4426950408889634
POOL_BLOCK = 128
POOL_HALO = 64
VMEM_LIMIT = 60 * 1024 * 1024

_CQ = (0, MLA_Q_RANK)
_CKV = (_CQ[1], _CQ[1] + MLA_KV_RANK)
_KR = (_CKV[1], _CKV[1] + LANES)
_GA = (_KR[1], _KR[1] + MLA_WIDTH)
_VP = (_GA[1], _GA[1] + POOL_WIDTH)
_GB = (_VP[1], _VP[1] + POOL_WIDTH)
MLA_IN_COLS = _GB[1]
MLA_HEAD_PAD = LANES
MLA_QK_W = MLA_HEADS * MLA_HEAD_PAD
MLA_Q_TILE = 1024
CTX_SEQS_PER_STEP = 8


def _params():
    return pltpu.CompilerParams(dimension_semantics=("arbitrary",), vmem_limit_bytes=VMEM_LIMIT)


def _params2():
    return pltpu.CompilerParams(dimension_semantics=("arbitrary", "arbitrary"), vmem_limit_bytes=VMEM_LIMIT)


def _dot(a, b):
    return jnp.dot(a, b, preferred_element_type=F32)


def _dot_nt(a, b):
    return lax.dot_general(a, b, (((1,), (1,)), ((), ())), preferred_element_type=F32)


def _silu(x):
    return x / (1.0 + jnp.exp(-x))


def _rms(x):
    return x * lax.rsqrt(jnp.mean(x * x, axis=-1, keepdims=True) + EPS)


def _lane_iota(shape):
    return lax.broadcasted_iota(jnp.int32, shape, len(shape) - 1)


def _full(shape):
    nd = len(shape)
    return pl.BlockSpec(shape, lambda *_: (0,) * nd, pipeline_mode=pl.Buffered(1))


def _ada_body(c_ref, w_ref, b_ref, o_ref):
    s = _silu(c_ref[...]).astype(BF16)
    o_ref[0] = _dot(s, w_ref[0].astype(BF16)) + b_ref[0]


def _ada_call(cvec, ada_w, ada_b):
    tn = 768
    return pl.pallas_call(
        _ada_body,
        grid=(DEPTH, 3 * D_MODEL // tn),
        in_specs=[
            pl.BlockSpec((MOD_ROWS, D_MODEL), lambda l, j: (0, 0)),
            pl.BlockSpec((1, D_MODEL, tn), lambda l, j: (l, 0, j)),
            pl.BlockSpec((1, 1, tn), lambda l, j: (l, 0, j)),
        ],
        out_specs=pl.BlockSpec((1, MOD_ROWS, tn), lambda l, j: (l, 0, j)),
        out_shape=jax.ShapeDtypeStruct((DEPTH, MOD_ROWS, 3 * D_MODEL), F32),
        compiler_params=_params2(),
        name="ada_mod",
    )(cvec, ada_w, ada_b.reshape(DEPTH, 1, 3 * D_MODEL))


def _modulated(x, mod_ref, g_ref):
    m = mod_ref[0]
    gain = g_ref[...] * (1.0 + m[1:2, :])
    return (_rms(x) * gain + m[0:1, :]).astype(BF16)


def _rope_block(b, cos, sin, partner):
    first = (_lane_iota(b.shape) & partner) == 0
    sw = jnp.where(first, pltpu.roll(b, LANES - partner, 1), pltpu.roll(b, partner, 1))
    return b * cos + sw * sin


def _emit_states(slot, values, prev_refs, out_refs):
    for idx, (val, out_ref) in enumerate(zip(values, out_refs)):
        if slot == 0:
            out_ref[...] = val
        else:
            prev_ref = prev_refs[idx]
            nb = val.shape[0] // SEQ
            out_ref[:, 0:SEQ, :] = prev_ref[...].reshape(nb, SEQ, val.shape[1])
            out_ref[:, SEQ:2 * SEQ, :] = val.reshape(nb, SEQ, val.shape[1])


def _state_specs(n, widths, slot, prev):
    tm = TOK_TILE
    if slot == 0:
        specs = [pl.BlockSpec((tm, c), lambda i: (i, 0)) for c in widths]
        return [], [], specs, [jax.ShapeDtypeStruct((n, c), F32) for c in widths]
    nb = tm // SEQ
    ins = [pl.BlockSpec((tm, c), lambda i: (i, 0)) for c in widths]
    outs = [pl.BlockSpec((nb, 2 * SEQ, c), lambda i: (i, 0, 0)) for c in widths]
    return ins, list(prev), outs, [jax.ShapeDtypeStruct((n // SEQ, 2 * SEQ, c), F32) for c in widths]


def _mod_row_fn(latent, tile):
    if latent:
        per_seq = DEC_SEQ // tile
        return lambda i: (1 + i // per_seq, 0, 0)
    return lambda i: (0, 0, 0)


def _mla_in_compute(x, ins, refs, *, rope, state):
    mod_ref, g_ref, win_ref, gq_ref, gkv_ref, wuq_ref, wuk_ref, wuv_ref = ins[:8]
    if rope:
        cos_ref, sin_ref = ins[8:10]
    q_ref, kc_ref, v_ref, ga_ref, vp_ref, gb_ref = refs[:6]
    h = _modulated(x, mod_ref, g_ref)

    def proj(c):
        return _dot(h, win_ref[:, c[0]:c[1]])

    cqn = (_rms(proj(_CQ)) * gq_ref[...]).astype(BF16)
    q = _dot(cqn, wuq_ref[...]) * (MLA_SCALE * LOG2E)
    ckvn = _rms(proj(_CKV)) * gkv_ref[...]
    ckvb = ckvn.astype(BF16)
    kr = proj(_KR)
    if state is not None:
        kr_state = pltpu.roll(kr, LANES - MLA_NOPE, 1)[:, :MLA_ROPE]
        _emit_states(state, (ckvn, kr_state), ins[8:10], refs[6:8])
    if rope:
        cos = cos_ref[...]
        sin = sin_ref[...]
        kr = _rope_block(kr, cos, sin, MLA_ROPE // 4)
    kc = _dot(ckvb, wuk_ref[...])
    for hh in range(MLA_HEADS):
        blk = slice(hh * MLA_HEAD_PAD, (hh + 1) * MLA_HEAD_PAD)
        qb = q[:, blk]
        if rope:
            qb = _rope_block(qb, cos, sin, MLA_ROPE // 4)
        q_ref[:, blk] = qb.astype(BF16)
        kc_ref[:, blk] = (kc[:, blk] + kr).astype(BF16)
    v_ref[...] = _dot(ckvb, wuv_ref[...]).astype(BF16)
    ga_ref[...] = _silu(proj(_GA)).astype(BF16)
    vp_ref[...] = proj(_VP).astype(BF16)
    gb_ref[...] = _silu(proj(_GB)).astype(BF16)


class _Part:
    def __init__(self, name, compute, in_specs, args, out_specs, out_shape):
        self.name = name
        self.compute = compute
        self.in_specs = in_specs
        self.args = args
        self.out_specs = out_specs
        self.out_shape = out_shape


def _mla_in_part(n, mod, gpre, w, latent, rope_tabs, slot=None, prev=()):
    tm = TOK_TILE
    row = lambda i: (i, 0)
    in_specs = [
        pl.BlockSpec((1, 3, D_MODEL), _mod_row_fn(latent, tm)),
        _full((1, D_MODEL)),
        _full((D_MODEL, MLA_IN_COLS)),
        _full((1, MLA_Q_RANK)),
        _full((1, MLA_KV_RANK)),
        _full((MLA_Q_RANK, MLA_QK_W)),
        _full((MLA_KV_RANK, MLA_QK_W)),
        _full((MLA_KV_RANK, MLA_WIDTH)),
    ]
    args = [mod, gpre, w["w_in"], w["g_qn"], w["g_kvn"], w["w_uq"], w["w_uk"], w["w_uv"]]
    if latent:
        per_seq = DEC_SEQ // tm
        in_specs += [pl.BlockSpec((tm, LANES), lambda i: (i % per_seq, 0))] * 2
        args += list(rope_tabs)
    widths = [MLA_QK_W, MLA_QK_W, MLA_WIDTH, MLA_WIDTH, POOL_WIDTH, POOL_WIDTH]
    out_shape = [jax.ShapeDtypeStruct((n, c), BF16) for c in widths]
    out_specs = [pl.BlockSpec((tm, c), row) for c in widths]
    if not latent:
        s_in, s_args, s_out, s_shape = _state_specs(n, (MLA_KV_RANK, MLA_ROPE), slot, prev)
        in_specs += s_in
        args += s_args
        out_specs += s_out
        out_shape += s_shape
    compute = functools.partial(_mla_in_compute, rope=latent, state=slot)
    return _Part("mla_in", compute, in_specs, args, out_specs, out_shape)


def _stage_body(*refs, out_part, in_part):
    x_ref = refs[0]
    n_out_in = len(out_part.in_specs) if out_part else 0
    n_in_in = len(in_part.in_specs) if in_part else 0
    out_ins = refs[1:1 + n_out_in]
    in_ins = refs[1 + n_out_in:1 + n_out_in + n_in_in]
    outs = refs[1 + n_out_in + n_in_in:]
    x = x_ref[...]
    if out_part:
        x = out_part.compute(x, out_ins)
        outs[0][...] = x
        outs = outs[1:]
    if in_part:
        in_part.compute(x, in_ins, outs)


def _stage_call(x, latent, out_part, in_part):
    n = x.shape[0]
    tm = TOK_TILE
    row = lambda i: (i, 0)
    in_specs = [pl.BlockSpec((tm, D_MODEL), row)]
    args = [x]
    out_specs, out_shape = [], []
    names = []
    if out_part:
        in_specs += out_part.in_specs
        args += out_part.args
        out_specs.append(pl.BlockSpec((tm, D_MODEL), row))
        out_shape.append(jax.ShapeDtypeStruct((n, D_MODEL), F32))
        names.append(out_part.name)
    if in_part:
        in_specs += in_part.in_specs
        args += in_part.args
        out_specs += in_part.out_specs
        out_shape += in_part.out_shape
        names.append(in_part.name)
    names.append("latent" if latent else "context")
    return pl.pallas_call(
        functools.partial(_stage_body, out_part=out_part, in_part=in_part),
        grid=(n // tm,),
        in_specs=in_specs,
        out_specs=out_specs,
        out_shape=out_shape,
        compiler_params=_params(),
        name="_".join(names),
    )(*args)


def _mla_attn_body(*refs, has_ctx, nseq):
    if has_ctx:
        q_ref, kc_ref, v_ref, ckv_ref, kr_ref, wuk_ref, wuv_ref, ga_ref, o_ref, va_ref, kcc_ref, vca_ref = refs
    else:
        q_ref, kc_ref, v_ref, ga_ref, o_ref, va_ref = refs

    def augment(v, dst):
        for j in range(MLA_HEADS // 2):
            dst[:, 2 * j * LANES:(2 * j + 1) * LANES] = v[:, j * LANES:(j + 1) * LANES]
            dst[:, (2 * j + 1) * LANES:(2 * j + 2) * LANES] = jnp.ones((v.shape[0], LANES), BF16)

    @pl.when(pl.program_id(1) == 0)
    def _():
        augment(v_ref[...], va_ref)
        if has_ctx:
            ckvb = ckv_ref[0, 0].astype(BF16)
            kr = kr_ref[...]
            kc = _dot(ckvb, wuk_ref[...])
            for hh in range(MLA_HEADS):
                blk = slice(hh * MLA_HEAD_PAD, (hh + 1) * MLA_HEAD_PAD)
                kcc_ref[:, blk] = (kc[:, blk] + kr).astype(BF16)
            augment(_dot(ckvb, wuv_ref[...]).astype(BF16), vca_ref)

    def lane_tile_max(t, s):
        for cb in range(s.shape[1] // LANES):
            blk = s[:, cb * LANES:(cb + 1) * LANES]
            t = blk if t is None else jnp.maximum(t, blk)
        return t

    qrows = q_ref.shape[0] // nseq
    krows = kc_ref.shape[0] // nseq
    low = _lane_iota((qrows, LANES)) < HALF
    for si in range(nseq):
        qr = slice(si * qrows, (si + 1) * qrows)
        kr_ = slice(si * krows, (si + 1) * krows)
        for j in range(MLA_HEADS // 2):
            pair = slice(j * LANES, (j + 1) * LANES)
            vaug = slice(2 * j * LANES, (2 * j + 2) * LANES)
            outs = []
            for hh in (2 * j, 2 * j + 1):
                blk = slice(hh * MLA_HEAD_PAD, (hh + 1) * MLA_HEAD_PAD)
                qh = q_ref[qr, blk]
                s = _dot_nt(qh, kc_ref[kr_, blk])
                t = lane_tile_max(None, s)
                if has_ctx:
                    sc = _dot_nt(qh, kcc_ref[:, blk])
                    t = lane_tile_max(t, sc)
                m = jnp.max(t, axis=-1, keepdims=True)
                acc = _dot(jnp.exp2(s - m).astype(BF16), va_ref[kr_, vaug])
                if has_ctx:
                    acc = acc + _dot(jnp.exp2(sc - m).astype(BF16), vca_ref[:, vaug])
                outs.append(acc[:, :LANES] / acc[:, LANES:])
            o_pair = jnp.where(low, outs[0], outs[1])
            o_ref[qr, pair] = (o_pair * ga_ref[qr, pair].astype(F32)).astype(BF16)


def _mla_attn_call(q, kc, v, ga, nbatch, seq, ctx=None):
    nseq = 1 if ctx is not None else CTX_SEQS_PER_STEP
    tq = min(seq, MLA_Q_TILE) * nseq
    seq = seq * nseq
    nbatch = nbatch // nseq
    scratch = [pltpu.VMEM((seq, 2 * MLA_WIDTH), BF16)]
    if ctx is not None:
        scratch += [pltpu.VMEM((PAST_LEN, MLA_QK_W), BF16), pltpu.VMEM((PAST_LEN, 2 * MLA_WIDTH), BF16)]
    nq = seq // tq
    qrow = lambda b, i: (b * nq + i, 0)
    brow = lambda b, i: (b, 0)
    in_specs = [
        pl.BlockSpec((tq, MLA_QK_W), qrow),
        pl.BlockSpec((seq, MLA_QK_W), brow),
        pl.BlockSpec((seq, MLA_WIDTH), brow),
    ]
    args = [q, kc, v]
    if ctx is not None:
        cache, slot, ctx_kr, w_uk, w_uv = ctx
        in_specs += [
            pl.BlockSpec((1, 1, PAST_LEN, MLA_KV_RANK), lambda b, i: (b, slot, 0, 0)),
            pl.BlockSpec((PAST_LEN, LANES), brow),
            _full((MLA_KV_RANK, MLA_QK_W)),
            _full((MLA_KV_RANK, MLA_WIDTH)),
        ]
        args += [cache, ctx_kr, w_uk, w_uv]
    in_specs.append(pl.BlockSpec((tq, MLA_WIDTH), qrow))
    args.append(ga)
    return pl.pallas_call(
        functools.partial(_mla_attn_body, has_ctx=ctx is not None, nseq=nseq),
        grid=(nbatch, nq),
        in_specs=in_specs,
        out_specs=pl.BlockSpec((tq, MLA_WIDTH), qrow),
        out_shape=jax.ShapeDtypeStruct((nbatch * seq, MLA_WIDTH), BF16),
        scratch_shapes=scratch,
        compiler_params=_params2(),
        name="mla_attn_latent" if ctx is not None else "mla_attn_context",
    )(*args)


def _post_residual(out, x, mod_ref, gpost_ref):
    m = mod_ref[0]
    return x + _rms(out) * (m[2:3, :] * gpost_ref[...])


def _mla_out_compute(x, ins, *, seq):
    a_ref, vp_ref, vprev_ref, vnext_ref, gb_ref, pm_ref, wp_ref, sp_ref, wo_ref, mod_ref, gpost_ref = ins
    tm = vp_ref.shape[0]
    cur = vp_ref[...]
    slab = jnp.concatenate([vprev_ref[...], cur, vnext_ref[...]], axis=0)
    srow = lax.broadcasted_iota(jnp.int32, (POOL_BLOCK + 2 * POOL_HALO, POOL_WIDTH), 0)
    trow = lax.broadcasted_iota(jnp.int32, (POOL_BLOCK, LANES), 0)
    zero = jnp.zeros((POOL_BLOCK + 2 * POOL_HALO, POOL_WIDTH), BF16)
    pooled = [[] for _ in POOL_WINDOWS]
    for sb in range(tm // POOL_BLOCK):
        pos0 = (pl.program_id(0) * tm + sb * POOL_BLOCK) % seq
        lo_cut = jnp.where(pos0 == 0, POOL_HALO, 0)
        hi_cut = jnp.where(pos0 == seq - POOL_BLOCK, POOL_BLOCK + POOL_HALO, POOL_BLOCK + 2 * POOL_HALO)
        chunk = slab[sb * POOL_BLOCK:(sb + 1) * POOL_BLOCK + 2 * POOL_HALO]
        chunk = jnp.where(srow >= lo_cut, chunk, zero)
        chunk = jnp.where(srow < hi_cut, chunk, zero)
        t = pos0 + trow
        for g, w in enumerate(POOL_WINDOWS):
            cols = slice(g * POOL_GROUP_W, (g + 1) * POOL_GROUP_W)
            tot = _dot(pm_ref[g], chunk[:, cols])
            cnt = jnp.minimum(t + w // 2, seq) - jnp.maximum(t - w // 2, 0)
            own = cur[sb * POOL_BLOCK:(sb + 1) * POOL_BLOCK, cols].astype(F32)
            pooled[g].append((tot / cnt.astype(F32) - own).astype(BF16))
    bs = []
    for g in range(POOL_GROUPS):
        cols = slice(g * POOL_GROUP_W, (g + 1) * POOL_GROUP_W)
        y = _dot(jnp.concatenate(pooled[g], axis=0), wp_ref[g]) * sp_ref[:, cols]
        bs.append((y * gb_ref[:, cols].astype(F32)).astype(BF16))
    b = jnp.concatenate(bs, axis=1)
    out = _dot(a_ref[...], wo_ref[0:MLA_WIDTH, :]) + _dot(b, wo_ref[MLA_WIDTH:, :])
    return _post_residual(out, x, mod_ref, gpost_ref)


def _pool_masks():
    r = np.arange(POOL_BLOCK)[:, None]
    c = np.arange(POOL_BLOCK + 2 * POOL_HALO)[None, :]
    rel = c - POOL_HALO - r
    ms = [((rel >= -(w // 2)) & (rel <= w // 2 - 1)) for w in POOL_WINDOWS]
    return jnp.asarray(np.stack(ms).astype(np.float32), dtype=BF16)


def _mla_out_part(n, a, vp, gb, mod, gpost, w, latent):
    tm = TOK_TILE
    seq = DEC_SEQ if latent else SEQ
    row = lambda i: (i, 0)
    hb = tm // POOL_HALO
    nhb = n // POOL_HALO
    in_specs = [
        pl.BlockSpec((tm, MLA_WIDTH), row),
        pl.BlockSpec((tm, POOL_WIDTH), row),
        pl.BlockSpec((POOL_HALO, POOL_WIDTH), lambda i: (jnp.maximum(i * hb - 1, 0), 0)),
        pl.BlockSpec((POOL_HALO, POOL_WIDTH), lambda i: (jnp.minimum((i + 1) * hb, nhb - 1), 0)),
        pl.BlockSpec((tm, POOL_WIDTH), row),
        _full((POOL_GROUPS, POOL_BLOCK, POOL_BLOCK + 2 * POOL_HALO)),
        _full((POOL_GROUPS, POOL_GROUP_W, POOL_GROUP_W)),
        _full((1, POOL_WIDTH)),
        _full((MLA_WIDTH + POOL_WIDTH, D_MODEL)),
        pl.BlockSpec((1, 3, D_MODEL), _mod_row_fn(latent, tm)),
        _full((1, D_MODEL)),
    ]
    args = [a, vp, vp, vp, gb, _pool_masks(), w["w_pool"], w["s_pool"], w["w_out"], mod, gpost]
    return _Part("mla_out", functools.partial(_mla_out_compute, seq=seq), in_specs, args, None, None)


def _swa_out_compute(x, ins):
    a_ref, wo_ref, mod_ref, gpost_ref = ins
    return _post_residual(_dot(a_ref[...], wo_ref[...]), x, mod_ref, gpost_ref)


def _swa_out_part(a, mod, gpost, w_out, latent):
    tm = TOK_TILE
    in_specs = [
        pl.BlockSpec((tm, SWA_WIDTH), lambda i: (i, 0)),
        _full((SWA_WIDTH, D_MODEL)),
        pl.BlockSpec((1, 3, D_MODEL), _mod_row_fn(latent, tm)),
        _full((1, D_MODEL)),
    ]
    return _Part("swa_out", _swa_out_compute, in_specs, [a, w_out, mod, gpost], None, None)


_SQ = (0, SWA_WIDTH)
_SK = (_SQ[1], _SQ[1] + SWA_KV_W)
_SV = (_SK[1], _SK[1] + SWA_KV_W)
_SG = (_SV[1], _SV[1] + SWA_WIDTH)
SWA_IN_COLS = _SG[1]
SWA_KV_DUP_W = 2 * SWA_KV_W


def _dup_heads(x, o_ref):
    for b in range(SWA_KV_W // LANES):
        blk = x[:, b * LANES:(b + 1) * LANES]
        rot = pltpu.roll(blk, HALF, 1)
        low = _lane_iota(blk.shape) < HALF
        o_ref[:, (2 * b) * LANES:(2 * b + 1) * LANES] = jnp.where(low, blk, rot).astype(BF16)
        o_ref[:, (2 * b + 1) * LANES:(2 * b + 2) * LANES] = jnp.where(low, rot, blk).astype(BF16)


def _swa_in_compute(x, ins, refs, *, rope, state):
    mod_ref, g_ref, win_ref = ins[:3]
    if rope:
        cos_ref, sin_ref = ins[3:5]
    q_ref, kd_ref, vd_ref, gs_ref = refs[:4]
    h = _modulated(x, mod_ref, g_ref)

    def proj(c):
        return _dot(h, win_ref[:, c[0]:c[1]])

    q = proj(_SQ) * (SWA_SCALE * LOG2E)
    k = proj(_SK)
    v = proj(_SV)
    if state is not None:
        _emit_states(state, (k, v), ins[3:5], refs[4:6])
    if rope:
        cos = cos_ref[...]
        sin = sin_ref[...]
        for b in range(SWA_WIDTH // LANES):
            blk = slice(b * LANES, (b + 1) * LANES)
            q_ref[:, blk] = _rope_block(q[:, blk], cos, sin, SWA_HEAD_DIM // 4).astype(BF16)
        k = jnp.concatenate(
            [_rope_block(k[:, b * LANES:(b + 1) * LANES], cos, sin, SWA_HEAD_DIM // 4)
             for b in range(SWA_KV_W // LANES)], axis=1)
    else:
        q_ref[...] = q.astype(BF16)
    _dup_heads(k, kd_ref)
    _dup_heads(v, vd_ref)
    gs_ref[...] = _silu(proj(_SG)).astype(BF16)


def _swa_in_part(n, mod, gpre, w_in, latent, rope_tabs, slot=None, prev=()):
    tm = TOK_TILE
    row = lambda i: (i, 0)
    in_specs = [
        pl.BlockSpec((1, 3, D_MODEL), _mod_row_fn(latent, tm)),
        _full((1, D_MODEL)),
        _full((D_MODEL, SWA_IN_COLS)),
    ]
    args = [mod, gpre, w_in]
    if latent:
        per_seq = DEC_SEQ // tm
        in_specs += [pl.BlockSpec((tm, LANES), lambda i: (i % per_seq, 0))] * 2
        args += list(rope_tabs)
    widths = [SWA_WIDTH, SWA_KV_DUP_W, SWA_KV_DUP_W, SWA_WIDTH]
    out_shape = [jax.ShapeDtypeStruct((n, c), BF16) for c in widths]
    out_specs = [pl.BlockSpec((tm, c), row) for c in widths]
    if not latent:
        s_in, s_args, s_out, s_shape = _state_specs(n, (SWA_KV_W, SWA_KV_W), slot, prev)
        in_specs += s_in
        args += s_args
        out_specs += s_out
        out_shape += s_shape
    compute = functools.partial(_swa_in_compute, rope=latent, state=slot)
    return _Part("swa_in", compute, in_specs, args, out_specs, out_shape)


SWA_GROUP = SWA_HEADS // SWA_KV_HEADS


def _swa_group_attend(qa, qb, parts, sinks):
    rows = qa.shape[0]
    low = _lane_iota(qa.shape) < HALF
    zero = jnp.zeros_like(qa)
    qst = jnp.concatenate(
        [jnp.where(low, qa, zero), jnp.where(low, zero, qa), jnp.where(low, qb, zero), jnp.where(low, zero, qb)],
        axis=0)
    sink_t = jnp.concatenate([jnp.full((rows, LANES), s * LOG2E, F32) for s in sinks], axis=0)
    scores = []
    t = sink_t
    for k, _, bias in parts:
        s = _dot_nt(qst, k)
        if bias is not None:
            s = s + bias
        scores.append(s)
        for cb in range(s.shape[1] // LANES):
            t = jnp.maximum(t, s[:, cb * LANES:(cb + 1) * LANES])
    m = jnp.max(t, axis=-1, keepdims=True)
    acc = None
    for s, (_, v, _) in zip(scores, parts):
        p = jnp.exp2(s - m).astype(BF16)
        pv = _dot(p, jnp.concatenate([v, jnp.ones_like(v)], axis=1))
        acc = pv if acc is None else acc + pv
    o = acc[:, :LANES] / (acc[:, LANES:] + jnp.exp2(sink_t - m))
    return (jnp.where(low, o[0:rows], o[rows:2 * rows]),
            jnp.where(low, o[2 * rows:3 * rows], o[3 * rows:4 * rows]))


def _swa_dense_body(sink_ref, q_ref, kd_ref, vd_ref, gs_ref, o_ref):
    for si in range(q_ref.shape[0] // SEQ):
        rows = slice(si * SEQ, (si + 1) * SEQ)
        for kh in range(SWA_KV_HEADS):
            kv = slice(kh * LANES, (kh + 1) * LANES)
            ta = slice(2 * kh * LANES, (2 * kh + 1) * LANES)
            tb = slice((2 * kh + 1) * LANES, (2 * kh + 2) * LANES)
            sinks = [sink_ref[SWA_GROUP * kh + e] for e in range(SWA_GROUP)]
            parts = [(kd_ref[rows, kv], vd_ref[rows, kv], None)]
            oa, ob = _swa_group_attend(q_ref[rows, ta], q_ref[rows, tb], parts, sinks)
            o_ref[rows, ta] = (oa * gs_ref[rows, ta].astype(F32)).astype(BF16)
            o_ref[rows, tb] = (ob * gs_ref[rows, tb].astype(F32)).astype(BF16)


def _swa_dense_call(sink, q, kd, vd, gs):
    n = q.shape[0]
    tr = CTX_SEQS_PER_STEP * SEQ
    row = lambda b: (b, 0)
    return pl.pallas_call(
        _swa_dense_body,
        grid=(n // tr,),
        in_specs=[
            pl.BlockSpec(memory_space=pltpu.SMEM),
            pl.BlockSpec((tr, SWA_WIDTH), row),
            pl.BlockSpec((tr, SWA_KV_DUP_W), row),
            pl.BlockSpec((tr, SWA_KV_DUP_W), row),
            pl.BlockSpec((tr, SWA_WIDTH), row),
        ],
        out_specs=pl.BlockSpec((tr, SWA_WIDTH), row),
        out_shape=jax.ShapeDtypeStruct((n, SWA_WIDTH), BF16),
        compiler_params=_params(),
        name="swa_attn_context",
    )(sink, q, kd, vd, gs)


SWA_Q_TILE = 2048
SWA_SPAN = 3 * SWA_WINDOW


def _swa_band_body(sink_ref, q_ref, kd_ref, vd_ref, ck_ref, cv_ref, gs_ref, o_ref, kx_ref, vx_ref):
    w = SWA_WINDOW
    nsub = q_ref.shape[0] // w
    step = pl.program_id(1)

    @pl.when(step == 0)
    def _():
        _dup_heads(ck_ref[0, 0], kx_ref)
        _dup_heads(cv_ref[0, 0], vx_ref)

    r = lax.broadcasted_iota(jnp.int32, (w, SWA_SPAN), 0)
    c = lax.broadcasted_iota(jnp.int32, (w, SWA_SPAN), 1)

    def block(sb, carry):
        qbase = (step * nsub + sb) * w
        start = pl.multiple_of(jnp.clip(qbase - w, 0, DEC_SEQ - SWA_SPAN), w)
        rel = c - r + (start - qbase)
        bias = jnp.where(jnp.abs(rel) <= w, 0.0, NEG_INF).astype(F32)
        bias = jnp.concatenate([bias] * SWA_GROUP, axis=0)
        rows = pl.ds(pl.multiple_of(sb * w, w), w)
        span = pl.ds(start, SWA_SPAN)
        for kh in range(SWA_KV_HEADS):
            kv = slice(kh * LANES, (kh + 1) * LANES)
            ta = slice(2 * kh * LANES, (2 * kh + 1) * LANES)
            tb = slice((2 * kh + 1) * LANES, (2 * kh + 2) * LANES)
            sinks = [sink_ref[SWA_GROUP * kh + e] for e in range(SWA_GROUP)]
            parts = [(kd_ref[span, kv], vd_ref[span, kv], bias), (kx_ref[:, kv], vx_ref[:, kv], None)]
            oa, ob = _swa_group_attend(q_ref[rows, ta], q_ref[rows, tb], parts, sinks)
            o_ref[rows, ta] = (oa * gs_ref[rows, ta].astype(F32)).astype(BF16)
            o_ref[rows, tb] = (ob * gs_ref[rows, tb].astype(F32)).astype(BF16)
        return carry

    lax.fori_loop(0, nsub, block, 0, unroll=True)


def _swa_band_call(sink, q, kd, vd, cache_k, cache_v, slot, gs):
    tq = SWA_Q_TILE
    nq = DEC_SEQ // tq
    qrow = lambda b, i: (b * nq + i, 0)
    brow = lambda b, i: (b, 0)
    cache = pl.BlockSpec((1, 1, PAST_LEN, SWA_KV_W), lambda b, i: (b, slot, 0, 0))
    return pl.pallas_call(
        _swa_band_body,
        grid=(DEC_BATCH, nq),
        in_specs=[
            pl.BlockSpec(memory_space=pltpu.SMEM),
            pl.BlockSpec((tq, SWA_WIDTH), qrow),
            pl.BlockSpec((DEC_SEQ, SWA_KV_DUP_W), brow),
            pl.BlockSpec((DEC_SEQ, SWA_KV_DUP_W), brow),
            cache,
            cache,
            pl.BlockSpec((tq, SWA_WIDTH), qrow),
        ],
        out_specs=pl.BlockSpec((tq, SWA_WIDTH), qrow),
        out_shape=jax.ShapeDtypeStruct((DEC_BATCH * DEC_SEQ, SWA_WIDTH), BF16),
        scratch_shapes=[pltpu.VMEM((PAST_LEN, SWA_KV_DUP_W), BF16)] * 2,
        compiler_params=_params2(),
        name="swa_attn_latent",
    )(sink, q, kd, vd, cache_k, cache_v, gs)


def _rope_tables(rope_dims, lane_period, lane_off):
    n = rope_dims // 2
    hf = n // 2
    t = jnp.arange(DEC_SEQ)
    pos = jnp.stack([(t // GRID_W).astype(F32), (t % GRID_W).astype(F32)], axis=0)
    inv = ROPE_BASE ** (-jnp.arange(0, n, 2, dtype=F32) / n)
    ang = pos[:, :, None] * inv[None, None, :]
    cos = jnp.cos(ang)
    sin = jnp.sin(ang)
    cos_d = jnp.concatenate([cos, cos], axis=-1)
    sin_d = jnp.concatenate([-sin, sin], axis=-1)
    cos_r = jnp.concatenate([cos_d[0], cos_d[1]], axis=-1)
    sin_r = jnp.concatenate([sin_d[0], sin_d[1]], axis=-1)
    reps = LANES // lane_period
    pad = ((0, 0), (lane_off, lane_period - lane_off - rope_dims))
    cos_p = jnp.tile(jnp.pad(cos_r, pad, constant_values=1.0), (1, reps))
    sin_p = jnp.tile(jnp.pad(sin_r, pad), (1, reps))
    return cos_p, sin_p


def _prep_mla(i, mla_w_in, mla_g_qn, mla_g_kvn, mla_w_uq, mla_w_ukv, pool_w, pool_scale, mixa_w_out):
    w_in = mla_w_in[i]
    i1 = MLA_Q_RANK
    i2 = i1 + MLA_KV_RANK
    i3 = i2 + MLA_ROPE
    kr = jnp.pad(w_in[:, i2:i3], ((0, 0), (MLA_NOPE, LANES - MLA_NOPE - MLA_ROPE)))
    w_in_r = jnp.concatenate([w_in[:, :i2], kr, w_in[:, i3:]], axis=1).astype(BF16)
    dq = MLA_NOPE + MLA_ROPE
    w_uq = jnp.pad(mla_w_uq[i].reshape(MLA_Q_RANK, MLA_HEADS, dq),
                   ((0, 0), (0, 0), (0, MLA_HEAD_PAD - dq))).reshape(MLA_Q_RANK, MLA_QK_W).astype(BF16)
    ukv = mla_w_ukv[i].reshape(MLA_KV_RANK, MLA_HEADS, MLA_NOPE + MLA_V)
    w_uk = jnp.pad(ukv[:, :, :MLA_NOPE], ((0, 0), (0, 0), (0, MLA_HEAD_PAD - MLA_NOPE)))
    w_uk = w_uk.reshape(MLA_KV_RANK, MLA_QK_W).astype(BF16)
    w_uv = ukv[:, :, MLA_NOPE:].reshape(MLA_KV_RANK, MLA_WIDTH).astype(BF16)
    return {
        "w_in": w_in_r, "w_uq": w_uq, "w_uk": w_uk, "w_uv": w_uv,
        "g_qn": mla_g_qn[i].reshape(1, -1), "g_kvn": mla_g_kvn[i].reshape(1, -1),
        "w_pool": pool_w[i].astype(BF16), "s_pool": pool_scale[i].reshape(1, -1),
        "w_out": mixa_w_out[i].astype(BF16),
    }


def kernel(x_prompt, x_sample, cache_ckv, cache_krope, cache_k, cache_v, c, c_ctx, ada_w, ada_b, norm_pre, norm_post, mla_w_in, mla_g_qn, mla_g_kvn, mla_w_uq, mla_w_ukv, pool_w, pool_scale, mixa_w_out, swa_w_in, swa_sink, swa_w_out):
    n_p = BATCH * SEQ
    n_s = DEC_BATCH * DEC_SEQ
    xp = x_prompt.reshape(n_p, D_MODEL)
    xs = x_sample.reshape(n_s, D_MODEL)
    cvec = jnp.concatenate(
        [c_ctx[None], c, jnp.zeros((MOD_ROWS - 1 - DEC_BATCH, D_MODEL), F32)], axis=0)
    mods = _ada_call(cvec, ada_w, ada_b).reshape(DEPTH, MOD_ROWS, 3, D_MODEL)
    mla_tabs = _rope_tables(MLA_ROPE, LANES, MLA_NOPE)
    swa_tabs = _rope_tables(SWA_HEAD_DIM, SWA_HEAD_DIM, 0)
    ck_view = cache_k.reshape(DEC_BATCH, DEPTH // 2, PAST_LEN, SWA_KV_W)
    cv_view = cache_v.reshape(DEC_BATCH, DEPTH // 2, PAST_LEN, SWA_KV_W)
    mla_state = swa_state = ()
    pend_p = pend_s = None

    def stage(x, latent, out_part, in_part):
        res = list(_stage_call(x, latent, out_part, in_part))
        if out_part is not None:
            return res[0], res[1:]
        return x, res

    for l in range(DEPTH):
        i = l // 2
        mod = mods[l]
        gpre = norm_pre[l].reshape(1, D_MODEL)
        gpost = norm_post[l].reshape(1, D_MODEL)
        if l % 2 == 0:
            w = _prep_mla(i, mla_w_in, mla_g_qn, mla_g_kvn, mla_w_uq, mla_w_ukv, pool_w, pool_scale,
                          mixa_w_out)
            xp, (qp, kcp, vp, gap, vpp, gbp, *mla_state) = stage(
                xp, False, pend_p, _mla_in_part(n_p, mod, gpre, w, False, None, i, mla_state))
            xs, (qs, kcs, vs, gas, vps, gbs) = stage(
                xs, True, pend_s, _mla_in_part(n_s, mod, gpre, w, True, mla_tabs))
            ctx_kr = jnp.pad(cache_krope[:, i].reshape(DEC_BATCH * PAST_LEN, MLA_ROPE),
                             ((0, 0), (MLA_NOPE, LANES - MLA_NOPE - MLA_ROPE)))
            ctx = (cache_ckv, i, ctx_kr, w["w_uk"], w["w_uv"])
            ap = _mla_attn_call(qp, kcp, vp, gap, BATCH, SEQ)
            as_ = _mla_attn_call(qs, kcs, vs, gas, DEC_BATCH, DEC_SEQ, ctx)
            pend_p = _mla_out_part(n_p, ap, vpp, gbp, mod, gpost, w, False)
            pend_s = _mla_out_part(n_s, as_, vps, gbs, mod, gpost, w, True)
        else:
            w_in = swa_w_in[i].astype(BF16)
            w_out = swa_w_out[i].astype(BF16)
            sink = swa_sink[i]
            xp, (qp, kdp, vdp, gsp, *swa_state) = stage(
                xp, False, pend_p, _swa_in_part(n_p, mod, gpre, w_in, False, None, i, swa_state))
            xs, (qs, kds, vds, gss) = stage(
                xs, True, pend_s, _swa_in_part(n_s, mod, gpre, w_in, True, swa_tabs))

            ap = _swa_dense_call(sink, qp, kdp, vdp, gsp)
            as_ = _swa_band_call(sink, qs, kds, vds, ck_view, cv_view, i, gss)
            pend_p = _swa_out_part(ap, mod, gpost, w_out, False)
            pend_s = _swa_out_part(as_, mod, gpost, w_out, True)
    xp, _ = stage(xp, False, pend_p, None)
    xs, _ = stage(xs, True, pend_s, None)
    return (
        xp.reshape(BATCH, SEQ, D_MODEL),
        xs.reshape(DEC_BATCH, DEC_SEQ, D_MODEL),
        mla_state[0].reshape(BATCH, DEPTH // 2, SEQ, MLA_KV_RANK),
        mla_state[1].reshape(BATCH, DEPTH // 2, SEQ, MLA_ROPE),
        swa_state[0].reshape(BATCH, DEPTH // 2, SEQ, SWA_KV_HEADS, SWA_HEAD_DIM),
        swa_state[1].reshape(BATCH, DEPTH // 2, SEQ, SWA_KV_HEADS, SWA_HEAD_DIM),
    )
```

```python
import functools

import numpy as np
import jax
import jax.numpy as jnp
from jax import lax
from jax.experimental import pallas as pl
from jax.experimental.pallas import tpu as pltpu

F32 = jnp.float32
BF16 = jnp.bfloat16

D_MODEL = 1024
BATCH = 32
SEQ = 256
DEPTH = 4
DEC_BATCH = 8
DEC_SEQ = 2048
PAST_LEN = 256
GRID_W = 64
ROPE_BASE = 10000.0
EPS = 1e-6
NEG_INF = -1e30

MLA_HEADS = 8
MLA_NOPE = 64
MLA_ROPE = 32
MLA_V = 64
MLA_Q_RANK = 384
MLA_KV_RANK = 256
MLA_WIDTH = MLA_HEADS * MLA_V
MLA_SCALE = (MLA_NOPE + MLA_ROPE) ** -0.5
POOL_WINDOWS = (2, 4, 8, 16)
POOL_GROUPS = 4
POOL_GROUP_W = D_MODEL // 8
POOL_WIDTH = POOL_GROUPS * POOL_GROUP_W
SWA_HEADS = 16
SWA_KV_HEADS = 4
SWA_HEAD_DIM = 64
SWA_WINDOW = 128
SWA_WIDTH = SWA_HEADS * SWA_HEAD_DIM
SWA_KV_W = SWA_KV_HEADS * SWA_HEAD_DIM
SWA_SCALE = SWA_HEAD_DIM ** -0.5

LANES = 128
HALF = LANES // 2
BF16_SUBLANES = 16
MOD_ROWS = 16
TOK_TILE = 1024
LOG2E = 1.4426950408889634
POOL_BLOCK = 128
POOL_HALO = 64
VMEM_LIMIT = 60 * 1024 * 1024

_CQ = (0, MLA_Q_RANK)
_CKV = (_CQ[1], _CQ[1] + MLA_KV_RANK)
_KR = (_CKV[1], _CKV[1] + LANES)
_GA = (_KR[1], _KR[1] + MLA_WIDTH)
_VP = (_GA[1], _GA[1] + POOL_WIDTH)
_GB = (_VP[1], _VP[1] + POOL_WIDTH)
MLA_IN_COLS = _GB[1]
MLA_HEAD_PAD = LANES
MLA_QK_W = MLA_HEADS * MLA_HEAD_PAD
MLA_Q_TILE = 1024
CTX_SEQS_PER_STEP = 4


def _params():
    return pltpu.CompilerParams(dimension_semantics=("arbitrary",), vmem_limit_bytes=VMEM_LIMIT,
                                shape_invariant_numerics=False)


def _params2():
    return pltpu.CompilerParams(dimension_semantics=("arbitrary", "arbitrary"), vmem_limit_bytes=VMEM_LIMIT,
                                shape_invariant_numerics=False)


def _dot(a, b):
    return jnp.dot(a, b, preferred_element_type=F32)


def _dot_nt(a, b):
    return lax.dot_general(a, b, (((1,), (1,)), ((), ())), preferred_element_type=F32)


def _silu(x):
    return x / (1.0 + jnp.exp(-x))


def _rms(x):
    return x * lax.rsqrt(jnp.mean(x * x, axis=-1, keepdims=True) + EPS)


def _lane_iota(shape):
    return lax.broadcasted_iota(jnp.int32, shape, len(shape) - 1)


def _full(shape):
    nd = len(shape)
    return pl.BlockSpec(shape, lambda *_: (0,) * nd, pipeline_mode=pl.Buffered(1))


def _ada_body(c_ref, w_ref, b_ref, o_ref):
    s = _silu(c_ref[...]).astype(BF16)
    o_ref[0] = _dot(s, w_ref[0].astype(BF16)) + b_ref[0]


def _ada_call(cvec, ada_w, ada_b):
    tn = 768
    return pl.pallas_call(
        _ada_body,
        grid=(DEPTH, 3 * D_MODEL // tn),
        in_specs=[
            pl.BlockSpec((MOD_ROWS, D_MODEL), lambda l, j: (0, 0)),
            pl.BlockSpec((1, D_MODEL, tn), lambda l, j: (l, 0, j)),
            pl.BlockSpec((1, 1, tn), lambda l, j: (l, 0, j)),
        ],
        out_specs=pl.BlockSpec((1, MOD_ROWS, tn), lambda l, j: (l, 0, j)),
        out_shape=jax.ShapeDtypeStruct((DEPTH, MOD_ROWS, 3 * D_MODEL), F32),
        compiler_params=_params2(),
        name="ada_mod",
    )(cvec, ada_w, ada_b.reshape(DEPTH, 1, 3 * D_MODEL))


def _modulated(x, mod_ref, g_ref):
    m = mod_ref[0]
    gain = g_ref[...] * (1.0 + m[1:2, :])
    return (_rms(x) * gain + m[0:1, :]).astype(BF16)


def _rope_block(b, cos, sin, partner):
    first = (_lane_iota(b.shape) & partner) == 0
    sw = jnp.where(first, pltpu.roll(b, LANES - partner, 1), pltpu.roll(b, partner, 1))
    return b * cos + sw * sin


def _emit_states(slot, values, prev_refs, out_refs):
    for idx, (val, out_ref) in enumerate(zip(values, out_refs)):
        if slot == 0:
            out_ref[...] = val
        else:
            prev_ref = prev_refs[idx]
            nb = val.shape[0] // SEQ
            out_ref[:, 0:SEQ, :] = prev_ref[...].reshape(nb, SEQ, val.shape[1])
            out_ref[:, SEQ:2 * SEQ, :] = val.reshape(nb, SEQ, val.shape[1])


def _state_specs(n, widths, slot, prev):
    tm = TOK_TILE
    if slot == 0:
        specs = [pl.BlockSpec((tm, c), lambda i: (i, 0)) for c in widths]
        return [], [], specs, [jax.ShapeDtypeStruct((n, c), F32) for c in widths]
    nb = tm // SEQ
    ins = [pl.BlockSpec((tm, c), lambda i: (i, 0)) for c in widths]
    outs = [pl.BlockSpec((nb, 2 * SEQ, c), lambda i: (i, 0, 0)) for c in widths]
    return ins, list(prev), outs, [jax.ShapeDtypeStruct((n // SEQ, 2 * SEQ, c), F32) for c in widths]


def _mod_row_fn(latent, tile):
    if latent:
        per_seq = DEC_SEQ // tile
        return lambda i: (1 + i // per_seq, 0, 0)
    return lambda i: (0, 0, 0)


def _mla_in_compute(x, ins, refs, *, rope, state):
    mod_ref, g_ref, win_ref, gq_ref, gkv_ref, wuq_ref, wuk_ref, wuv_ref = ins[:8]
    if rope:
        cos_ref, sin_ref = ins[8:10]
    q_ref, kc_ref, v_ref, ga_ref, vp_ref, gb_ref = refs[:6]
    h = _modulated(x, mod_ref, g_ref)

    def proj(c):
        return _dot(h, win_ref[:, c[0]:c[1]])

    cqn = (_rms(proj(_CQ)) * gq_ref[...]).astype(BF16)
    q = _dot(cqn, wuq_ref[...]) * (MLA_SCALE * LOG2E)
    ckvn = _rms(proj(_CKV)) * gkv_ref[...]
    ckvb = ckvn.astype(BF16)
    kr = proj(_KR)
    if state is not None:
        kr_state = pltpu.roll(kr, LANES - MLA_NOPE, 1)[:, :MLA_ROPE]
        _emit_states(state, (ckvn, kr_state), ins[8:10], refs[6:8])
    if rope:
        cos = cos_ref[...]
        sin = sin_ref[...]
        kr = _rope_block(kr, cos, sin, MLA_ROPE // 4)
    kc = _dot(ckvb, wuk_ref[...])
    for hh in range(MLA_HEADS):
        blk = slice(hh * MLA_HEAD_PAD, (hh + 1) * MLA_HEAD_PAD)
        qb = q[:, blk]
        if rope:
            qb = _rope_block(qb, cos, sin, MLA_ROPE // 4)
        q_ref[:, blk] = qb.astype(BF16)
        kc_ref[:, blk] = (kc[:, blk] + kr).astype(BF16)
    v_ref[...] = _dot(ckvb, wuv_ref[...]).astype(BF16)
    ga_ref[...] = _silu(proj(_GA)).astype(BF16)
    vp_ref[...] = proj(_VP).astype(BF16)
    gb_ref[...] = _silu(proj(_GB)).astype(BF16)


class _Part:
    def __init__(self, name, compute, in_specs, args, out_specs, out_shape):
        self.name = name
        self.compute = compute
        self.in_specs = in_specs
        self.args = args
        self.out_specs = out_specs
        self.out_shape = out_shape


def _mla_in_part(n, mod, gpre, w, latent, rope_tabs, slot=None, prev=()):
    tm = TOK_TILE
    row = lambda i: (i, 0)
    in_specs = [
        pl.BlockSpec((1, 3, D_MODEL), _mod_row_fn(latent, tm)),
        _full((1, D_MODEL)),
        _full((D_MODEL, MLA_IN_COLS)),
        _full((1, MLA_Q_RANK)),
        _full((1, MLA_KV_RANK)),
        _full((MLA_Q_RANK, MLA_QK_W)),
        _full((MLA_KV_RANK, MLA_QK_W)),
        _full((MLA_KV_RANK, MLA_WIDTH)),
    ]
    args = [mod, gpre, w["w_in"], w["g_qn"], w["g_kvn"], w["w_uq"], w["w_uk"], w["w_uv"]]
    if latent:
        per_seq = DEC_SEQ // tm
        in_specs += [pl.BlockSpec((tm, LANES), lambda i: (i % per_seq, 0))] * 2
        args += list(rope_tabs)
    widths = [MLA_QK_W, MLA_QK_W, MLA_WIDTH, MLA_WIDTH, POOL_WIDTH, POOL_WIDTH]
    out_shape = [jax.ShapeDtypeStruct((n, c), BF16) for c in widths]
    out_specs = [pl.BlockSpec((tm, c), row) for c in widths]
    if not latent:
        s_in, s_args, s_out, s_shape = _state_specs(n, (MLA_KV_RANK, MLA_ROPE), slot, prev)
        in_specs += s_in
        args += s_args
        out_specs += s_out
        out_shape += s_shape
    compute = functools.partial(_mla_in_compute, rope=latent, state=slot)
    return _Part("mla_in", compute, in_specs, args, out_specs, out_shape)


def _stage_body(*refs, out_part, in_part):
    x_ref = refs[0]
    n_out_in = len(out_part.in_specs) if out_part else 0
    n_in_in = len(in_part.in_specs) if in_part else 0
    out_ins = refs[1:1 + n_out_in]
    in_ins = refs[1 + n_out_in:1 + n_out_in + n_in_in]
    outs = refs[1 + n_out_in + n_in_in:]
    x = x_ref[...]
    if out_part:
        x = out_part.compute(x, out_ins)
        outs[0][...] = x
        outs = outs[1:]
    if in_part:
        in_part.compute(x, in_ins, outs)


def _stage_call(x, latent, out_part, in_part):
    n = x.shape[0]
    tm = TOK_TILE
    row = lambda i: (i, 0)
    in_specs = [pl.BlockSpec((tm, D_MODEL), row)]
    args = [x]
    out_specs, out_shape = [], []
    names = []
    if out_part:
        in_specs += out_part.in_specs
        args += out_part.args
        out_specs.append(pl.BlockSpec((tm, D_MODEL), row))
        out_shape.append(jax.ShapeDtypeStruct((n, D_MODEL), F32))
        names.append(out_part.name)
    if in_part:
        in_specs += in_part.in_specs
        args += in_part.args
        out_specs += in_part.out_specs
        out_shape += in_part.out_shape
        names.append(in_part.name)
    names.append("latent" if latent else "context")
    return pl.pallas_call(
        functools.partial(_stage_body, out_part=out_part, in_part=in_part),
        grid=(n // tm,),
        in_specs=in_specs,
        out_specs=out_specs,
        out_shape=out_shape,
        compiler_params=_params(),
        name="_".join(names),
    )(*args)


def _mla_attn_body(*refs, has_ctx, nseq):
    if has_ctx:
        q_ref, kc_ref, v_ref, ckv_ref, kr_ref, wuk_ref, wuv_ref, ga_ref, o_ref, va_ref, kcc_ref, vca_ref = refs
    else:
        q_ref, kc_ref, v_ref, ga_ref, o_ref, va_ref = refs

    def augment(v, dst):
        for j in range(MLA_HEADS // 2):
            dst[:, 2 * j * LANES:(2 * j + 1) * LANES] = v[:, j * LANES:(j + 1) * LANES]
            dst[:, (2 * j + 1) * LANES:(2 * j + 2) * LANES] = jnp.ones((v.shape[0], LANES), BF16)

    @pl.when(pl.program_id(1) == 0)
    def _():
        augment(v_ref[...], va_ref)
        if has_ctx:
            ckvb = ckv_ref[0, 0].astype(BF16)
            kr = kr_ref[...]
            kc = _dot(ckvb, wuk_ref[...])
            for hh in range(MLA_HEADS):
                blk = slice(hh * MLA_HEAD_PAD, (hh + 1) * MLA_HEAD_PAD)
                kcc_ref[:, blk] = (kc[:, blk] + kr).astype(BF16)
            augment(_dot(ckvb, wuv_ref[...]).astype(BF16), vca_ref)

    def lane_tile_max(t, s):
        for cb in range(s.shape[1] // LANES):
            blk = s[:, cb * LANES:(cb + 1) * LANES]
            t = blk if t is None else jnp.maximum(t, blk)
        return t

    qrows = q_ref.shape[0] // nseq
    krows = kc_ref.shape[0] // nseq
    low = _lane_iota((qrows, LANES)) < HALF
    for si in range(nseq):
        qr = slice(si * qrows, (si + 1) * qrows)
        kr_ = slice(si * krows, (si + 1) * krows)
        for j in range(MLA_HEADS // 2):
            pair = slice(j * LANES, (j + 1) * LANES)
            vaug = slice(2 * j * LANES, (2 * j + 2) * LANES)
            outs = []
            for hh in (2 * j, 2 * j + 1):
                blk = slice(hh * MLA_HEAD_PAD, (hh + 1) * MLA_HEAD_PAD)
                qh = q_ref[qr, blk]
                s = _dot_nt(qh, kc_ref[kr_, blk])
                t = lane_tile_max(None, s)
                if has_ctx:
                    sc = _dot_nt(qh, kcc_ref[:, blk])
                    t = lane_tile_max(t, sc)
                m = jnp.max(t, axis=-1, keepdims=True)
                acc = _dot(jnp.exp2(s - m).astype(BF16), va_ref[kr_, vaug])
                if has_ctx:
                    acc = acc + _dot(jnp.exp2(sc - m).astype(BF16), vca_ref[:, vaug])
                outs.append(acc[:, :LANES] / acc[:, LANES:])
            o_pair = jnp.where(low, outs[0], outs[1])
            o_ref[qr, pair] = (o_pair * ga_ref[qr, pair].astype(F32)).astype(BF16)


def _mla_attn_call(q, kc, v, ga, nbatch, seq, ctx=None):
    nseq = 1 if ctx is not None else CTX_SEQS_PER_STEP
    tq = min(seq, MLA_Q_TILE) * nseq
    seq = seq * nseq
    nbatch = nbatch // nseq
    scratch = [pltpu.VMEM((seq, 2 * MLA_WIDTH), BF16)]
    if ctx is not None:
        scratch += [pltpu.VMEM((PAST_LEN, MLA_QK_W), BF16), pltpu.VMEM((PAST_LEN, 2 * MLA_WIDTH), BF16)]
    nq = seq // tq
    qrow = lambda b, i: (b * nq + i, 0)
    brow = lambda b, i: (b, 0)
    in_specs = [
        pl.BlockSpec((tq, MLA_QK_W), qrow),
        pl.BlockSpec((seq, MLA_QK_W), brow),
        pl.BlockSpec((seq, MLA_WIDTH), brow),
    ]
    args = [q, kc, v]
    if ctx is not None:
        cache, slot, ctx_kr, w_uk, w_uv = ctx
        in_specs += [
            pl.BlockSpec((1, 1, PAST_LEN, MLA_KV_RANK), lambda b, i: (b, slot, 0, 0)),
            pl.BlockSpec((PAST_LEN, LANES), brow),
            _full((MLA_KV_RANK, MLA_QK_W)),
            _full((MLA_KV_RANK, MLA_WIDTH)),
        ]
        args += [cache, ctx_kr, w_uk, w_uv]
    in_specs.append(pl.BlockSpec((tq, MLA_WIDTH), qrow))
    args.append(ga)
    return pl.pallas_call(
        functools.partial(_mla_attn_body, has_ctx=ctx is not None, nseq=nseq),
        grid=(nbatch, nq),
        in_specs=in_specs,
        out_specs=pl.BlockSpec((tq, MLA_WIDTH), qrow),
        out_shape=jax.ShapeDtypeStruct((nbatch * seq, MLA_WIDTH), BF16),
        scratch_shapes=scratch,
        compiler_params=_params2(),
        name="mla_attn_latent" if ctx is not None else "mla_attn_context",
    )(*args)


def _post_residual(out, x, mod_ref, gpost_ref):
    m = mod_ref[0]
    return x + _rms(out) * (m[2:3, :] * gpost_ref[...])


def _mla_out_compute(x, ins, *, seq):
    a_ref, vp_ref, vprev_ref, vnext_ref, gb_ref, pm_ref, wp_ref, sp_ref, wo_ref, mod_ref, gpost_ref = ins
    tm = vp_ref.shape[0]
    cur = vp_ref[...]
    slab = jnp.concatenate([vprev_ref[...], cur, vnext_ref[...]], axis=0)
    srow = lax.broadcasted_iota(jnp.int32, (POOL_BLOCK + 2 * POOL_HALO, POOL_WIDTH), 0)
    trow = lax.broadcasted_iota(jnp.int32, (POOL_BLOCK, LANES), 0)
    zero = jnp.zeros((POOL_BLOCK + 2 * POOL_HALO, POOL_WIDTH), BF16)
    pooled = [[] for _ in POOL_WINDOWS]
    for sb in range(tm // POOL_BLOCK):
        pos0 = (pl.program_id(0) * tm + sb * POOL_BLOCK) % seq
        lo_cut = jnp.where(pos0 == 0, POOL_HALO, 0)
        hi_cut = jnp.where(pos0 == seq - POOL_BLOCK, POOL_BLOCK + POOL_HALO, POOL_BLOCK + 2 * POOL_HALO)
        chunk = slab[sb * POOL_BLOCK:(sb + 1) * POOL_BLOCK + 2 * POOL_HALO]
        chunk = jnp.where(srow >= lo_cut, chunk, zero)
        chunk = jnp.where(srow < hi_cut, chunk, zero)
        t = pos0 + trow
        for g, w in enumerate(POOL_WINDOWS):
            cols = slice(g * POOL_GROUP_W, (g + 1) * POOL_GROUP_W)
            tot = _dot(pm_ref[g], chunk[:, cols])
            cnt = jnp.minimum(t + w // 2, seq) - jnp.maximum(t - w // 2, 0)
            own = cur[sb * POOL_BLOCK:(sb + 1) * POOL_BLOCK, cols].astype(F32)
            pooled[g].append((tot / cnt.astype(F32) - own).astype(BF16))
    bs = []
    for g in range(POOL_GROUPS):
        cols = slice(g * POOL_GROUP_W, (g + 1) * POOL_GROUP_W)
        y = _dot(jnp.concatenate(pooled[g], axis=0), wp_ref[g]) * sp_ref[:, cols]
        bs.append((y * gb_ref[:, cols].astype(F32)).astype(BF16))
    b = jnp.concatenate(bs, axis=1)
    out = _dot(a_ref[...], wo_ref[0:MLA_WIDTH, :]) + _dot(b, wo_ref[MLA_WIDTH:, :])
    return _post_residual(out, x, mod_ref, gpost_ref)


def _pool_masks():
    r = np.arange(POOL_BLOCK)[:, None]
    c = np.arange(POOL_BLOCK + 2 * POOL_HALO)[None, :]
    rel = c - POOL_HALO - r
    ms = [((rel >= -(w // 2)) & (rel <= w // 2 - 1)) for w in POOL_WINDOWS]
    return jnp.asarray(np.stack(ms).astype(np.float32), dtype=BF16)


def _mla_out_part(n, a, vp, gb, mod, gpost, w, latent):
    tm = TOK_TILE
    seq = DEC_SEQ if latent else SEQ
    row = lambda i: (i, 0)
    hb = tm // POOL_HALO
    nhb = n // POOL_HALO
    in_specs = [
        pl.BlockSpec((tm, MLA_WIDTH), row),
        pl.BlockSpec((tm, POOL_WIDTH), row),
        pl.BlockSpec((POOL_HALO, POOL_WIDTH), lambda i: (jnp.maximum(i * hb - 1, 0), 0)),
        pl.BlockSpec((POOL_HALO, POOL_WIDTH), lambda i: (jnp.minimum((i + 1) * hb, nhb - 1), 0)),
        pl.BlockSpec((tm, POOL_WIDTH), row),
        _full((POOL_GROUPS, POOL_BLOCK, POOL_BLOCK + 2 * POOL_HALO)),
        _full((POOL_GROUPS, POOL_GROUP_W, POOL_GROUP_W)),
        _full((1, POOL_WIDTH)),
        _full((MLA_WIDTH + POOL_WIDTH, D_MODEL)),
        pl.BlockSpec((1, 3, D_MODEL), _mod_row_fn(latent, tm)),
        _full((1, D_MODEL)),
    ]
    args = [a, vp, vp, vp, gb, _pool_masks(), w["w_pool"], w["s_pool"], w["w_out"], mod, gpost]
    return _Part("mla_out", functools.partial(_mla_out_compute, seq=seq), in_specs, args, None, None)


def _swa_out_compute(x, ins):
    a_ref, wo_ref, mod_ref, gpost_ref = ins
    return _post_residual(_dot(a_ref[...], wo_ref[...]), x, mod_ref, gpost_ref)


def _swa_out_part(a, mod, gpost, w_out, latent):
    tm = TOK_TILE
    in_specs = [
        pl.BlockSpec((tm, SWA_WIDTH), lambda i: (i, 0)),
        _full((SWA_WIDTH, D_MODEL)),
        pl.BlockSpec((1, 3, D_MODEL), _mod_row_fn(latent, tm)),
        _full((1, D_MODEL)),
    ]
    return _Part("swa_out", _swa_out_compute, in_specs, [a, w_out, mod, gpost], None, None)


_SQ = (0, SWA_WIDTH)
_SK = (_SQ[1], _SQ[1] + SWA_KV_W)
_SV = (_SK[1], _SK[1] + SWA_KV_W)
_SG = (_SV[1], _SV[1] + SWA_WIDTH)
SWA_IN_COLS = _SG[1]
SWA_KV_DUP_W = 2 * SWA_KV_W


def _dup_heads(x, o_ref):
    for b in range(SWA_KV_W // LANES):
        blk = x[:, b * LANES:(b + 1) * LANES]
        rot = pltpu.roll(blk, HALF, 1)
        low = _lane_iota(blk.shape) < HALF
        o_ref[:, (2 * b) * LANES:(2 * b + 1) * LANES] = jnp.where(low, blk, rot).astype(BF16)
        o_ref[:, (2 * b + 1) * LANES:(2 * b + 2) * LANES] = jnp.where(low, rot, blk).astype(BF16)


def _swa_in_compute(x, ins, refs, *, rope, state):
    mod_ref, g_ref, win_ref = ins[:3]
    if rope:
        cos_ref, sin_ref = ins[3:5]
    q_ref, kd_ref, vd_ref, gs_ref = refs[:4]
    h = _modulated(x, mod_ref, g_ref)

    def proj(c):
        return _dot(h, win_ref[:, c[0]:c[1]])

    q = proj(_SQ) * (SWA_SCALE * LOG2E)
    k = proj(_SK)
    v = proj(_SV)
    if state is not None:
        _emit_states(state, (k, v), ins[3:5], refs[4:6])
    if rope:
        cos = cos_ref[...]
        sin = sin_ref[...]
        for b in range(SWA_WIDTH // LANES):
            blk = slice(b * LANES, (b + 1) * LANES)
            q_ref[:, blk] = _rope_block(q[:, blk], cos, sin, SWA_HEAD_DIM // 4).astype(BF16)
        k = jnp.concatenate(
            [_rope_block(k[:, b * LANES:(b + 1) * LANES], cos, sin, SWA_HEAD_DIM // 4)
             for b in range(SWA_KV_W // LANES)], axis=1)
    else:
        q_ref[...] = q.astype(BF16)
    _dup_heads(k, kd_ref)
    _dup_heads(v, vd_ref)
    gs_ref[...] = _silu(proj(_SG)).astype(BF16)


def _swa_in_part(n, mod, gpre, w_in, latent, rope_tabs, slot=None, prev=()):
    tm = TOK_TILE
    row = lambda i: (i, 0)
    in_specs = [
        pl.BlockSpec((1, 3, D_MODEL), _mod_row_fn(latent, tm)),
        _full((1, D_MODEL)),
        _full((D_MODEL, SWA_IN_COLS)),
    ]
    args = [mod, gpre, w_in]
    if latent:
        per_seq = DEC_SEQ // tm
        in_specs += [pl.BlockSpec((tm, LANES), lambda i: (i % per_seq, 0))] * 2
        args += list(rope_tabs)
    widths = [SWA_WIDTH, SWA_KV_DUP_W, SWA_KV_DUP_W, SWA_WIDTH]
    out_shape = [jax.ShapeDtypeStruct((n, c), BF16) for c in widths]
    out_specs = [pl.BlockSpec((tm, c), row) for c in widths]
    if not latent:
        s_in, s_args, s_out, s_shape = _state_specs(n, (SWA_KV_W, SWA_KV_W), slot, prev)
        in_specs += s_in
        args += s_args
        out_specs += s_out
        out_shape += s_shape
    compute = functools.partial(_swa_in_compute, rope=latent, state=slot)
    return _Part("swa_in", compute, in_specs, args, out_specs, out_shape)


SWA_GROUP = SWA_HEADS // SWA_KV_HEADS


def _swa_group_attend(qa, qb, parts, sinks):
    rows = qa.shape[0]
    low = _lane_iota(qa.shape) < HALF
    zero = jnp.zeros_like(qa)
    qst = jnp.concatenate(
        [jnp.where(low, qa, zero), jnp.where(low, zero, qa), jnp.where(low, qb, zero), jnp.where(low, zero, qb)],
        axis=0)
    sink_t = jnp.concatenate([jnp.full((rows, LANES), s * LOG2E, F32) for s in sinks], axis=0)
    scores = []
    t = sink_t
    for k, _, bias in parts:
        s = _dot_nt(qst, k)
        if bias is not None:
            s = s + bias
        scores.append(s)
        for cb in range(s.shape[1] // LANES):
            t = jnp.maximum(t, s[:, cb * LANES:(cb + 1) * LANES])
    m = jnp.max(t, axis=-1, keepdims=True)
    acc = None
    for s, (_, v, _) in zip(scores, parts):
        p = jnp.exp2(s - m).astype(BF16)
        pv = _dot(p, jnp.concatenate([v, jnp.ones_like(v)], axis=1))
        acc = pv if acc is None else acc + pv
    o = acc[:, :LANES] / (acc[:, LANES:] + jnp.exp2(sink_t - m))
    return (jnp.where(low, o[0:rows], o[rows:2 * rows]),
            jnp.where(low, o[2 * rows:3 * rows], o[3 * rows:4 * rows]))


def _swa_dense_body(sink_ref, q_ref, kd_ref, vd_ref, gs_ref, o_ref):
    for si in range(q_ref.shape[0] // SEQ):
        rows = slice(si * SEQ, (si + 1) * SEQ)
        for kh in range(SWA_KV_HEADS):
            kv = slice(kh * LANES, (kh + 1) * LANES)
            ta = slice(2 * kh * LANES, (2 * kh + 1) * LANES)
            tb = slice((2 * kh + 1) * LANES, (2 * kh + 2) * LANES)
            sinks = [sink_ref[SWA_GROUP * kh + e] for e in range(SWA_GROUP)]
            parts = [(kd_ref[rows, kv], vd_ref[rows, kv], None)]
            oa, ob = _swa_group_attend(q_ref[rows, ta], q_ref[rows, tb], parts, sinks)
            o_ref[rows, ta] = (oa * gs_ref[rows, ta].astype(F32)).astype(BF16)
            o_ref[rows, tb] = (ob * gs_ref[rows, tb].astype(F32)).astype(BF16)


def _swa_dense_call(sink, q, kd, vd, gs):
    n = q.shape[0]
    tr = CTX_SEQS_PER_STEP * SEQ
    row = lambda b: (b, 0)
    return pl.pallas_call(
        _swa_dense_body,
        grid=(n // tr,),
        in_specs=[
            pl.BlockSpec(memory_space=pltpu.SMEM),
            pl.BlockSpec((tr, SWA_WIDTH), row),
            pl.BlockSpec((tr, SWA_KV_DUP_W), row),
            pl.BlockSpec((tr, SWA_KV_DUP_W), row),
            pl.BlockSpec((tr, SWA_WIDTH), row),
        ],
        out_specs=pl.BlockSpec((tr, SWA_WIDTH), row),
        out_shape=jax.ShapeDtypeStruct((n, SWA_WIDTH), BF16),
        compiler_params=_params(),
        name="swa_attn_context",
    )(sink, q, kd, vd, gs)


SWA_Q_TILE = 1024
SWA_SPAN = 3 * SWA_WINDOW


def _swa_band_body(sink_ref, q_ref, kd_ref, vd_ref, ck_ref, cv_ref, gs_ref, o_ref, kx_ref, vx_ref):
    w = SWA_WINDOW
    nsub = q_ref.shape[0] // w
    step = pl.program_id(1)

    @pl.when(step == 0)
    def _():
        _dup_heads(ck_ref[0, 0], kx_ref)
        _dup_heads(cv_ref[0, 0], vx_ref)

    r = lax.broadcasted_iota(jnp.int32, (w, SWA_SPAN), 0)
    c = lax.broadcasted_iota(jnp.int32, (w, SWA_SPAN), 1)

    def block(sb, carry):
        qbase = (step * nsub + sb) * w
        start = pl.multiple_of(jnp.clip(qbase - w, 0, DEC_SEQ - SWA_SPAN), w)
        rel = c - r + (start - qbase)
        bias = jnp.where(jnp.abs(rel) <= w, 0.0, NEG_INF).astype(F32)
        bias = jnp.concatenate([bias] * SWA_GROUP, axis=0)
        rows = pl.ds(pl.multiple_of(sb * w, w), w)
        span = pl.ds(start, SWA_SPAN)
        for kh in range(SWA_KV_HEADS):
            kv = slice(kh * LANES, (kh + 1) * LANES)
            ta = slice(2 * kh * LANES, (2 * kh + 1) * LANES)
            tb = slice((2 * kh + 1) * LANES, (2 * kh + 2) * LANES)
            sinks = [sink_ref[SWA_GROUP * kh + e] for e in range(SWA_GROUP)]
            parts = [(kd_ref[span, kv], vd_ref[span, kv], bias), (kx_ref[:, kv], vx_ref[:, kv], None)]
            oa, ob = _swa_group_attend(q_ref[rows, ta], q_ref[rows, tb], parts, sinks)
            o_ref[rows, ta] = (oa * gs_ref[rows, ta].astype(F32)).astype(BF16)
            o_ref[rows, tb] = (ob * gs_ref[rows, tb].astype(F32)).astype(BF16)
        return carry

    lax.fori_loop(0, nsub, block, 0, unroll=True)


def _swa_band_call(sink, q, kd, vd, cache_k, cache_v, slot, gs):
    tq = SWA_Q_TILE
    nq = DEC_SEQ // tq
    qrow = lambda b, i: (b * nq + i, 0)
    brow = lambda b, i: (b, 0)
    cache = pl.BlockSpec((1, 1, PAST_LEN, SWA_KV_W), lambda b, i: (b, slot, 0, 0))
    return pl.pallas_call(
        _swa_band_body,
        grid=(DEC_BATCH, nq),
        in_specs=[
            pl.BlockSpec(memory_space=pltpu.SMEM),
            pl.BlockSpec((tq, SWA_WIDTH), qrow),
            pl.BlockSpec((DEC_SEQ, SWA_KV_DUP_W), brow),
            pl.BlockSpec((DEC_SEQ, SWA_KV_DUP_W), brow),
            cache,
            cache,
            pl.BlockSpec((tq, SWA_WIDTH), qrow),
        ],
        out_specs=pl.BlockSpec((tq, SWA_WIDTH), qrow),
        out_shape=jax.ShapeDtypeStruct((DEC_BATCH * DEC_SEQ, SWA_WIDTH), BF16),
        scratch_shapes=[pltpu.VMEM((PAST_LEN, SWA_KV_DUP_W), BF16)] * 2,
        compiler_params=_params2(),
        name="swa_attn_latent",
    )(sink, q, kd, vd, cache_k, cache_v, gs)


def _rope_tables(rope_dims, lane_period, lane_off):
    n = rope_dims // 2
    hf = n // 2
    t = jnp.arange(DEC_SEQ)
    pos = jnp.stack([(t // GRID_W).astype(F32), (t % GRID_W).astype(F32)], axis=0)
    inv = ROPE_BASE ** (-jnp.arange(0, n, 2, dtype=F32) / n)
    ang = pos[:, :, None] * inv[None, None, :]
    cos = jnp.cos(ang)
    sin = jnp.sin(ang)
    cos_d = jnp.concatenate([cos, cos], axis=-1)
    sin_d = jnp.concatenate([-sin, sin], axis=-1)
    cos_r = jnp.concatenate([cos_d[0], cos_d[1]], axis=-1)
    sin_r = jnp.concatenate([sin_d[0], sin_d[1]], axis=-1)
    reps = LANES // lane_period
    pad = ((0, 0), (lane_off, lane_period - lane_off - rope_dims))
    cos_p = jnp.tile(jnp.pad(cos_r, pad, constant_values=1.0), (1, reps))
    sin_p = jnp.tile(jnp.pad(sin_r, pad), (1, reps))
    return cos_p, sin_p


def _prep_mla(i, mla_w_in, mla_g_qn, mla_g_kvn, mla_w_uq, mla_w_ukv, pool_w, pool_scale, mixa_w_out):
    w_in = mla_w_in[i]
    i1 = MLA_Q_RANK
    i2 = i1 + MLA_KV_RANK
    i3 = i2 + MLA_ROPE
    kr = jnp.pad(w_in[:, i2:i3], ((0, 0), (MLA_NOPE, LANES - MLA_NOPE - MLA_ROPE)))
    w_in_r = jnp.concatenate([w_in[:, :i2], kr, w_in[:, i3:]], axis=1).astype(BF16)
    dq = MLA_NOPE + MLA_ROPE
    w_uq = jnp.pad(mla_w_uq[i].reshape(MLA_Q_RANK, MLA_HEADS, dq),
                   ((0, 0), (0, 0), (0, MLA_HEAD_PAD - dq))).reshape(MLA_Q_RANK, MLA_QK_W).astype(BF16)
    ukv = mla_w_ukv[i].reshape(MLA_KV_RANK, MLA_HEADS, MLA_NOPE + MLA_V)
    w_uk = jnp.pad(ukv[:, :, :MLA_NOPE], ((0, 0), (0, 0), (0, MLA_HEAD_PAD - MLA_NOPE)))
    w_uk = w_uk.reshape(MLA_KV_RANK, MLA_QK_W).astype(BF16)
    w_uv = ukv[:, :, MLA_NOPE:].reshape(MLA_KV_RANK, MLA_WIDTH).astype(BF16)
    return {
        "w_in": w_in_r, "w_uq": w_uq, "w_uk": w_uk, "w_uv": w_uv,
        "g_qn": mla_g_qn[i].reshape(1, -1), "g_kvn": mla_g_kvn[i].reshape(1, -1),
        "w_pool": pool_w[i].astype(BF16), "s_pool": pool_scale[i].reshape(1, -1),
        "w_out": mixa_w_out[i].astype(BF16),
    }


def kernel(x_prompt, x_sample, cache_ckv, cache_krope, cache_k, cache_v, c, c_ctx, ada_w, ada_b, norm_pre, norm_post, mla_w_in, mla_g_qn, mla_g_kvn, mla_w_uq, mla_w_ukv, pool_w, pool_scale, mixa_w_out, swa_w_in, swa_sink, swa_w_out):
    n_p = BATCH * SEQ
    n_s = DEC_BATCH * DEC_SEQ
    xp = x_prompt.reshape(n_p, D_MODEL)
    xs = x_sample.reshape(n_s, D_MODEL)
    cvec = jnp.concatenate(
        [c_ctx[None], c, jnp.zeros((MOD_ROWS - 1 - DEC_BATCH, D_MODEL), F32)], axis=0)
    mods = _ada_call(cvec, ada_w, ada_b).reshape(DEPTH, MOD_ROWS, 3, D_MODEL)
    mla_tabs = _rope_tables(MLA_ROPE, LANES, MLA_NOPE)
    swa_tabs = _rope_tables(SWA_HEAD_DIM, SWA_HEAD_DIM, 0)
    ck_view = cache_k.reshape(DEC_BATCH, DEPTH // 2, PAST_LEN, SWA_KV_W)
    cv_view = cache_v.reshape(DEC_BATCH, DEPTH // 2, PAST_LEN, SWA_KV_W)
    mla_state = swa_state = ()
    pend_p = pend_s = None

    def stage(x, latent, out_part, in_part):
        res = list(_stage_call(x, latent, out_part, in_part))
        if out_part is not None:
            return res[0], res[1:]
        return x, res

    for l in range(DEPTH):
        i = l // 2
        mod = mods[l]
        gpre = norm_pre[l].reshape(1, D_MODEL)
        gpost = norm_post[l].reshape(1, D_MODEL)
        if l % 2 == 0:
            w = _prep_mla(i, mla_w_in, mla_g_qn, mla_g_kvn, mla_w_uq, mla_w_ukv, pool_w, pool_scale,
                          mixa_w_out)
            xp, (qp, kcp, vp, gap, vpp, gbp, *mla_state) = stage(
                xp, False, pend_p, _mla_in_part(n_p, mod, gpre, w, False, None, i, mla_state))
            xs, (qs, kcs, vs, gas, vps, gbs) = stage(
                xs, True, pend_s, _mla_in_part(n_s, mod, gpre, w, True, mla_tabs))
            ctx_kr = jnp.pad(cache_krope[:, i].reshape(DEC_BATCH * PAST_LEN, MLA_ROPE),
                             ((0, 0), (MLA_NOPE, LANES - MLA_NOPE - MLA_ROPE)))
            ctx = (cache_ckv, i, ctx_kr, w["w_uk"], w["w_uv"])
            ap = _mla_attn_call(qp, kcp, vp, gap, BATCH, SEQ)
            as_ = _mla_attn_call(qs, kcs, vs, gas, DEC_BATCH, DEC_SEQ, ctx)
            pend_p = _mla_out_part(n_p, ap, vpp, gbp, mod, gpost, w, False)
            pend_s = _mla_out_part(n_s, as_, vps, gbs, mod, gpost, w, True)
        else:
            w_in = swa_w_in[i].astype(BF16)
            w_out = swa_w_out[i].astype(BF16)
            sink = swa_sink[i]
            xp, (qp, kdp, vdp, gsp, *swa_state) = stage(
                xp, False, pend_p, _swa_in_part(n_p, mod, gpre, w_in, False, None, i, swa_state))
            xs, (qs, kds, vds, gss) = stage(
                xs, True, pend_s, _swa_in_part(n_s, mod, gpre, w_in, True, swa_tabs))

            ap = _swa_dense_call(sink, qp, kdp, vdp, gsp)
            as_ = _swa_band_call(sink, qs, kds, vds, ck_view, cv_view, i, gss)
            pend_p = _swa_out_part(ap, mod, gpost, w_out, False)
            pend_s = _swa_out_part(as_, mod, gpost, w_out, True)
    xp, _ = stage(xp, False, pend_p, None)
    xs, _ = stage(xs, True, pend_s, None)
    return (
        xp.reshape(BATCH, SEQ, D_MODEL),
        xs.reshape(DEC_BATCH, DEC_SEQ, D_MODEL),
        mla_state[0].reshape(BATCH, DEPTH // 2, SEQ, MLA_KV_RANK),
        mla_state[1].reshape(BATCH, DEPTH // 2, SEQ, MLA_ROPE),
        swa_state[0].reshape(BATCH, DEPTH // 2, SEQ, SWA_KV_HEADS, SWA_HEAD_DIM),
        swa_state[1].reshape(BATCH, DEPTH // 2, SEQ, SWA_KV_HEADS, SWA_HEAD_DIM),
    )
```

```python
import functools

import numpy as np
import jax
import jax.numpy as jnp
from jax import lax
from jax.experimental import pallas as pl
from jax.experimental.pallas import tpu as pltpu

F32 = jnp.float32
BF16 = jnp.bfloat16

D_MODEL = 1024
BATCH = 32
SEQ = 256
DEPTH = 4
DEC_BATCH = 8
DEC_SEQ = 2048
PAST_LEN = 256
GRID_W = 64
ROPE_BASE = 10000.0
EPS = 1e-6
NEG_INF = -1e30

MLA_HEADS = 8
MLA_NOPE = 64
MLA_ROPE = 32
MLA_V = 64
MLA_Q_RANK = 384
MLA_KV_RANK = 256
MLA_WIDTH = MLA_HEADS * MLA_V
MLA_SCALE = (MLA_NOPE + MLA_ROPE) ** -0.5
POOL_WINDOWS = (2, 4, 8, 16)
POOL_GROUPS = 4
POOL_GROUP_W = D_MODEL // 8
POOL_WIDTH = POOL_GROUPS * POOL_GROUP_W
SWA_HEADS = 16
SWA_KV_HEADS = 4
SWA_HEAD_DIM = 64
SWA_WINDOW = 128
SWA_WIDTH = SWA_HEADS * SWA_HEAD_DIM
SWA_KV_W = SWA_KV_HEADS * SWA_HEAD_DIM
SWA_SCALE = SWA_HEAD_DIM ** -0.5

LANES = 128
HALF = LANES // 2
BF16_SUBLANES = 16
MOD_ROWS = 16
TOK_TILE = 1024
LOG2E = 1.4426950408889634
POOL_BLOCK = 128
POOL_HALO = 64
VMEM_LIMIT = 60 * 1024 * 1024

_CQ = (0, MLA_Q_RANK)
_CKV = (_CQ[1], _CQ[1] + MLA_KV_RANK)
_KR = (_CKV[1], _CKV[1] + LANES)
_GA = (_KR[1], _KR[1] + MLA_WIDTH)
_VP = (_GA[1], _GA[1] + POOL_WIDTH)
_GB = (_VP[1], _VP[1] + POOL_WIDTH)
MLA_IN_COLS = _GB[1]
MLA_HEAD_PAD = LANES
MLA_QK_W = MLA_HEADS * MLA_HEAD_PAD
MLA_Q_TILE = 1024
CTX_SEQS_PER_STEP = 4


def _params():
    return pltpu.CompilerParams(dimension_semantics=("arbitrary",), vmem_limit_bytes=VMEM_LIMIT)


def _params2():
    return pltpu.CompilerParams(dimension_semantics=("arbitrary", "arbitrary"), vmem_limit_bytes=VMEM_LIMIT)


def _dot(a, b):
    return jnp.dot(a, b, preferred_element_type=F32)


def _dot_nt(a, b):
    return lax.dot_general(a, b, (((1,), (1,)), ((), ())), preferred_element_type=F32)


def _silu(x):
    return x / (1.0 + jnp.exp(-x))


def _rms(x):
    return x * lax.rsqrt(jnp.mean(x * x, axis=-1, keepdims=True) + EPS)


def _lane_iota(shape):
    return lax.broadcasted_iota(jnp.int32, shape, len(shape) - 1)


def _full(shape):
    nd = len(shape)
    return pl.BlockSpec(shape, lambda *_: (0,) * nd, pipeline_mode=pl.Buffered(1))


def _ada_body(c_ref, w_ref, b_ref, o_ref):
    s = _silu(c_ref[...]).astype(BF16)
    o_ref[0] = _dot(s, w_ref[0].astype(BF16)) + b_ref[0]


def _ada_call(cvec, ada_w, ada_b):
    tn = 768
    return pl.pallas_call(
        _ada_body,
        grid=(DEPTH, 3 * D_MODEL // tn),
        in_specs=[
            pl.BlockSpec((MOD_ROWS, D_MODEL), lambda l, j: (0, 0)),
            pl.BlockSpec((1, D_MODEL, tn), lambda l, j: (l, 0, j)),
            pl.BlockSpec((1, 1, tn), lambda l, j: (l, 0, j)),
        ],
        out_specs=pl.BlockSpec((1, MOD_ROWS, tn), lambda l, j: (l, 0, j)),
        out_shape=jax.ShapeDtypeStruct((DEPTH, MOD_ROWS, 3 * D_MODEL), F32),
        compiler_params=_params2(),
        name="ada_mod",
    )(cvec, ada_w, ada_b.reshape(DEPTH, 1, 3 * D_MODEL))


def _modulated(x, mod_ref, g_ref):
    m = mod_ref[0]
    gain = g_ref[...] * (1.0 + m[1:2, :])
    return (_rms(x) * gain + m[0:1, :]).astype(BF16)


def _rope_block(b, cos, sin, partner):
    first = (_lane_iota(b.shape) & partner) == 0
    sw = jnp.where(first, pltpu.roll(b, LANES - partner, 1), pltpu.roll(b, partner, 1))
    return b * cos + sw * sin


def _emit_states(slot, values, prev_refs, out_refs):
    for idx, (val, out_ref) in enumerate(zip(values, out_refs)):
        if slot == 0:
            out_ref[...] = val
        else:
            prev_ref = prev_refs[idx]
            nb = val.shape[0] // SEQ
            out_ref[:, 0:SEQ, :] = prev_ref[...].reshape(nb, SEQ, val.shape[1])
            out_ref[:, SEQ:2 * SEQ, :] = val.reshape(nb, SEQ, val.shape[1])


def _state_specs(n, widths, slot, prev):
    tm = TOK_TILE
    if slot == 0:
        specs = [pl.BlockSpec((tm, c), lambda i: (i, 0)) for c in widths]
        return [], [], specs, [jax.ShapeDtypeStruct((n, c), F32) for c in widths]
    nb = tm // SEQ
    ins = [pl.BlockSpec((tm, c), lambda i: (i, 0)) for c in widths]
    outs = [pl.BlockSpec((nb, 2 * SEQ, c), lambda i: (i, 0, 0)) for c in widths]
    return ins, list(prev), outs, [jax.ShapeDtypeStruct((n // SEQ, 2 * SEQ, c), F32) for c in widths]


def _mod_row_fn(latent, tile):
    if latent:
        per_seq = DEC_SEQ // tile
        return lambda i: (1 + i // per_seq, 0, 0)
    return lambda i: (0, 0, 0)


def _mla_in_compute(x, ins, refs, *, rope, state):
    mod_ref, g_ref, win_ref, gq_ref, gkv_ref, wuq_ref, wuk_ref, wuv_ref = ins[:8]
    if rope:
        cos_ref, sin_ref = ins[8:10]
    q_ref, kc_ref, v_ref, ga_ref, vp_ref, gb_ref = refs[:6]
    h = _modulated(x, mod_ref, g_ref)

    def proj(c):
        return _dot(h, win_ref[:, c[0]:c[1]])

    cqn = (_rms(proj(_CQ)) * gq_ref[...]).astype(BF16)
    q = _dot(cqn, wuq_ref[...]) * (MLA_SCALE * LOG2E)
    ckvn = _rms(proj(_CKV)) * gkv_ref[...]
    ckvb = ckvn.astype(BF16)
    kr = proj(_KR)
    if state is not None:
        kr_state = pltpu.roll(kr, LANES - MLA_NOPE, 1)[:, :MLA_ROPE]
        _emit_states(state, (ckvn, kr_state), ins[8:10], refs[6:8])
    if rope:
        cos = cos_ref[...]
        sin = sin_ref[...]
        kr = _rope_block(kr, cos, sin, MLA_ROPE // 4)
    kc = _dot(ckvb, wuk_ref[...])
    for hh in range(MLA_HEADS):
        blk = slice(hh * MLA_HEAD_PAD, (hh + 1) * MLA_HEAD_PAD)
        qb = q[:, blk]
        if rope:
            qb = _rope_block(qb, cos, sin, MLA_ROPE // 4)
        q_ref[:, blk] = qb.astype(BF16)
        kc_ref[:, blk] = (kc[:, blk] + kr).astype(BF16)
    v_ref[...] = _dot(ckvb, wuv_ref[...]).astype(BF16)
    ga_ref[...] = _silu(proj(_GA)).astype(BF16)
    vp_ref[...] = proj(_VP).astype(BF16)
    gb_ref[...] = _silu(proj(_GB)).astype(BF16)


class _Part:
    def __init__(self, name, compute, in_specs, args, out_specs, out_shape):
        self.name = name
        self.compute = compute
        self.in_specs = in_specs
        self.args = args
        self.out_specs = out_specs
        self.out_shape = out_shape


def _mla_in_part(n, mod, gpre, w, latent, rope_tabs, slot=None, prev=()):
    tm = TOK_TILE
    row = lambda i: (i, 0)
    in_specs = [
        pl.BlockSpec((1, 3, D_MODEL), _mod_row_fn(latent, tm)),
        _full((1, D_MODEL)),
        _full((D_MODEL, MLA_IN_COLS)),
        _full((1, MLA_Q_RANK)),
        _full((1, MLA_KV_RANK)),
        _full((MLA_Q_RANK, MLA_QK_W)),
        _full((MLA_KV_RANK, MLA_QK_W)),
        _full((MLA_KV_RANK, MLA_WIDTH)),
    ]
    args = [mod, gpre, w["w_in"], w["g_qn"], w["g_kvn"], w["w_uq"], w["w_uk"], w["w_uv"]]
    if latent:
        per_seq = DEC_SEQ // tm
        in_specs += [pl.BlockSpec((tm, LANES), lambda i: (i % per_seq, 0))] * 2
        args += list(rope_tabs)
    widths = [MLA_QK_W, MLA_QK_W, MLA_WIDTH, MLA_WIDTH, POOL_WIDTH, POOL_WIDTH]
    out_shape = [jax.ShapeDtypeStruct((n, c), BF16) for c in widths]
    out_specs = [pl.BlockSpec((tm, c), row) for c in widths]
    if not latent:
        s_in, s_args, s_out, s_shape = _state_specs(n, (MLA_KV_RANK, MLA_ROPE), slot, prev)
        in_specs += s_in
        args += s_args
        out_specs += s_out
        out_shape += s_shape
    compute = functools.partial(_mla_in_compute, rope=latent, state=slot)
    return _Part("mla_in", compute, in_specs, args, out_specs, out_shape)


def _stage_body(*refs, out_part, in_part):
    x_ref = refs[0]
    n_out_in = len(out_part.in_specs) if out_part else 0
    n_in_in = len(in_part.in_specs) if in_part else 0
    out_ins = refs[1:1 + n_out_in]
    in_ins = refs[1 + n_out_in:1 + n_out_in + n_in_in]
    outs = refs[1 + n_out_in + n_in_in:]
    x = x_ref[...]
    if out_part:
        x = out_part.compute(x, out_ins)
        outs[0][...] = x
        outs = outs[1:]
    if in_part:
        in_part.compute(x, in_ins, outs)


def _stage_call(x, latent, out_part, in_part):
    n = x.shape[0]
    tm = TOK_TILE
    row = lambda i: (i, 0)
    in_specs = [pl.BlockSpec((tm, D_MODEL), row)]
    args = [x]
    out_specs, out_shape = [], []
    names = []
    if out_part:
        in_specs += out_part.in_specs
        args += out_part.args
        out_specs.append(pl.BlockSpec((tm, D_MODEL), row))
        out_shape.append(jax.ShapeDtypeStruct((n, D_MODEL), F32))
        names.append(out_part.name)
    if in_part:
        in_specs += in_part.in_specs
        args += in_part.args
        out_specs += in_part.out_specs
        out_shape += in_part.out_shape
        names.append(in_part.name)
    names.append("latent" if latent else "context")
    return pl.pallas_call(
        functools.partial(_stage_body, out_part=out_part, in_part=in_part),
        grid=(n // tm,),
        in_specs=in_specs,
        out_specs=out_specs,
        out_shape=out_shape,
        compiler_params=_params(),
        name="_".join(names),
    )(*args)


def _mla_attn_body(*refs, has_ctx, nseq):
    if has_ctx:
        q_ref, kc_ref, v_ref, ckv_ref, kr_ref, wuk_ref, wuv_ref, ga_ref, o_ref, va_ref, kcc_ref, vca_ref = refs
    else:
        q_ref, kc_ref, v_ref, ga_ref, o_ref, va_ref = refs

    def augment(v, dst):
        for j in range(MLA_HEADS // 2):
            dst[:, 2 * j * LANES:(2 * j + 1) * LANES] = v[:, j * LANES:(j + 1) * LANES]
            dst[:, (2 * j + 1) * LANES:(2 * j + 2) * LANES] = jnp.ones((v.shape[0], LANES), BF16)

    @pl.when(pl.program_id(1) == 0)
    def _():
        augment(v_ref[...], va_ref)
        if has_ctx:
            ckvb = ckv_ref[0, 0].astype(BF16)
            kr = kr_ref[...]
            kc = _dot(ckvb, wuk_ref[...])
            for hh in range(MLA_HEADS):
                blk = slice(hh * MLA_HEAD_PAD, (hh + 1) * MLA_HEAD_PAD)
                kcc_ref[:, blk] = (kc[:, blk] + kr).astype(BF16)
            augment(_dot(ckvb, wuv_ref[...]).astype(BF16), vca_ref)

    def lane_tile_max(t, s):
        for cb in range(s.shape[1] // LANES):
            blk = s[:, cb * LANES:(cb + 1) * LANES]
            t = blk if t is None else jnp.maximum(t, blk)
        return t

    qrows = q_ref.shape[0] // nseq
    krows = kc_ref.shape[0] // nseq
    low = _lane_iota((qrows, LANES)) < HALF
    for si in range(nseq):
        qr = slice(si * qrows, (si + 1) * qrows)
        kr_ = slice(si * krows, (si + 1) * krows)
        for j in range(MLA_HEADS // 2):
            pair = slice(j * LANES, (j + 1) * LANES)
            vaug = slice(2 * j * LANES, (2 * j + 2) * LANES)
            outs = []
            for hh in (2 * j, 2 * j + 1):
                blk = slice(hh * MLA_HEAD_PAD, (hh + 1) * MLA_HEAD_PAD)
                qh = q_ref[qr, blk]
                s = _dot_nt(qh, kc_ref[kr_, blk])
                t = lane_tile_max(None, s)
                if has_ctx:
                    sc = _dot_nt(qh, kcc_ref[:, blk])
                    t = lane_tile_max(t, sc)
                m = jnp.max(t, axis=-1, keepdims=True)
                acc = _dot(jnp.exp2(s - m).astype(BF16), va_ref[kr_, vaug])
                if has_ctx:
                    acc = acc + _dot(jnp.exp2(sc - m).astype(BF16), vca_ref[:, vaug])
                outs.append(acc[:, :LANES] / acc[:, LANES:])
            o_pair = jnp.where(low, outs[0], outs[1])
            o_ref[qr, pair] = o_pair.astype(BF16) * ga_ref[qr, pair]


def _mla_attn_call(q, kc, v, ga, nbatch, seq, ctx=None):
    nseq = 1 if ctx is not None else CTX_SEQS_PER_STEP
    tq = min(seq, MLA_Q_TILE) * nseq
    seq = seq * nseq
    nbatch = nbatch // nseq
    scratch = [pltpu.VMEM((seq, 2 * MLA_WIDTH), BF16)]
    if ctx is not None:
        scratch += [pltpu.VMEM((PAST_LEN, MLA_QK_W), BF16), pltpu.VMEM((PAST_LEN, 2 * MLA_WIDTH), BF16)]
    nq = seq // tq
    qrow = lambda b, i: (b * nq + i, 0)
    brow = lambda b, i: (b, 0)
    in_specs = [
        pl.BlockSpec((tq, MLA_QK_W), qrow),
        pl.BlockSpec((seq, MLA_QK_W), brow),
        pl.BlockSpec((seq, MLA_WIDTH), brow),
    ]
    args = [q, kc, v]
    if ctx is not None:
        cache, slot, ctx_kr, w_uk, w_uv = ctx
        in_specs += [
            pl.BlockSpec((1, 1, PAST_LEN, MLA_KV_RANK), lambda b, i: (b, slot, 0, 0)),
            pl.BlockSpec((PAST_LEN, LANES), brow),
            _full((MLA_KV_RANK, MLA_QK_W)),
            _full((MLA_KV_RANK, MLA_WIDTH)),
        ]
        args += [cache, ctx_kr, w_uk, w_uv]
    in_specs.append(pl.BlockSpec((tq, MLA_WIDTH), qrow))
    args.append(ga)
    return pl.pallas_call(
        functools.partial(_mla_attn_body, has_ctx=ctx is not None, nseq=nseq),
        grid=(nbatch, nq),
        in_specs=in_specs,
        out_specs=pl.BlockSpec((tq, MLA_WIDTH), qrow),
        out_shape=jax.ShapeDtypeStruct((nbatch * seq, MLA_WIDTH), BF16),
        scratch_shapes=scratch,
        compiler_params=_params2(),
        name="mla_attn_latent" if ctx is not None else "mla_attn_context",
    )(*args)


def _post_residual(out, x, mod_ref, gpost_ref):
    m = mod_ref[0]
    return x + _rms(out) * (m[2:3, :] * gpost_ref[...])


def _mla_out_compute(x, ins, *, seq):
    a_ref, vp_ref, vprev_ref, vnext_ref, gb_ref, pm_ref, wp_ref, sp_ref, wo_ref, mod_ref, gpost_ref = ins
    tm = vp_ref.shape[0]
    cur = vp_ref[...]
    slab = jnp.concatenate([vprev_ref[...], cur, vnext_ref[...]], axis=0)
    srow = lax.broadcasted_iota(jnp.int32, (POOL_BLOCK + 2 * POOL_HALO, POOL_WIDTH), 0)
    trow = lax.broadcasted_iota(jnp.int32, (POOL_BLOCK, LANES), 0)
    zero = jnp.zeros((POOL_BLOCK + 2 * POOL_HALO, POOL_WIDTH), BF16)
    pooled = [[] for _ in POOL_WINDOWS]
    for sb in range(tm // POOL_BLOCK):
        pos0 = (pl.program_id(0) * tm + sb * POOL_BLOCK) % seq
        lo_cut = jnp.where(pos0 == 0, POOL_HALO, 0)
        hi_cut = jnp.where(pos0 == seq - POOL_BLOCK, POOL_BLOCK + POOL_HALO, POOL_BLOCK + 2 * POOL_HALO)
        chunk = slab[sb * POOL_BLOCK:(sb + 1) * POOL_BLOCK + 2 * POOL_HALO]
        chunk = jnp.where(srow >= lo_cut, chunk, zero)
        chunk = jnp.where(srow < hi_cut, chunk, zero)
        t = pos0 + trow
        for g, w in enumerate(POOL_WINDOWS):
            cols = slice(g * POOL_GROUP_W, (g + 1) * POOL_GROUP_W)
            tot = _dot(pm_ref[g], chunk[:, cols])
            cnt = jnp.minimum(t + w // 2, seq) - jnp.maximum(t - w // 2, 0)
            own = cur[sb * POOL_BLOCK:(sb + 1) * POOL_BLOCK, cols].astype(F32)
            pooled[g].append((tot / cnt.astype(F32) - own).astype(BF16))
    bs = []
    for g in range(POOL_GROUPS):
        cols = slice(g * POOL_GROUP_W, (g + 1) * POOL_GROUP_W)
        y = _dot(jnp.concatenate(pooled[g], axis=0), wp_ref[g]) * sp_ref[:, cols]
        bs.append(y.astype(BF16) * gb_ref[:, cols])
    b = jnp.concatenate(bs, axis=1)
    out = _dot(a_ref[...], wo_ref[0:MLA_WIDTH, :]) + _dot(b, wo_ref[MLA_WIDTH:, :])
    return _post_residual(out, x, mod_ref, gpost_ref)


def _pool_masks():
    r = np.arange(POOL_BLOCK)[:, None]
    c = np.arange(POOL_BLOCK + 2 * POOL_HALO)[None, :]
    rel = c - POOL_HALO - r
    ms = [((rel >= -(w // 2)) & (rel <= w // 2 - 1)) for w in POOL_WINDOWS]
    return jnp.asarray(np.stack(ms).astype(np.float32), dtype=BF16)


def _mla_out_part(n, a, vp, gb, mod, gpost, w, latent):
    tm = TOK_TILE
    seq = DEC_SEQ if latent else SEQ
    row = lambda i: (i, 0)
    hb = tm // POOL_HALO
    nhb = n // POOL_HALO
    in_specs = [
        pl.BlockSpec((tm, MLA_WIDTH), row),
        pl.BlockSpec((tm, POOL_WIDTH), row),
        pl.BlockSpec((POOL_HALO, POOL_WIDTH), lambda i: (jnp.maximum(i * hb - 1, 0), 0)),
        pl.BlockSpec((POOL_HALO, POOL_WIDTH), lambda i: (jnp.minimum((i + 1) * hb, nhb - 1), 0)),
        pl.BlockSpec((tm, POOL_WIDTH), row),
        _full((POOL_GROUPS, POOL_BLOCK, POOL_BLOCK + 2 * POOL_HALO)),
        _full((POOL_GROUPS, POOL_GROUP_W, POOL_GROUP_W)),
        _full((1, POOL_WIDTH)),
        _full((MLA_WIDTH + POOL_WIDTH, D_MODEL)),
        pl.BlockSpec((1, 3, D_MODEL), _mod_row_fn(latent, tm)),
        _full((1, D_MODEL)),
    ]
    args = [a, vp, vp, vp, gb, _pool_masks(), w["w_pool"], w["s_pool"], w["w_out"], mod, gpost]
    return _Part("mla_out", functools.partial(_mla_out_compute, seq=seq), in_specs, args, None, None)


def _swa_out_compute(x, ins):
    a_ref, wo_ref, mod_ref, gpost_ref = ins
    return _post_residual(_dot(a_ref[...], wo_ref[...]), x, mod_ref, gpost_ref)


def _swa_out_part(a, mod, gpost, w_out, latent):
    tm = TOK_TILE
    in_specs = [
        pl.BlockSpec((tm, SWA_WIDTH), lambda i: (i, 0)),
        _full((SWA_WIDTH, D_MODEL)),
        pl.BlockSpec((1, 3, D_MODEL), _mod_row_fn(latent, tm)),
        _full((1, D_MODEL)),
    ]
    return _Part("swa_out", _swa_out_compute, in_specs, [a, w_out, mod, gpost], None, None)


_SQ = (0, SWA_WIDTH)
_SK = (_SQ[1], _SQ[1] + SWA_KV_W)
_SV = (_SK[1], _SK[1] + SWA_KV_W)
_SG = (_SV[1], _SV[1] + SWA_WIDTH)
SWA_IN_COLS = _SG[1]
SWA_KV_DUP_W = 2 * SWA_KV_W


def _dup_heads(x, o_ref):
    for b in range(SWA_KV_W // LANES):
        blk = x[:, b * LANES:(b + 1) * LANES]
        rot = pltpu.roll(blk, HALF, 1)
        low = _lane_iota(blk.shape) < HALF
        o_ref[:, (2 * b) * LANES:(2 * b + 1) * LANES] = jnp.where(low, blk, rot).astype(BF16)
        o_ref[:, (2 * b + 1) * LANES:(2 * b + 2) * LANES] = jnp.where(low, rot, blk).astype(BF16)


def _swa_in_compute(x, ins, refs, *, rope, state):
    mod_ref, g_ref, win_ref = ins[:3]
    if rope:
        cos_ref, sin_ref = ins[3:5]
    q_ref, kd_ref, vd_ref, gs_ref = refs[:4]
    h = _modulated(x, mod_ref, g_ref)

    def proj(c):
        return _dot(h, win_ref[:, c[0]:c[1]])

    q = proj(_SQ) * (SWA_SCALE * LOG2E)
    k = proj(_SK)
    v = proj(_SV)
    if state is not None:
        _emit_states(state, (k, v), ins[3:5], refs[4:6])
    if rope:
        cos = cos_ref[...]
        sin = sin_ref[...]
        for b in range(SWA_WIDTH // LANES):
            blk = slice(b * LANES, (b + 1) * LANES)
            q_ref[:, blk] = _rope_block(q[:, blk], cos, sin, SWA_HEAD_DIM // 4).astype(BF16)
        k = jnp.concatenate(
            [_rope_block(k[:, b * LANES:(b + 1) * LANES], cos, sin, SWA_HEAD_DIM // 4)
             for b in range(SWA_KV_W // LANES)], axis=1)
    else:
        q_ref[...] = q.astype(BF16)
    _dup_heads(k, kd_ref)
    _dup_heads(v, vd_ref)
    gs_ref[...] = _silu(proj(_SG)).astype(BF16)


def _swa_in_part(n, mod, gpre, w_in, latent, rope_tabs, slot=None, prev=()):
    tm = TOK_TILE
    row = lambda i: (i, 0)
    in_specs = [
        pl.BlockSpec((1, 3, D_MODEL), _mod_row_fn(latent, tm)),
        _full((1, D_MODEL)),
        _full((D_MODEL, SWA_IN_COLS)),
    ]
    args = [mod, gpre, w_in]
    if latent:
        per_seq = DEC_SEQ // tm
        in_specs += [pl.BlockSpec((tm, LANES), lambda i: (i % per_seq, 0))] * 2
        args += list(rope_tabs)
    widths = [SWA_WIDTH, SWA_KV_DUP_W, SWA_KV_DUP_W, SWA_WIDTH]
    out_shape = [jax.ShapeDtypeStruct((n, c), BF16) for c in widths]
    out_specs = [pl.BlockSpec((tm, c), row) for c in widths]
    if not latent:
        s_in, s_args, s_out, s_shape = _state_specs(n, (SWA_KV_W, SWA_KV_W), slot, prev)
        in_specs += s_in
        args += s_args
        out_specs += s_out
        out_shape += s_shape
    compute = functools.partial(_swa_in_compute, rope=latent, state=slot)
    return _Part("swa_in", compute, in_specs, args, out_specs, out_shape)


SWA_GROUP = SWA_HEADS // SWA_KV_HEADS


def _swa_group_attend(qa, qb, parts, sinks):
    rows = qa.shape[0]
    low = _lane_iota(qa.shape) < HALF
    zero = jnp.zeros_like(qa)
    qst = jnp.concatenate(
        [jnp.where(low, qa, zero), jnp.where(low, zero, qa), jnp.where(low, qb, zero), jnp.where(low, zero, qb)],
        axis=0)
    sink_t = jnp.concatenate([jnp.full((rows, LANES), s * LOG2E, F32) for s in sinks], axis=0)
    scores = []
    t = sink_t
    for k, _, bias in parts:
        s = _dot_nt(qst, k)
        if bias is not None:
            s = s + bias
        scores.append(s)
        for cb in range(s.shape[1] // LANES):
            t = jnp.maximum(t, s[:, cb * LANES:(cb + 1) * LANES])
    m = jnp.max(t, axis=-1, keepdims=True)
    acc = None
    for s, (_, v, _) in zip(scores, parts):
        p = jnp.exp2(s - m).astype(BF16)
        pv = _dot(p, jnp.concatenate([v, jnp.ones_like(v)], axis=1))
        acc = pv if acc is None else acc + pv
    o = acc[:, :LANES] / (acc[:, LANES:] + jnp.exp2(sink_t - m))
    return (jnp.where(low, o[0:rows], o[rows:2 * rows]),
            jnp.where(low, o[2 * rows:3 * rows], o[3 * rows:4 * rows]))


def _swa_dense_body(sink_ref, q_ref, kd_ref, vd_ref, gs_ref, o_ref):
    for si in range(q_ref.shape[0] // SEQ):
        rows = slice(si * SEQ, (si + 1) * SEQ)
        for kh in range(SWA_KV_HEADS):
            kv = slice(kh * LANES, (kh + 1) * LANES)
            ta = slice(2 * kh * LANES, (2 * kh + 1) * LANES)
            tb = slice((2 * kh + 1) * LANES, (2 * kh + 2) * LANES)
            sinks = [sink_ref[SWA_GROUP * kh + e] for e in range(SWA_GROUP)]
            parts = [(kd_ref[rows, kv], vd_ref[rows, kv], None)]
            oa, ob = _swa_group_attend(q_ref[rows, ta], q_ref[rows, tb], parts, sinks)
            o_ref[rows, ta] = oa.astype(BF16) * gs_ref[rows, ta]
            o_ref[rows, tb] = ob.astype(BF16) * gs_ref[rows, tb]


def _swa_dense_call(sink, q, kd, vd, gs):
    n = q.shape[0]
    tr = CTX_SEQS_PER_STEP * SEQ
    row = lambda b: (b, 0)
    return pl.pallas_call(
        _swa_dense_body,
        grid=(n // tr,),
        in_specs=[
            pl.BlockSpec(memory_space=pltpu.SMEM),
            pl.BlockSpec((tr, SWA_WIDTH), row),
            pl.BlockSpec((tr, SWA_KV_DUP_W), row),
            pl.BlockSpec((tr, SWA_KV_DUP_W), row),
            pl.BlockSpec((tr, SWA_WIDTH), row),
        ],
        out_specs=pl.BlockSpec((tr, SWA_WIDTH), row),
        out_shape=jax.ShapeDtypeStruct((n, SWA_WIDTH), BF16),
        compiler_params=_params(),
        name="swa_attn_context",
    )(sink, q, kd, vd, gs)


SWA_Q_TILE = 1024
SWA_SPAN = 3 * SWA_WINDOW


def _swa_band_body(sink_ref, q_ref, kd_ref, vd_ref, ck_ref, cv_ref, gs_ref, o_ref, kx_ref, vx_ref):
    w = SWA_WINDOW
    nsub = q_ref.shape[0] // w
    step = pl.program_id(1)

    @pl.when(step == 0)
    def _():
        _dup_heads(ck_ref[0, 0], kx_ref)
        _dup_heads(cv_ref[0, 0], vx_ref)

    r = lax.broadcasted_iota(jnp.int32, (w, SWA_SPAN), 0)
    c = lax.broadcasted_iota(jnp.int32, (w, SWA_SPAN), 1)

    def block(sb, carry):
        qbase = (step * nsub + sb) * w
        start = pl.multiple_of(jnp.clip(qbase - w, 0, DEC_SEQ - SWA_SPAN), w)
        rel = c - r + (start - qbase)
        bias = jnp.where(jnp.abs(rel) <= w, 0.0, NEG_INF).astype(F32)
        bias = jnp.concatenate([bias] * SWA_GROUP, axis=0)
        rows = pl.ds(pl.multiple_of(sb * w, w), w)
        span = pl.ds(start, SWA_SPAN)
        for kh in range(SWA_KV_HEADS):
            kv = slice(kh * LANES, (kh + 1) * LANES)
            ta = slice(2 * kh * LANES, (2 * kh + 1) * LANES)
            tb = slice((2 * kh + 1) * LANES, (2 * kh + 2) * LANES)
            sinks = [sink_ref[SWA_GROUP * kh + e] for e in range(SWA_GROUP)]
            parts = [(kd_ref[span, kv], vd_ref[span, kv], bias), (kx_ref[:, kv], vx_ref[:, kv], None)]
            oa, ob = _swa_group_attend(q_ref[rows, ta], q_ref[rows, tb], parts, sinks)
            o_ref[rows, ta] = oa.astype(BF16) * gs_ref[rows, ta]
            o_ref[rows, tb] = ob.astype(BF16) * gs_ref[rows, tb]
        return carry

    lax.fori_loop(0, nsub, block, 0, unroll=True)


def _swa_band_call(sink, q, kd, vd, cache_k, cache_v, slot, gs):
    tq = SWA_Q_TILE
    nq = DEC_SEQ // tq
    qrow = lambda b, i: (b * nq + i, 0)
    brow = lambda b, i: (b, 0)
    cache = pl.BlockSpec((1, 1, PAST_LEN, SWA_KV_W), lambda b, i: (b, slot, 0, 0))
    return pl.pallas_call(
        _swa_band_body,
        grid=(DEC_BATCH, nq),
        in_specs=[
            pl.BlockSpec(memory_space=pltpu.SMEM),
            pl.BlockSpec((tq, SWA_WIDTH), qrow),
            pl.BlockSpec((DEC_SEQ, SWA_KV_DUP_W), brow),
            pl.BlockSpec((DEC_SEQ, SWA_KV_DUP_W), brow),
            cache,
            cache,
            pl.BlockSpec((tq, SWA_WIDTH), qrow),
        ],
        out_specs=pl.BlockSpec((tq, SWA_WIDTH), qrow),
        out_shape=jax.ShapeDtypeStruct((DEC_BATCH * DEC_SEQ, SWA_WIDTH), BF16),
        scratch_shapes=[pltpu.VMEM((PAST_LEN, SWA_KV_DUP_W), BF16)] * 2,
        compiler_params=_params2(),
        name="swa_attn_latent",
    )(sink, q, kd, vd, cache_k, cache_v, gs)


def _rope_tables(rope_dims, lane_period, lane_off):
    n = rope_dims // 2
    hf = n // 2
    t = jnp.arange(DEC_SEQ)
    pos = jnp.stack([(t // GRID_W).astype(F32), (t % GRID_W).astype(F32)], axis=0)
    inv = ROPE_BASE ** (-jnp.arange(0, n, 2, dtype=F32) / n)
    ang = pos[:, :, None] * inv[None, None, :]
    cos = jnp.cos(ang)
    sin = jnp.sin(ang)
    cos_d = jnp.concatenate([cos, cos], axis=-1)
    sin_d = jnp.concatenate([-sin, sin], axis=-1)
    cos_r = jnp.concatenate([cos_d[0], cos_d[1]], axis=-1)
    sin_r = jnp.concatenate([sin_d[0], sin_d[1]], axis=-1)
    reps = LANES // lane_period
    pad = ((0, 0), (lane_off, lane_period - lane_off - rope_dims))
    cos_p = jnp.tile(jnp.pad(cos_r, pad, constant_values=1.0), (1, reps))
    sin_p = jnp.tile(jnp.pad(sin_r, pad), (1, reps))
    return cos_p, sin_p


def _prep_mla(i, mla_w_in, mla_g_qn, mla_g_kvn, mla_w_uq, mla_w_ukv, pool_w, pool_scale, mixa_w_out):
    w_in = mla_w_in[i]
    i1 = MLA_Q_RANK
    i2 = i1 + MLA_KV_RANK
    i3 = i2 + MLA_ROPE
    kr = jnp.pad(w_in[:, i2:i3], ((0, 0), (MLA_NOPE, LANES - MLA_NOPE - MLA_ROPE)))
    w_in_r = jnp.concatenate([w_in[:, :i2], kr, w_in[:, i3:]], axis=1).astype(BF16)
    dq = MLA_NOPE + MLA_ROPE
    w_uq = jnp.pad(mla_w_uq[i].reshape(MLA_Q_RANK, MLA_HEADS, dq),
                   ((0, 0), (0, 0), (0, MLA_HEAD_PAD - dq))).reshape(MLA_Q_RANK, MLA_QK_W).astype(BF16)
    ukv = mla_w_ukv[i].reshape(MLA_KV_RANK, MLA_HEADS, MLA_NOPE + MLA_V)
    w_uk = jnp.pad(ukv[:, :, :MLA_NOPE], ((0, 0), (0, 0), (0, MLA_HEAD_PAD - MLA_NOPE)))
    w_uk = w_uk.reshape(MLA_KV_RANK, MLA_QK_W).astype(BF16)
    w_uv = ukv[:, :, MLA_NOPE:].reshape(MLA_KV_RANK, MLA_WIDTH).astype(BF16)
    return {
        "w_in": w_in_r, "w_uq": w_uq, "w_uk": w_uk, "w_uv": w_uv,
        "g_qn": mla_g_qn[i].reshape(1, -1), "g_kvn": mla_g_kvn[i].reshape(1, -1),
        "w_pool": pool_w[i].astype(BF16), "s_pool": pool_scale[i].reshape(1, -1),
        "w_out": mixa_w_out[i].astype(BF16),
    }


def kernel(x_prompt, x_sample, cache_ckv, cache_krope, cache_k, cache_v, c, c_ctx, ada_w, ada_b, norm_pre, norm_post, mla_w_in, mla_g_qn, mla_g_kvn, mla_w_uq, mla_w_ukv, pool_w, pool_scale, mixa_w_out, swa_w_in, swa_sink, swa_w_out):
    n_p = BATCH * SEQ
    n_s = DEC_BATCH * DEC_SEQ
    xp = x_prompt.reshape(n_p, D_MODEL)
    xs = x_sample.reshape(n_s, D_MODEL)
    cvec = jnp.concatenate(
        [c_ctx[None], c, jnp.zeros((MOD_ROWS - 1 - DEC_BATCH, D_MODEL), F32)], axis=0)
    mods = _ada_call(cvec, ada_w, ada_b).reshape(DEPTH, MOD_ROWS, 3, D_MODEL)
    mla_tabs = _rope_tables(MLA_ROPE, LANES, MLA_NOPE)
    swa_tabs = _rope_tables(SWA_HEAD_DIM, SWA_HEAD_DIM, 0)
    ck_view = cache_k.reshape(DEC_BATCH, DEPTH // 2, PAST_LEN, SWA_KV_W)
    cv_view = cache_v.reshape(DEC_BATCH, DEPTH // 2, PAST_LEN, SWA_KV_W)
    mla_state = swa_state = ()
    pend_p = pend_s = None

    def stage(x, latent, out_part, in_part):
        res = list(_stage_call(x, latent, out_part, in_part))
        if out_part is not None:
            return res[0], res[1:]
        return x, res

    for l in range(DEPTH):
        i = l // 2
        mod = mods[l]
        gpre = norm_pre[l].reshape(1, D_MODEL)
        gpost = norm_post[l].reshape(1, D_MODEL)
        if l % 2 == 0:
            w = _prep_mla(i, mla_w_in, mla_g_qn, mla_g_kvn, mla_w_uq, mla_w_ukv, pool_w, pool_scale,
                          mixa_w_out)
            xp, (qp, kcp, vp, gap, vpp, gbp, *mla_state) = stage(
                xp, False, pend_p, _mla_in_part(n_p, mod, gpre, w, False, None, i, mla_state))
            xs, (qs, kcs, vs, gas, vps, gbs) = stage(
                xs, True, pend_s, _mla_in_part(n_s, mod, gpre, w, True, mla_tabs))
            ctx_kr = jnp.pad(cache_krope[:, i].reshape(DEC_BATCH * PAST_LEN, MLA_ROPE),
                             ((0, 0), (MLA_NOPE, LANES - MLA_NOPE - MLA_ROPE)))
            ctx = (cache_ckv, i, ctx_kr, w["w_uk"], w["w_uv"])
            ap = _mla_attn_call(qp, kcp, vp, gap, BATCH, SEQ)
            as_ = _mla_attn_call(qs, kcs, vs, gas, DEC_BATCH, DEC_SEQ, ctx)
            pend_p = _mla_out_part(n_p, ap, vpp, gbp, mod, gpost, w, False)
            pend_s = _mla_out_part(n_s, as_, vps, gbs, mod, gpost, w, True)
        else:
            w_in = swa_w_in[i].astype(BF16)
            w_out = swa_w_out[i].astype(BF16)
            sink = swa_sink[i]
            xp, (qp, kdp, vdp, gsp, *swa_state) = stage(
                xp, False, pend_p, _swa_in_part(n_p, mod, gpre, w_in, False, None, i, swa_state))
            xs, (qs, kds, vds, gss) = stage(
                xs, True, pend_s, _swa_in_part(n_s, mod, gpre, w_in, True, swa_tabs))

            ap = _swa_dense_call(sink, qp, kdp, vdp, gsp)
            as_ = _swa_band_call(sink, qs, kds, vds, ck_view, cv_view, i, gss)
            pend_p = _swa_out_part(ap, mod, gpost, w_out, False)
            pend_s = _swa_out_part(as_, mod, gpost, w_out, True)
    xp, _ = stage(xp, False, pend_p, None)
    xs, _ = stage(xs, True, pend_s, None)
    return (
        xp.reshape(BATCH, SEQ, D_MODEL),
        xs.reshape(DEC_BATCH, DEC_SEQ, D_MODEL),
        mla_state[0].reshape(BATCH, DEPTH // 2, SEQ, MLA_KV_RANK),
        mla_state[1].reshape(BATCH, DEPTH // 2, SEQ, MLA_ROPE),
        swa_state[0].reshape(BATCH, DEPTH // 2, SEQ, SWA_KV_HEADS, SWA_HEAD_DIM),
        swa_state[1].reshape(BATCH, DEPTH // 2, SEQ, SWA_KV_HEADS, SWA_HEAD_DIM),
    )
```
